```python
import math
import jax, jax.numpy as jnp
from jax import lax
import numpy as np

D_MODEL = 1024
BATCH = 8
SEQ = 4096
DEPTH = 2

N_MIXERS = 2
HEAD_DIM = 128
GDN_HEADS = D_MODEL // HEAD_DIM
GDN_CONV = 4
GDN_CHUNK = 64
NSA_Q_HEADS = D_MODEL // HEAD_DIM
NSA_KV_HEADS = NSA_Q_HEADS // 4
CMP_LEN = 32
CMP_STRIDE = 16
SLC_BLOCK = 64
SLC_TOPK = 16
WINDOW = 512
NSA_Q_BLOCK = 64
MEM_TOKENS = 256
MEM_HEADS = 4
NUM_BUCKETS = 32
MAX_DISTANCE = 128
D_FF = ((8 * D_MODEL // 3 + 255) // 256) * 256
FFN_CONV = 3
MEM_WIDTH = MEM_HEADS * HEAD_DIM
MIX_WIDTH = D_MODEL + MEM_WIDTH
GDN_COLS = 4 * GDN_HEADS * HEAD_DIM + 2 * GDN_HEADS
NSA_KV_WIDTH = NSA_KV_HEADS * HEAD_DIM
NSA_COLS = NSA_Q_HEADS * HEAD_DIM + 6 * NSA_KV_WIDTH + 3 * NSA_Q_HEADS
N_GDN = (DEPTH + 1) // 2
N_NSA = DEPTH // 2
EPS = 1e-6

kernel_name = 'hybrid_gdn_nsa_memory_convglu'


def rms_norm(x, w):
    xf = x.astype(jnp.float32)
    y = xf * lax.rsqrt(jnp.mean(xf * xf, axis=-1, keepdims=True) + EPS)
    return (y * w.astype(jnp.float32)).astype(x.dtype)


def l2norm(x):
    xf = x.astype(jnp.float32)
    return xf * lax.rsqrt(jnp.sum(xf * xf, axis=-1, keepdims=True) + EPS)


def causal_dwconv(x, w):
    K = w.shape[0]
    S = x.shape[1]
    xp = jnp.pad(x, ((0, 0), (K - 1, 0), (0, 0)))
    y = xp[:, 0:S] * w[0]
    for j in range(1, K):
        y = y + xp[:, j:j + S] * w[j]
    return y


def t5_bucket(dist):
    n = jnp.maximum(dist, 0)
    max_exact = NUM_BUCKETS // 2
    logv = jnp.log(jnp.maximum(n, 1).astype(jnp.float32) / max_exact) / math.log(MAX_DISTANCE / max_exact)
    large = max_exact + (logv * (NUM_BUCKETS - max_exact)).astype(jnp.int32)
    large = jnp.minimum(large, NUM_BUCKETS - 1)
    return jnp.where(n < max_exact, n, large)


def masked_softmax(s, mask):
    s = jnp.where(mask, s.astype(jnp.float32), -jnp.inf)
    m = jnp.max(s, axis=-1, keepdims=True)
    m = jnp.where(jnp.isfinite(m), m, 0.0)
    p = jnp.where(mask, jnp.exp(s - m), 0.0)
    return p / jnp.maximum(jnp.sum(p, axis=-1, keepdims=True), 1e-30)


def gated_delta_rule_chunked(q, k, v, g, beta):
    f32 = jnp.float32
    B, H, S, dk = q.shape
    dv = v.shape[-1]
    C = GDN_CHUNK
    N = S // C
    q = (q.astype(f32) * dk ** -0.5).reshape(B, H, N, C, dk)
    k = k.astype(f32).reshape(B, H, N, C, dk)
    v = v.astype(f32).reshape(B, H, N, C, dv)
    beta = beta.astype(f32).reshape(B, H, N, C)
    gc = jnp.cumsum(g.astype(f32).reshape(B, H, N, C), axis=-1)
    incl = jnp.tril(jnp.ones((C, C), bool))
    strict = jnp.tril(jnp.ones((C, C), bool), -1)
    diff = gc[..., :, None] - gc[..., None, :]
    decay = jnp.where(incl, jnp.exp(jnp.where(incl, diff, 0.0)), 0.0)
    kb = k * beta[..., None]
    L = jnp.where(strict, jnp.einsum('bhnid,bhnjd->bhnij', kb, k) * decay, 0.0)
    eye = jnp.eye(C, dtype=f32)
    T = lax.linalg.triangular_solve(eye + L, jnp.broadcast_to(eye, L.shape),
                                    left_side=True, lower=True, unit_diagonal=True)
    u = jnp.einsum('bhnij,bhnjd->bhnid', T, v * beta[..., None])
    w = jnp.einsum('bhnij,bhnjd->bhnid', T, kb * jnp.exp(gc)[..., None])
    a_intra = jnp.where(incl, jnp.einsum('bhnid,bhnjd->bhnij', q, k) * decay, 0.0)
    q_dec = q * jnp.exp(gc)[..., None]
    k_dec = k * jnp.exp(gc[..., -1:] - gc)[..., None]
    g_last = jnp.exp(gc[..., -1])

    def chunk_step(state, xs):
        u_n, w_n, a_n, qd_n, kd_n, gl_n = xs
        v_new = u_n - jnp.einsum('bhcd,bhde->bhce', w_n, state)
        o_n = jnp.einsum('bhcd,bhde->bhce', qd_n, state) + jnp.einsum('bhij,bhje->bhie', a_n, v_new)
        state = state * gl_n[..., None, None] + jnp.einsum('bhcd,bhce->bhde', kd_n, v_new)
        return state, o_n

    to_n = lambda t: jnp.moveaxis(t, 2, 0)
    state0 = jnp.zeros((B, H, dk, dv), f32)
    _, o = lax.scan(chunk_step, state0,
                    (to_n(u), to_n(w), to_n(a_intra), to_n(q_dec), to_n(k_dec), to_n(g_last)))
    return jnp.moveaxis(o, 0, 2).reshape(B, H, S, dv)


def gdn_mixer(cols, conv_w, a_log, dt_bias, norm_w):
    B, S, _ = cols.shape
    H, dh = GDN_HEADS, HEAD_DIM
    wd = H * dh
    qkv = jax.nn.silu(causal_dwconv(cols[..., :3 * wd], conv_w)).reshape(B, S, 3, H, dh)
    q, k, v = qkv[:, :, 0], qkv[:, :, 1], qkv[:, :, 2]
    z = cols[..., 3 * wd:4 * wd].reshape(B, S, H, dh)
    b = cols[..., 4 * wd:4 * wd + H]
    a = cols[..., 4 * wd + H:4 * wd + 2 * H]
    beta = jax.nn.sigmoid(b.astype(jnp.float32))
    g = -jnp.exp(a_log.astype(jnp.float32)) * jax.nn.softplus(a.astype(jnp.float32) + dt_bias.astype(jnp.float32))
    heads = lambda t: jnp.swapaxes(t, 1, 2)
    o = gated_delta_rule_chunked(heads(l2norm(q)), heads(l2norm(k)), heads(v.astype(jnp.float32)),
                                 heads(g), heads(beta))
    o = jnp.swapaxes(o, 1, 2)
    o = rms_norm(o, norm_w) * jax.nn.silu(z.astype(jnp.float32))
    return o.reshape(B, S, wd).astype(cols.dtype)


def compress_blocks(t, pos_emb, w1, w2):
    B, S, G, dh = t.shape
    n_cmp = (S - CMP_LEN) // CMP_STRIDE + 1
    idx = (np.arange(n_cmp, dtype=np.int32)[:, None] * CMP_STRIDE
           + np.arange(CMP_LEN, dtype=np.int32)[None, :])
    blk = t[:, idx] + pos_emb[:, None, :]
    flat = jnp.transpose(blk, (0, 3, 1, 2, 4)).reshape(B, G, n_cmp, CMP_LEN * dh)
    return jax.nn.silu(flat @ w1) @ w2


def nsa_mixer(cols, rel_bias, pos_k, w1_k, w2_k, pos_v, w1_v, w2_v):
    B, S, _ = cols.shape
    HQ, G, dh = NSA_Q_HEADS, NSA_KV_HEADS, HEAD_DIM
    R = HQ // G
    Q = NSA_Q_BLOCK
    scale = dh ** -0.5
    qw = HQ * dh
    q = cols[..., :qw]
    kvs = [cols[..., qw + j * NSA_KV_WIDTH:qw + (j + 1) * NSA_KV_WIDTH].reshape(B, S, G, dh) for j in range(6)]
    k_cmp, v_cmp, k_slc, v_slc, k_win, v_win = kvs
    gates = jax.nn.sigmoid(cols[..., qw + 6 * NSA_KV_WIDTH:].astype(jnp.float32)).reshape(B, S, HQ, 3)

    n_q = S // Q
    n_cmp = (S - CMP_LEN) // CMP_STRIDE + 1
    n_slc = S // SLC_BLOCK
    n_sel = min(SLC_TOPK, n_slc)

    kc = compress_blocks(k_cmp, pos_k, w1_k, w2_k)
    vc = compress_blocks(v_cmp, pos_v, w1_v, w2_v)
    c_start = np.arange(n_cmp, dtype=np.int32) * CMP_STRIDE
    c_end = c_start + CMP_LEN - 1
    s_start = np.arange(n_slc, dtype=np.int32) * SLC_BLOCK
    sel_map = ((c_start[:, None] < s_start[None, :] + SLC_BLOCK)
               & (s_start[None, :] <= c_end[:, None])).astype(np.float32)
    to_blocks = lambda t: jnp.swapaxes(t, 1, 2).reshape(B, G, n_slc, SLC_BLOCK, dh)
    ks, vs = to_blocks(k_slc), to_blocks(v_slc)
    pad = lambda t: jnp.pad(jnp.swapaxes(t, 1, 2), ((0, 0), (0, 0), (WINDOW, 0), (0, 0)))
    kw, vw = pad(k_win), pad(v_win)
    tab = rel_bias.reshape(NUM_BUCKETS, G, R)
    c_end_j = jnp.asarray(c_end)
    bi = jnp.arange(B)[:, None, None, None]
    gi = jnp.arange(G)[None, :, None, None]

    def head_bias(dist):
        return jnp.moveaxis(tab[t5_bucket(dist)], (-2, -1), (0, 1))

    qh = jnp.moveaxis(q.reshape(B, S, G, R, dh).transpose(0, 2, 3, 1, 4).reshape(B, G, R, n_q, Q, dh), 3, 0)
    gh = jnp.moveaxis(gates.reshape(B, S, G, R, 3).transpose(0, 2, 3, 1, 4).reshape(B, G, R, n_q, Q, 3), 3, 0)

    def step(args):
        qi, qb, gb = args
        s0 = qi * Q
        t = s0 + jnp.arange(Q, dtype=jnp.int32)
        dist_c = t[:, None] - c_end_j[None, :]
        sc = jnp.einsum('bgrqd,bgcd->bgrqc', qb, kc).astype(jnp.float32) * scale + head_bias(dist_c)
        p_c = masked_softmax(sc, dist_c >= 0)
        o_c = jnp.einsum('bgrqc,bgcd->bgrqd', p_c, vc.astype(jnp.float32))
        imp = jnp.einsum('bgqc,cs->bgqs', jnp.sum(p_c, axis=2), sel_map)
        jb = jnp.arange(n_slc, dtype=jnp.int32)[None, :]
        tb = (t // SLC_BLOCK)[:, None]
        forced = (jb == 0) | (jb == tb) | (jb == tb - 1)
        score = jnp.where(jb <= tb, jnp.where(forced, jnp.inf, imp), -jnp.inf)
        vals, idx = lax.top_k(score, n_sel)
        ok = vals > -jnp.inf
        kg = ks[bi, gi, idx].reshape(B, G, Q, n_sel * SLC_BLOCK, dh)
        vg = vs[bi, gi, idx].reshape(B, G, Q, n_sel * SLC_BLOCK, dh)
        pos = idx[..., None] * SLC_BLOCK + jnp.arange(SLC_BLOCK, dtype=jnp.int32)
        dist_s = t[:, None, None] - pos
        mask_s = (ok[..., None] & (dist_s >= 0)).reshape(B, G, Q, n_sel * SLC_BLOCK)
        bias_s = jnp.moveaxis(tab[t5_bucket(dist_s).reshape(B, G, Q, n_sel * SLC_BLOCK), gi], -1, 2)
        ss = jnp.einsum('bgrqd,bgqkd->bgrqk', qb, kg).astype(jnp.float32) * scale + bias_s
        p_s = masked_softmax(ss, mask_s[:, :, None])
        o_s = jnp.einsum('bgrqk,bgqkd->bgrqd', p_s, vg.astype(jnp.float32))
        kwb = lax.dynamic_slice_in_dim(kw, s0, Q + WINDOW, axis=2)
        vwb = lax.dynamic_slice_in_dim(vw, s0, Q + WINDOW, axis=2)
        kpos = s0 - WINDOW + jnp.arange(Q + WINDOW, dtype=jnp.int32)
        dist_w = t[:, None] - kpos[None, :]
        mask_w = (dist_w >= 0) & (dist_w < WINDOW) & (kpos[None, :] >= 0)
        sw = jnp.einsum('bgrqd,bgkd->bgrqk', qb, kwb).astype(jnp.float32) * scale + head_bias(dist_w)
        p_w = masked_softmax(sw, mask_w)
        o_w = jnp.einsum('bgrqk,bgkd->bgrqd', p_w, vwb.astype(jnp.float32))
        o = gb[..., 0:1] * o_c + gb[..., 1:2] * o_s + gb[..., 2:3] * o_w
        return o.astype(cols.dtype)

    out = lax.map(step, (jnp.arange(n_q, dtype=jnp.int32), qh, gh))
    return jnp.transpose(out, (1, 0, 4, 2, 3, 5)).reshape(B, S, HQ * dh)


def memory_attention(qm, mem_n, w_kv):
    B, S = qm.shape[:2]
    kv = (mem_n @ w_kv).reshape(B, -1, 2, MEM_HEADS, HEAD_DIM)
    k, v = kv[:, :, 0], kv[:, :, 1]
    s = jnp.einsum('bshd,bmhd->bhsm', qm, k).astype(jnp.float32) * HEAD_DIM ** -0.5
    p = jax.nn.softmax(s, axis=-1)
    o = jnp.einsum('bhsm,bmhd->bshd', p, v.astype(jnp.float32))
    return o.reshape(B, S, MEM_WIDTH).astype(qm.dtype)


def conv_glu(h, w_up, conv_w, conv_b, w_down):
    gate, val = jnp.split(h @ w_up, 2, axis=-1)
    gate = causal_dwconv(gate, conv_w) + conv_b
    return (jax.nn.silu(gate) * val) @ w_down


def setup_inputs(seed: int = 0) -> dict:
    key = jax.random.key(seed)
    ks = iter(jax.random.split(key, 32))
    f32 = jnp.float32
    nrm = lambda shape, scale: scale * jax.random.normal(next(ks), shape, f32)
    gain = lambda shape: 1.0 + nrm(shape, 0.02)
    x = nrm((BATCH, SEQ, D_MODEL), 1.0)
    mem = nrm((BATCH, MEM_TOKENS, D_MODEL), 1.0)
    rel_bias = nrm((NUM_BUCKETS, NSA_Q_HEADS), 0.2)
    norm_mix_w = gain((DEPTH, D_MODEL))
    norm_ffn_w = gain((DEPTH, D_MODEL))
    final_norm_w = gain((D_MODEL,))
    mem_norm_w = gain((DEPTH, D_MODEL))
    mem_w_kv = nrm((DEPTH, D_MODEL, 2 * MEM_WIDTH), D_MODEL ** -0.5)
    w_out = nrm((DEPTH, MIX_WIDTH, D_MODEL), MIX_WIDTH ** -0.5)
    gdn_w_in = nrm((N_GDN, D_MODEL, GDN_COLS + MEM_WIDTH), D_MODEL ** -0.5)
    gdn_conv_w = nrm((N_GDN, GDN_CONV, 3 * GDN_HEADS * HEAD_DIM), GDN_CONV ** -0.5)
    gdn_a_log = jnp.log(jax.random.uniform(next(ks), (N_GDN, GDN_HEADS), f32, 1.0, 16.0))
    dt = jnp.exp(jax.random.uniform(next(ks), (N_GDN, GDN_HEADS), f32, math.log(1e-3), math.log(1e-1)))
    gdn_dt_bias = dt + jnp.log(-jnp.expm1(-dt))
    gdn_norm_w = gain((N_GDN, HEAD_DIM))
    nsa_w_in = nrm((N_NSA, D_MODEL, NSA_COLS + MEM_WIDTH), D_MODEL ** -0.5)
    nsa_cmp_pos_k = nrm((N_NSA, CMP_LEN, HEAD_DIM), 0.1)
    nsa_cmp_w1_k = nrm((N_NSA, CMP_LEN * HEAD_DIM, HEAD_DIM), (CMP_LEN * HEAD_DIM) ** -0.5)
    nsa_cmp_w2_k = nrm((N_NSA, HEAD_DIM, HEAD_DIM), HEAD_DIM ** -0.5)
    nsa_cmp_pos_v = nrm((N_NSA, CMP_LEN, HEAD_DIM), 0.1)
    nsa_cmp_w1_v = nrm((N_NSA, CMP_LEN * HEAD_DIM, HEAD_DIM), (CMP_LEN * HEAD_DIM) ** -0.5)
    nsa_cmp_w2_v = nrm((N_NSA, HEAD_DIM, HEAD_DIM), HEAD_DIM ** -0.5)
    ffn_w_up = nrm((DEPTH, D_MODEL, 2 * D_FF), D_MODEL ** -0.5)
    ffn_conv_w = nrm((DEPTH, FFN_CONV, D_FF), FFN_CONV ** -0.5)
    ffn_conv_b = nrm((DEPTH, D_FF), 0.02)
    ffn_w_down = nrm((DEPTH, D_FF, D_MODEL), D_FF ** -0.5)
    return {'x': x, 'mem': mem, 'rel_bias': rel_bias, 'norm_mix_w': norm_mix_w,
            'norm_ffn_w': norm_ffn_w, 'final_norm_w': final_norm_w, 'mem_norm_w': mem_norm_w,
            'mem_w_kv': mem_w_kv, 'w_out': w_out, 'gdn_w_in': gdn_w_in, 'gdn_conv_w': gdn_conv_w,
            'gdn_a_log': gdn_a_log, 'gdn_dt_bias': gdn_dt_bias, 'gdn_norm_w': gdn_norm_w,
            'nsa_w_in': nsa_w_in, 'nsa_cmp_pos_k': nsa_cmp_pos_k, 'nsa_cmp_w1_k': nsa_cmp_w1_k,
            'nsa_cmp_w2_k': nsa_cmp_w2_k, 'nsa_cmp_pos_v': nsa_cmp_pos_v, 'nsa_cmp_w1_v': nsa_cmp_w1_v,
            'nsa_cmp_w2_v': nsa_cmp_w2_v, 'ffn_w_up': ffn_w_up, 'ffn_conv_w': ffn_conv_w,
            'ffn_conv_b': ffn_conv_b, 'ffn_w_down': ffn_w_down}


def reference(x, mem, rel_bias, norm_mix_w, norm_ffn_w, final_norm_w, mem_norm_w, mem_w_kv, w_out,
              gdn_w_in, gdn_conv_w, gdn_a_log, gdn_dt_bias, gdn_norm_w, nsa_w_in, nsa_cmp_pos_k,
              nsa_cmp_w1_k, nsa_cmp_w2_k, nsa_cmp_pos_v, nsa_cmp_w1_v, nsa_cmp_w2_v, ffn_w_up,
              ffn_conv_w, ffn_conv_b, ffn_w_down):
    B, S, _ = x.shape
    for i in range(DEPTH):
        h = rms_norm(x, norm_mix_w[i])
        mem_n = rms_norm(mem, mem_norm_w[i])
        j = i // N_MIXERS
        if i % N_MIXERS == 0:
            proj = h @ gdn_w_in[j]
            mix = gdn_mixer(proj[..., :GDN_COLS], gdn_conv_w[j], gdn_a_log[j], gdn_dt_bias[j], gdn_norm_w[j])
            qm = proj[..., GDN_COLS:]
        else:
            proj = h @ nsa_w_in[j]
            mix = nsa_mixer(proj[..., :NSA_COLS], rel_bias, nsa_cmp_pos_k[j], nsa_cmp_w1_k[j],
                            nsa_cmp_w2_k[j], nsa_cmp_pos_v[j], nsa_cmp_w1_v[j], nsa_cmp_w2_v[j])
            qm = proj[..., NSA_COLS:]
        mo = memory_attention(qm.reshape(B, S, MEM_HEADS, HEAD_DIM), mem_n, mem_w_kv[i])
        x = x + jnp.concatenate([mix, mo], axis=-1) @ w_out[i]
        x = x + conv_glu(rms_norm(x, norm_ffn_w[i]), ffn_w_up[i], ffn_conv_w[i], ffn_conv_b[i], ffn_w_down[i])
    return rms_norm(x, final_norm_w)
```

```python
import functools
import math

import jax
import jax.numpy as jnp
import numpy as np
from jax import lax
from jax.experimental import pallas as pl
from jax.experimental.pallas import tpu as pltpu

F32 = jnp.float32
BF16 = jnp.bfloat16

HEAD_DIM = 128
GDN_CONV = 4
GDN_CHUNK = 64
GDN_HEADS_PER_STEP = 4
NSA_GROUP = 4
CMP_LEN = 32
CMP_STRIDE = 16
SLC_BLOCK = 64
SLC_TOPK = 16
WINDOW = 512
NSA_QB = 256
MEM_HEADS = 4
NUM_BUCKETS = 32
MAX_DISTANCE = 128
FFN_CONV = 3
EPS = 1e-6
NEG = -1e30
SEL_PENALTY = 32768.0
VMEM_LIMIT = 56 * 1024 * 1024


def _cparams(sem):
    return pltpu.CompilerParams(dimension_semantics=sem, vmem_limit_bytes=VMEM_LIMIT)


def _dot(a, b):
    return jnp.dot(a, b, preferred_element_type=F32)


def _dot_nt(a, b):
    return lax.dot_general(a, b, (((1,), (1,)), ((), ())), preferred_element_type=F32)


def _silu(x):
    return x * (1.0 / (1.0 + jnp.exp(-x)))


def _sigmoid(x):
    return 1.0 / (1.0 + jnp.exp(-x))


def _softplus(x):
    return jnp.maximum(x, 0.0) + jnp.log(1.0 + jnp.exp(-jnp.abs(x)))


def _split3(x):
    h = x.astype(BF16)
    r = x - h.astype(F32)
    m = r.astype(BF16)
    l = (r - m.astype(F32)).astype(BF16)
    return h, m, l


def _norm_mm_kernel(x_ref, nw_ref, w_ref, waux_ref, o_ref, oaux_ref, xn_ref):
    @pl.when(pl.program_id(1) == 0)
    def _():
        x = x_ref[...]
        ms = jnp.mean(x * x, axis=-1, keepdims=True)
        xn = (x * lax.rsqrt(ms + EPS) * nw_ref[...]).astype(BF16)
        xn_ref[...] = xn
        oaux_ref[...] = _dot(xn, waux_ref[...])

    o_ref[...] = _dot(xn_ref[...], w_ref[...]).astype(o_ref.dtype)


def norm_matmul(x, nw, w, waux, *, tm, tn):
    t, d = x.shape
    n = w.shape[1]
    na = waux.shape[1]
    return pl.pallas_call(
        _norm_mm_kernel,
        grid=(t // tm, n // tn),
        in_specs=[
            pl.BlockSpec((tm, d), lambda i, j: (i, 0)),
            pl.BlockSpec((1, d), lambda i, j: (0, 0)),
            pl.BlockSpec((d, tn), lambda i, j: (0, j)),
            pl.BlockSpec((d, na), lambda i, j: (0, 0)),
        ],
        out_specs=[
            pl.BlockSpec((tm, tn), lambda i, j: (i, j)),
            pl.BlockSpec((tm, na), lambda i, j: (i, 0)),
        ],
        out_shape=[jax.ShapeDtypeStruct((t, n), BF16), jax.ShapeDtypeStruct((t, na), F32)],
        scratch_shapes=[pltpu.VMEM((tm, d), BF16)],
        compiler_params=_cparams(("parallel", "arbitrary")),
        name="norm_matmul",
    )(x, nw.reshape(1, d), w, waux)


def _mem_attn_kernel(q_ref, kv_ref, o_ref):
    scale = HEAD_DIM ** -0.5
    mw = MEM_HEADS * HEAD_DIM
    for h in range(MEM_HEADS):
        lo = h * HEAD_DIM
        q = q_ref[:, lo:lo + HEAD_DIM]
        k = kv_ref[:, lo:lo + HEAD_DIM]
        v = kv_ref[:, mw + lo:mw + lo + HEAD_DIM]
        s = _dot_nt(q, k) * scale
        m = jnp.max(s, axis=-1, keepdims=True)
        p = jnp.exp(s - m)
        l = jnp.sum(p, axis=-1, keepdims=True)
        o = _dot(p.astype(BF16), v) / l
        o_ref[:, lo:lo + HEAD_DIM] = o.astype(o_ref.dtype)


def mem_attention(proj, kv, *, batch, seq, q_col_block, ts):
    mw = MEM_HEADS * HEAD_DIM
    m_tok = kv.shape[0] // batch
    nt = seq // ts
    return pl.pallas_call(
        _mem_attn_kernel,
        grid=(batch, nt),
        in_specs=[
            pl.BlockSpec((ts, mw), lambda b, i: (b * nt + i, q_col_block)),
            pl.BlockSpec((m_tok, 2 * mw), lambda b, i: (b, 0)),
        ],
        out_specs=pl.BlockSpec((ts, mw), lambda b, i: (b * nt + i, 0)),
        out_shape=jax.ShapeDtypeStruct((batch * seq, mw), BF16),
        compiler_params=_cparams(("parallel", "parallel")),
        name="mem_attention",
    )(proj, kv)


def _out_proj_kernel(x_ref, a_ref, b_ref, wa_ref, wb_ref, o_ref):
    o_ref[...] = x_ref[...] + _dot(a_ref[...], wa_ref[...]) + _dot(b_ref[...], wb_ref[...])


def out_proj(x, a, b, wa, wb, *, tm):
    t, d = x.shape
    ka, kb = a.shape[1], b.shape[1]
    return pl.pallas_call(
        _out_proj_kernel,
        grid=(t // tm,),
        in_specs=[
            pl.BlockSpec((tm, d), lambda i: (i, 0)),
            pl.BlockSpec((tm, ka), lambda i: (i, 0)),
            pl.BlockSpec((tm, kb), lambda i: (i, 0)),
            pl.BlockSpec((ka, d), lambda i: (0, 0)),
            pl.BlockSpec((kb, d), lambda i: (0, 0)),
        ],
        out_specs=pl.BlockSpec((tm, d), lambda i: (i, 0)),
        out_shape=jax.ShapeDtypeStruct((t, d), F32),
        compiler_params=_cparams(("parallel",)),
        name="out_proj",
    )(x, a, b, wa, wb)


def _ffn_up_kernel(x_ref, nw_ref, wg_ref, wv_ref, cw_ref, cb_ref, o_ref, xn_ref, gs_ref, carry_ref,
                   *, tiles_per_seq):
    i = pl.program_id(0)
    j = pl.program_id(1)
    tm = x_ref.shape[0]

    @pl.when(j == 0)
    def _():
        x = x_ref[...]
        ms = jnp.mean(x * x, axis=-1, keepdims=True)
        xn_ref[...] = (x * lax.rsqrt(ms + EPS) * nw_ref[...]).astype(BF16)

    xn = xn_ref[...]
    g = _dot(xn, wg_ref[...])
    v = _dot(xn, wv_ref[...])

    @pl.when(i % tiles_per_seq == 0)
    def _():
        gs_ref[0:8, :] = jnp.zeros((8, gs_ref.shape[1]), F32)

    @pl.when(i % tiles_per_seq != 0)
    def _():
        gs_ref[0:8, :] = carry_ref[j]

    gs_ref[8:, :] = g
    carry_ref[j] = g[tm - 8:, :]
    cw = cw_ref[...]
    conv = (gs_ref[6:6 + tm, :] * cw[0:1, :] + gs_ref[7:7 + tm, :] * cw[1:2, :]
            + g * cw[2:3, :] + cb_ref[...])
    o_ref[...] = (_silu(conv) * v).astype(o_ref.dtype)


def ffn_up(x, nw, wg, wv, cw, cb, *, seq, tm, tn):
    t, d = x.shape
    dff = wg.shape[1]
    nj = dff // tn
    kern = functools.partial(_ffn_up_kernel, tiles_per_seq=seq // tm)
    return pl.pallas_call(
        kern,
        grid=(t // tm, nj),
        in_specs=[
            pl.BlockSpec((tm, d), lambda i, j: (i, 0)),
            pl.BlockSpec((1, d), lambda i, j: (0, 0)),
            pl.BlockSpec((d, tn), lambda i, j: (0, j)),
            pl.BlockSpec((d, tn), lambda i, j: (0, j)),
            pl.BlockSpec((8, tn), lambda i, j: (0, j)),
            pl.BlockSpec((1, tn), lambda i, j: (0, j)),
        ],
        out_specs=pl.BlockSpec((tm, tn), lambda i, j: (i, j)),
        out_shape=jax.ShapeDtypeStruct((t, dff), BF16),
        scratch_shapes=[
            pltpu.VMEM((tm, d), BF16),
            pltpu.VMEM((tm + 8, tn), F32),
            pltpu.VMEM((nj, 8, tn), F32),
        ],
        compiler_params=_cparams(("arbitrary", "arbitrary")),
        name="ffn_up",
    )(x, nw.reshape(1, d), wg, wv, jnp.pad(cw, ((0, 8 - cw.shape[0]), (0, 0))), cb.reshape(1, dff))


def _ffn_down_kernel(x_ref, a_ref, w_ref, fw_ref, o_ref, *, final_norm):
    y = x_ref[...] + _dot(a_ref[...], w_ref[...])
    if final_norm:
        ms = jnp.mean(y * y, axis=-1, keepdims=True)
        y = y * lax.rsqrt(ms + EPS) * fw_ref[...]
    o_ref[...] = y


def ffn_down(x, a, w, fw, *, tm, final_norm):
    t, d = x.shape
    k = a.shape[1]
    return pl.pallas_call(
        functools.partial(_ffn_down_kernel, final_norm=final_norm),
        grid=(t // tm,),
        in_specs=[
            pl.BlockSpec((tm, d), lambda i: (i, 0)),
            pl.BlockSpec((tm, k), lambda i: (i, 0)),
            pl.BlockSpec((k, d), lambda i: (0, 0)),
            pl.BlockSpec((1, d), lambda i: (0, 0)),
        ],
        out_specs=pl.BlockSpec((tm, d), lambda i: (i, 0)),
        out_shape=jax.ShapeDtypeStruct((t, d), F32),
        compiler_params=_cparams(("parallel",)),
        name="ffn_down",
    )(x, a, w, fw.reshape(1, d))


def _gdn_kernel(q_ref, k_ref, v_ref, z_ref, gt_ref, cw_ref, arow_ref, dtrow_ref, nw_ref, o_ref,
                qs_ref, ks_ref, vs_ref, state_ref, *, n_chunks):
    hp = GDN_HEADS_PER_STEP
    dh = HEAD_DIM
    c = GDN_CHUNK
    lblk = q_ref.shape[0]
    sb = pl.program_id(2)

    @pl.when(sb == 0)
    def _():
        zero8 = jnp.zeros((8, hp * dh), F32)
        qs_ref[0:8, :] = zero8
        ks_ref[0:8, :] = zero8
        vs_ref[0:8, :] = zero8
        state_ref[...] = jnp.zeros(state_ref.shape, F32)

    for idx, (src, dst) in enumerate(((q_ref, qs_ref), (k_ref, ks_ref), (v_ref, vs_ref))):
        dst[8:, :] = src[...].astype(F32)
        w = cw_ref[idx]
        y = (dst[5:5 + lblk, :] * w[0:1, :] + dst[6:6 + lblk, :] * w[1:2, :]
             + dst[7:7 + lblk, :] * w[2:3, :] + dst[8:8 + lblk, :] * w[3:4, :])
        tail = dst[lblk:lblk + 8, :]
        dst[8:, :] = _silu(y)
        dst[0:8, :] = tail

    row = lax.broadcasted_iota(jnp.int32, (c, c), 0)
    col = lax.broadcasted_iota(jnp.int32, (c, c), 1)
    incl = row >= col
    strict = row > col
    tril = jnp.where(incl, 1.0, 0.0).astype(BF16)
    arow = arow_ref[0]
    dtrow = dtrow_ref[0]
    nw = nw_ref[...]

    def chunk_body(ci, carry):
        r0 = pl.multiple_of(ci * c, c)
        gt = gt_ref[pl.ds(r0, c), :]
        beta_all = _sigmoid(gt)
        g_all = -arow * _softplus(gt + dtrow)
        gh, gm, gl = _split3(g_all)
        gc_all = _dot(tril, gh) + _dot(tril, gm) + _dot(tril, gl)
        gc_t = gc_all.T
        for r in range(hp):
            lo = r * dh
            qh = qs_ref[pl.ds(r0 + 8, c), lo:lo + dh]
            kh = ks_ref[pl.ds(r0 + 8, c), lo:lo + dh]
            vh = vs_ref[pl.ds(r0 + 8, c), lo:lo + dh]
            qn = qh * lax.rsqrt(jnp.sum(qh * qh, axis=-1, keepdims=True) + EPS) * (dh ** -0.5)
            kn = kh * lax.rsqrt(jnp.sum(kh * kh, axis=-1, keepdims=True) + EPS)
            beta = beta_all[:, r:r + 1]
            gc_col = gc_all[:, hp + r:hp + r + 1]
            gc_row = gc_t[hp + r:hp + r + 1, :]
            gc_last = gc_row[:, c - 1:c]
            decay = jnp.where(incl, jnp.exp(jnp.where(incl, gc_col - gc_row, 0.0)), 0.0)
            eg = jnp.exp(gc_col)
            kb = kn * beta
            kn_b = kn.astype(BF16)
            lmat = jnp.where(strict, _dot_nt(kb.astype(BF16), kn_b) * decay, 0.0)
            a_intra = jnp.where(incl, _dot_nt(qn.astype(BF16), kn_b) * decay, 0.0)
            x = jnp.concatenate([vh * beta, kb * eg], axis=1)
            mp = -lmat
            for it in range(6):
                mp_b = mp.astype(BF16)
                x = x + _dot(mp_b, x.astype(BF16))
                if it < 5:
                    mp = _dot(mp_b, mp_b)
            u = x[:, :dh]
            w = x[:, dh:]
            state = state_ref[r]
            state_b = state.astype(BF16)
            v_new = u - _dot(w.astype(BF16), state_b)
            v_new_b = v_new.astype(BF16)
            o = _dot((qn * eg).astype(BF16), state_b) + _dot(a_intra.astype(BF16), v_new_b)
            k_dec = kn * jnp.exp(gc_last - gc_col)
            state_ref[r] = state * jnp.exp(gc_last) + _dot(k_dec.T.astype(BF16), v_new_b)
            ms = jnp.mean(o * o, axis=-1, keepdims=True)
            zz = z_ref[pl.ds(r0, c), lo:lo + dh].astype(F32)
            o = o * lax.rsqrt(ms + EPS) * nw * _silu(zz)
            o_ref[pl.ds(r0, c), lo:lo + dh] = o.astype(o_ref.dtype)
        return carry

    lax.fori_loop(0, n_chunks, chunk_body, 0)


def gdn_mixer(proj, gates, conv_w, a_log, dt_bias, norm_w, *, batch, seq, lblk):
    hp = GDN_HEADS_PER_STEP
    dh = HEAD_DIM
    n_heads = a_log.shape[0]
    ng = n_heads // hp
    wd = n_heads * dh
    bw = hp * dh
    nsb = seq // lblk
    arow = jnp.zeros((ng, 1, 128), F32).at[:, 0, hp:2 * hp].set(jnp.exp(a_log.astype(F32)).reshape(ng, hp))
    dtrow = jnp.zeros((ng, 1, 128), F32).at[:, 0, hp:2 * hp].set(dt_bias.astype(F32).reshape(ng, hp))
    cw = conv_w.reshape(GDN_CONV, 3, ng, bw).transpose(1, 2, 0, 3)
    cw = jnp.pad(cw, ((0, 0), (0, 0), (0, 8 - GDN_CONV), (0, 0)))
    kern = functools.partial(_gdn_kernel, n_chunks=lblk // GDN_CHUNK)
    row_map = lambda off: (lambda b, g, s: (b * nsb + s, off + g))
    return pl.pallas_call(
        kern,
        grid=(batch, ng, nsb),
        in_specs=[
            pl.BlockSpec((lblk, bw), row_map(0)),
            pl.BlockSpec((lblk, bw), row_map(ng)),
            pl.BlockSpec((lblk, bw), row_map(2 * ng)),
            pl.BlockSpec((lblk, bw), row_map(3 * ng)),
            pl.BlockSpec((lblk, 128), row_map(0)),
            pl.BlockSpec((3, None, 8, bw), lambda b, g, s: (0, g, 0, 0)),
            pl.BlockSpec((None, 1, 128), lambda b, g, s: (g, 0, 0)),
            pl.BlockSpec((None, 1, 128), lambda b, g, s: (g, 0, 0)),
            pl.BlockSpec((1, dh), lambda b, g, s: (0, 0)),
        ],
        out_specs=pl.BlockSpec((lblk, bw), row_map(0)),
        out_shape=jax.ShapeDtypeStruct((batch * seq, wd), BF16),
        scratch_shapes=[
            pltpu.VMEM((lblk + 8, bw), F32),
            pltpu.VMEM((lblk + 8, bw), F32),
            pltpu.VMEM((lblk + 8, bw), F32),
            pltpu.VMEM((hp, dh, dh), F32),
        ],
        compiler_params=_cparams(("parallel", "parallel", "arbitrary")),
        name="gdn_mixer",
    )(proj, proj, proj, proj, gates, cw, arow, dtrow, norm_w.reshape(1, dh).astype(F32))


def _nsa_compress_kernel(t_ref, pos_ref, w1_ref, w2_ref, o_ref):
    nc = t_ref.shape[0] // CMP_STRIDE
    dh = HEAD_DIM
    acc_a = jnp.zeros((nc, dh), F32)
    acc_b = jnp.zeros((nc, dh), F32)
    for i in range(CMP_STRIDE):
        xi = t_ref[pl.ds(i, nc, stride=CMP_STRIDE), :]
        xa = (xi + pos_ref[i:i + 1, :]).astype(BF16)
        xb = (xi + pos_ref[CMP_STRIDE + i:CMP_STRIDE + i + 1, :]).astype(BF16)
        acc_a = acc_a + _dot(xa, w1_ref[i * dh:(i + 1) * dh, :])
        acc_b = acc_b + _dot(xb, w1_ref[(CMP_STRIDE + i) * dh:(CMP_STRIDE + i + 1) * dh, :])
    h = acc_a + jnp.concatenate([acc_b[1:, :], acc_b[:1, :]], axis=0)
    o_ref[...] = _dot(_silu(h).astype(BF16), w2_ref[...]).astype(o_ref.dtype)


def nsa_compress(aux, pos, w1, w2, *, batch, seq, n_groups):
    dh = HEAD_DIM
    nc = seq // CMP_STRIDE
    return pl.pallas_call(
        _nsa_compress_kernel,
        grid=(batch, 2, n_groups),
        in_specs=[
            pl.BlockSpec((seq, dh), lambda b, j, g: (b, j * n_groups + g)),
            pl.BlockSpec((None, CMP_LEN, dh), lambda b, j, g: (j, 0, 0)),
            pl.BlockSpec((None, CMP_LEN * dh, dh), lambda b, j, g: (j, 0, 0)),
            pl.BlockSpec((None, dh, dh), lambda b, j, g: (j, 0, 0)),
        ],
        out_specs=pl.BlockSpec((None, None, None, nc, dh), lambda b, j, g: (j, b, g, 0, 0)),
        out_shape=jax.ShapeDtypeStruct((2, batch, n_groups, nc, dh), BF16),
        compiler_params=_cparams(("parallel", "parallel", "parallel")),
        name="nsa_compress",
    )(aux, pos, w1, w2)


def _t5_bucket_np(n):
    max_exact = NUM_BUCKETS // 2
    nf = np.maximum(n, 1).astype(np.float32)
    logv = np.log(nf / np.float32(max_exact)) / np.float32(math.log(MAX_DISTANCE / max_exact))
    large = max_exact + (logv * np.float32(NUM_BUCKETS - max_exact)).astype(np.int32)
    large = np.minimum(large, NUM_BUCKETS - 1)
    return np.where(n < max_exact, n, large).astype(np.int32)


def _nsa_bias_tables(rel_bias, seq):
    qb = NSA_QB
    nc = seq // CMP_STRIDE
    n_heads = rel_bias.shape[1]
    tab = rel_bias.astype(F32)

    def lookup(dist):
        b = _t5_bucket_np(np.maximum(dist, 0))
        vals = jnp.moveaxis(jnp.take(tab, jnp.asarray(b), axis=0), -1, 0)
        return jnp.where(jnp.asarray(dist >= 0)[None], vals, NEG)

    q = np.arange(qb, dtype=np.int32)
    d0 = lookup(q[:, None] - q[None, :])
    d1 = lookup(q[:, None] - q[None, :] + qb)
    dtab = jnp.stack([d0, d1], axis=1)
    nw = 2 * qb // CMP_STRIDE
    x = np.arange(nw, dtype=np.int32)
    near = lookup(q[None, :] - CMP_STRIDE * x[:, None] + (qb - CMP_LEN + 1))
    far = jnp.broadcast_to(tab[NUM_BUCKETS - 1][:, None, None], (n_heads, nc, qb))
    future = jnp.full((n_heads, nc, qb), NEG, F32)
    tt = jnp.concatenate([far, near, future], axis=1)
    return dtab, tt


def _sel_map_t(seq):
    nc = seq // CMP_STRIDE
    n_slc = seq // SLC_BLOCK
    c_start = np.arange(nc, dtype=np.int32) * CMP_STRIDE
    c_end = c_start + CMP_LEN - 1
    s_start = np.arange(n_slc, dtype=np.int32) * SLC_BLOCK
    m = (c_start[None, :] < s_start[:, None] + SLC_BLOCK) & (s_start[:, None] <= c_end[None, :])
    return m.astype(np.float32)


def _flash_step(s_list, v_chunk, m_ref, l_ref, acc_ref):
    qb = NSA_QB
    for r, s in enumerate(s_list):
        rows = slice(r * qb, (r + 1) * qb)
        m_old = m_ref[rows, :]
        m_new = jnp.maximum(m_old, jnp.max(s, axis=-1, keepdims=True))
        alpha = jnp.exp(m_old - m_new)
        p = jnp.exp(s - m_new)
        l_ref[rows, :] = alpha * l_ref[rows, :] + jnp.sum(p, axis=-1, keepdims=True)
        acc_ref[rows, :] = alpha * acc_ref[rows, :] + _dot(p.astype(BF16), v_chunk)
        m_ref[rows, :] = m_new


def _nsa_attn_kernel(c31_ref, q_ref, ksl_ref, vsl_ref, kw_ref, vw_ref, kc_ref, vc_ref, gt_ref,
                     dtab_ref, tt_ref, smt_ref, o_ref,
                     kaug_ref, qaug_ref, vct_ref, sc_ref, m_ref, l_ref, acc_ref, osel_ref,
                     *, n_sel):
    qb = NSA_QB
    dh = HEAD_DIM
    grp = NSA_GROUP
    scale = dh ** -0.5
    g = pl.program_id(1)
    qi = pl.program_id(2)
    seq = ksl_ref.shape[0]
    nc = kc_ref.shape[0]
    n_slc = seq // SLC_BLOCK
    blocks_per_chunk = qb // SLC_BLOCK

    @pl.when(qi == 0)
    def _():
        kaug_ref[:, 0:dh] = ksl_ref[...]
        rb = lax.broadcasted_iota(jnp.int32, (seq, dh), 0) // SLC_BLOCK
        cb = lax.broadcasted_iota(jnp.int32, (seq, dh), 1)
        kaug_ref[:, dh:2 * dh] = jnp.where(rb == cb, -SEL_PENALTY, 0.0).astype(BF16)
        vct_ref[...] = vc_ref[...].astype(F32).T.astype(BF16)

    for r in range(grp):
        qaug_ref[r * qb:(r + 1) * qb, 0:dh] = q_ref[:, r * dh:(r + 1) * dh]
    q4 = qaug_ref[:, 0:dh]

    st = _dot_nt(kc_ref[...], q4)
    start = pl.multiple_of(nc + CMP_STRIDE - (qb // CMP_STRIDE) * qi, CMP_STRIDE)
    psum = jnp.zeros((nc, qb), F32)
    o_cmp = []
    for r in range(grp):
        bias = tt_ref[r, pl.ds(start, nc), :]
        s = st[:, r * qb:(r + 1) * qb] * scale + bias
        valid = bias > 0.5 * NEG
        m = jnp.max(s, axis=0, keepdims=True)
        p = jnp.where(valid, jnp.exp(s - m), 0.0)
        p = p / jnp.maximum(jnp.sum(p, axis=0, keepdims=True), 1e-30)
        psum = psum + p
        o_cmp.append(_dot(vct_ref[...], p.astype(BF16)).T)

    ph = psum.astype(BF16)
    pl_ = (psum - ph.astype(F32)).astype(BF16)
    smt = smt_ref[...]
    imp = _dot(smt, ph) + _dot(smt, pl_)
    jb = lax.broadcasted_iota(jnp.int32, (n_slc, qb), 0)
    tb = qi * blocks_per_chunk + lax.broadcasted_iota(jnp.int32, (n_slc, qb), 1) // SLC_BLOCK
    forced = (jb == 0) | (jb == tb) | (jb == tb - 1)
    score = jnp.where(jb <= tb, jnp.where(forced, jnp.inf, imp), -jnp.inf)
    sc_ref[...] = score

    def rank_body(j, rank):
        rowv = jnp.broadcast_to(sc_ref[pl.ds(j, 1), :], (n_slc, qb))
        ahead = (rowv > score) | ((rowv == score) & (jb > j))
        return rank + jnp.where(ahead, 1.0, 0.0)

    rank = lax.fori_loop(0, n_slc, rank_body, jnp.zeros((n_slc, qb), F32))
    notsel = jnp.where((rank < n_sel) & (score > -jnp.inf), 0.0, 1.0)
    notsel = jnp.concatenate([notsel, jnp.zeros((dh - n_slc, qb), F32)], axis=0).T
    notsel = notsel.astype(BF16)
    for r in range(grp):
        qaug_ref[r * qb:(r + 1) * qb, dh:2 * dh] = notsel

    c31 = [c31_ref[g * grp + r] for r in range(grp)]

    def reset():
        m_ref[...] = jnp.full(m_ref.shape, NEG, F32)
        l_ref[...] = jnp.zeros(l_ref.shape, F32)
        acc_ref[...] = jnp.zeros(acc_ref.shape, F32)

    def heads(s, add):
        return [s[r * qb:(r + 1) * qb, :] * scale + add(r) for r in range(grp)]

    reset()
    qa = qaug_ref[...]

    def far_body(kc, carry):
        k0 = pl.multiple_of(kc * qb, qb)
        s = _dot_nt(qa, kaug_ref[pl.ds(k0, qb), :])
        _flash_step(heads(s, lambda r: c31[r]), vsl_ref[pl.ds(k0, qb), :], m_ref, l_ref, acc_ref)
        return carry

    lax.fori_loop(0, jnp.maximum(qi - 1, 0), far_body, 0)

    @pl.when(qi >= 1)
    def _():
        k0 = pl.multiple_of((qi - 1) * qb, qb)
        s = _dot_nt(qa, kaug_ref[pl.ds(k0, qb), :])
        _flash_step(heads(s, lambda r: dtab_ref[r, 1]), vsl_ref[pl.ds(k0, qb), :], m_ref, l_ref, acc_ref)

    k0 = pl.multiple_of(qi * qb, qb)
    s = _dot_nt(qa, kaug_ref[pl.ds(k0, qb), :])
    _flash_step(heads(s, lambda r: dtab_ref[r, 0]), vsl_ref[pl.ds(k0, qb), :], m_ref, l_ref, acc_ref)
    osel_ref[...] = acc_ref[...] / l_ref[...]

    reset()

    @pl.when(qi >= 2)
    def _():
        k0 = pl.multiple_of((qi - 2) * qb, qb)
        s = _dot_nt(q4, kw_ref[pl.ds(k0, qb), :])
        rq = lax.broadcasted_iota(jnp.int32, (qb, qb), 0)
        ck = lax.broadcasted_iota(jnp.int32, (qb, qb), 1)
        inwin = ck > rq
        _flash_step(heads(s, lambda r: jnp.where(inwin, c31[r], NEG)), vw_ref[pl.ds(k0, qb), :],
                    m_ref, l_ref, acc_ref)

    @pl.when(qi >= 1)
    def _():
        k0 = pl.multiple_of((qi - 1) * qb, qb)
        s = _dot_nt(q4, kw_ref[pl.ds(k0, qb), :])
        _flash_step(heads(s, lambda r: dtab_ref[r, 1]), vw_ref[pl.ds(k0, qb), :], m_ref, l_ref, acc_ref)

    k0 = pl.multiple_of(qi * qb, qb)
    s = _dot_nt(q4, kw_ref[pl.ds(k0, qb), :])
    _flash_step(heads(s, lambda r: dtab_ref[r, 0]), vw_ref[pl.ds(k0, qb), :], m_ref, l_ref, acc_ref)

    gates = _sigmoid(gt_ref[...])
    for r in range(grp):
        rows = slice(r * qb, (r + 1) * qb)
        o_win = acc_ref[rows, :] / l_ref[rows, :]
        o = (gates[:, 3 * r:3 * r + 1] * o_cmp[r] + gates[:, 3 * r + 1:3 * r + 2] * osel_ref[rows, :]
             + gates[:, 3 * r + 2:3 * r + 3] * o_win)
        o_ref[:, r * dh:(r + 1) * dh] = o.astype(o_ref.dtype)


def nsa_attention(proj, aux, kvc, rel_bias, *, batch, seq, n_groups):
    qb = NSA_QB
    dh = HEAD_DIM
    grp = NSA_GROUP
    nq = seq // qb
    nc = seq // CMP_STRIDE
    n_slc = seq // SLC_BLOCK
    n_sel = min(SLC_TOPK, n_slc)
    dtab, tt = _nsa_bias_tables(rel_bias, seq)
    smt = jnp.asarray(_sel_map_t(seq), BF16)
    c31 = rel_bias[NUM_BUCKETS - 1].astype(F32)
    kv_base = n_groups * grp
    kv_map = lambda j: (lambda b, g, i, c: (b, kv_base + j * n_groups + g))
    grid_spec = pltpu.PrefetchScalarGridSpec(
        num_scalar_prefetch=1,
        grid=(batch, n_groups, nq),
        in_specs=[
            pl.BlockSpec((qb, grp * dh), lambda b, g, i, c: (b * nq + i, g)),
            pl.BlockSpec((seq, dh), kv_map(0)),
            pl.BlockSpec((seq, dh), kv_map(1)),
            pl.BlockSpec((seq, dh), kv_map(2)),
            pl.BlockSpec((seq, dh), kv_map(3)),
            pl.BlockSpec((None, None, None, nc, dh), lambda b, g, i, c: (0, b, g, 0, 0)),
            pl.BlockSpec((None, None, None, nc, dh), lambda b, g, i, c: (1, b, g, 0, 0)),
            pl.BlockSpec((qb, 128), lambda b, g, i, c: (b * nq + i, 2 * n_groups + g)),
            pl.BlockSpec((grp, 2, qb, qb), lambda b, g, i, c: (g, 0, 0, 0)),
            pl.BlockSpec((grp, tt.shape[1], qb), lambda b, g, i, c: (g, 0, 0)),
            pl.BlockSpec((n_slc, nc), lambda b, g, i, c: (0, 0)),
        ],
        out_specs=pl.BlockSpec((qb, grp * dh), lambda b, g, i, c: (b * nq + i, g)),
        scratch_shapes=[
            pltpu.VMEM((seq, 2 * dh), BF16),
            pltpu.VMEM((grp * qb, 2 * dh), BF16),
            pltpu.VMEM((dh, nc), BF16),
            pltpu.VMEM((n_slc, qb), F32),
            pltpu.VMEM((grp * qb, 1), F32),
            pltpu.VMEM((grp * qb, 1), F32),
            pltpu.VMEM((grp * qb, dh), F32),
            pltpu.VMEM((grp * qb, dh), F32),
        ],
    )
    return pl.pallas_call(
        functools.partial(_nsa_attn_kernel, n_sel=n_sel),
        grid_spec=grid_spec,
        out_shape=jax.ShapeDtypeStruct((batch * seq, n_groups * grp * dh), BF16),
        compiler_params=_cparams(("parallel", "parallel", "arbitrary")),
        name="nsa_attention",
    )(c31, proj, proj, proj, proj, proj, kvc, kvc, aux, dtab, tt, smt)


def _gdn_in_weights(w_in, n_heads):
    hp = GDN_HEADS_PER_STEP
    wd = n_heads * HEAD_DIM
    ng = n_heads // hp
    main = jnp.concatenate([w_in[:, :4 * wd], w_in[:, 4 * wd + 2 * n_heads:]], axis=1)
    wb = w_in[:, 4 * wd:4 * wd + n_heads].reshape(-1, ng, hp)
    wa = w_in[:, 4 * wd + n_heads:4 * wd + 2 * n_heads].reshape(-1, ng, hp)
    gate = jnp.concatenate([wb, wa, jnp.zeros((w_in.shape[0], ng, 128 - 2 * hp), w_in.dtype)], axis=2)
    return main.astype(BF16), gate.reshape(w_in.shape[0], ng * 128).astype(BF16)


def _nsa_in_weights(w_in, n_heads):
    grp = NSA_GROUP
    ng = n_heads // grp
    qw = n_heads * HEAD_DIM
    kvw = ng * HEAD_DIM
    main = jnp.concatenate([w_in[:, :qw], w_in[:, qw + 2 * kvw:qw + 6 * kvw],
                            w_in[:, qw + 6 * kvw + 3 * n_heads:]], axis=1)
    cmp_w = w_in[:, qw:qw + 2 * kvw]
    wg = w_in[:, qw + 6 * kvw:qw + 6 * kvw + 3 * n_heads].reshape(-1, ng, 3 * grp)
    gate = jnp.concatenate([wg, jnp.zeros((w_in.shape[0], ng, 128 - 3 * grp), w_in.dtype)], axis=2)
    aux = jnp.concatenate([cmp_w, gate.reshape(w_in.shape[0], ng * 128)], axis=1)
    return main.astype(BF16), aux.astype(BF16)


def _pick(n, candidates):
    for c in candidates:
        if n % c == 0:
            return c
    return n


def kernel(x, mem, rel_bias, norm_mix_w, norm_ffn_w, final_norm_w, mem_norm_w, mem_w_kv, w_out, gdn_w_in, gdn_conv_w, gdn_a_log, gdn_dt_bias, gdn_norm_w, nsa_w_in, nsa_cmp_pos_k, nsa_cmp_w1_k, nsa_cmp_w2_k, nsa_cmp_pos_v, nsa_cmp_w1_v, nsa_cmp_w2_v, ffn_w_up, ffn_conv_w, ffn_conv_b, ffn_w_down):
    batch, seq, d_model = x.shape
    depth = norm_mix_w.shape[0]
    n_heads = d_model // HEAD_DIM
    m_tok = mem.shape[1]
    mw = MEM_HEADS * HEAD_DIM
    d_ff = ffn_w_down.shape[1]
    t = batch * seq
    xf = x.reshape(t, d_model)
    memf = mem.reshape(batch * m_tok, d_model)
    tm = _pick(seq, (1024, 512, 256, 128))
    tm_small = _pick(seq, (512, 256, 128))

    for i in range(depth):
        j = i // 2
        kvw = mem_w_kv[i].astype(BF16)
        kv, _ = norm_matmul(memf, mem_norm_w[i], kvw, kvw[:, :128],
                            tm=_pick(batch * m_tok, (512, 256)), tn=kvw.shape[1])
        if i % 2 == 0:
            w_main, w_aux = _gdn_in_weights(gdn_w_in[j], n_heads)
            proj, aux = norm_matmul(xf, norm_mix_w[i], w_main, w_aux, tm=tm,
                                    tn=_pick(w_main.shape[1], (1152, 768, 512, 256)))
            mix = gdn_mixer(proj, aux, gdn_conv_w[j], gdn_a_log[j], gdn_dt_bias[j], gdn_norm_w[j],
                            batch=batch, seq=seq, lblk=_pick(seq, (512, 256, 128, 64)))
            qm_block = 4 * n_heads * HEAD_DIM // mw
        else:
            ng = n_heads // NSA_GROUP
            w_main, w_aux = _nsa_in_weights(nsa_w_in[j], n_heads)
            proj, aux = norm_matmul(xf, norm_mix_w[i], w_main, w_aux, tm=tm,
                                    tn=_pick(w_main.shape[1], (1280, 640, 512, 256)))
            pos = jnp.stack([nsa_cmp_pos_k[j], nsa_cmp_pos_v[j]]).astype(F32)
            w1 = jnp.stack([nsa_cmp_w1_k[j], nsa_cmp_w1_v[j]]).astype(BF16)
            w2 = jnp.stack([nsa_cmp_w2_k[j], nsa_cmp_w2_v[j]]).astype(BF16)
            kvc = nsa_compress(aux, pos, w1, w2, batch=batch, seq=seq, n_groups=ng)
            mix = nsa_attention(proj, aux, kvc, rel_bias, batch=batch, seq=seq, n_groups=ng)
            qm_block = (n_heads * HEAD_DIM + 4 * ng * HEAD_DIM) // mw
        mo = mem_attention(proj, kv, batch=batch, seq=seq, q_col_block=qm_block, ts=tm)
        wo = w_out[i].astype(BF16)
        xf = out_proj(xf, mix, mo, wo[:n_heads * HEAD_DIM], wo[n_heads * HEAD_DIM:], tm=tm_small)
        wu = ffn_w_up[i].astype(BF16)
        act = ffn_up(xf, norm_ffn_w[i], wu[:, :d_ff], wu[:, d_ff:], ffn_conv_w[i], ffn_conv_b[i],
                     seq=seq, tm=tm, tn=_pick(d_ff, (256, 128)))
        xf = ffn_down(xf, act, ffn_w_down[i].astype(BF16), final_norm_w, tm=tm_small,
                      final_norm=(i == depth - 1))
    return xf.reshape(batch, seq, d_model)
```

```python
import functools
import math

import jax
import jax.numpy as jnp
import numpy as np
from jax import lax
from jax.experimental import pallas as pl
from jax.experimental.pallas import tpu as pltpu

F32 = jnp.float32
BF16 = jnp.bfloat16

HEAD_DIM = 128
GDN_CONV = 4
GDN_CHUNK = 64
GDN_HEADS_PER_STEP = 4
NSA_GROUP = 4
CMP_LEN = 32
CMP_STRIDE = 16
SLC_BLOCK = 64
SLC_TOPK = 16
WINDOW = 512
NSA_QB = 256
NSA_ROWBLK = 128
MEM_HEADS = 4
NUM_BUCKETS = 32
MAX_DISTANCE = 128
FFN_CONV = 3
EPS = 1e-6
LOG2E = math.log2(math.e)
NEG = -1e30
SEL_PENALTY = 32768.0
VMEM_LIMIT = 56 * 1024 * 1024


def _cparams(sem):
    return pltpu.CompilerParams(dimension_semantics=sem, vmem_limit_bytes=VMEM_LIMIT)


def _dot(a, b):
    return jnp.dot(a, b, preferred_element_type=F32)


def _dot_nt(a, b):
    return lax.dot_general(a, b, (((1,), (1,)), ((), ())), preferred_element_type=F32)


def _silu(x):
    return x * (1.0 / (1.0 + jnp.exp(-x)))


def _sigmoid(x):
    return 1.0 / (1.0 + jnp.exp(-x))


def _softplus(x):
    return jnp.maximum(x, 0.0) + jnp.log(1.0 + jnp.exp(-jnp.abs(x)))


def _split3(x):
    h = x.astype(BF16)
    r = x - h.astype(F32)
    m = r.astype(BF16)
    l = (r - m.astype(F32)).astype(BF16)
    return h, m, l


def _norm_mm_kernel(x_ref, nw_ref, w_ref, waux_ref, o_ref, oaux_ref, xn_ref):
    @pl.when(pl.program_id(1) == 0)
    def _():
        x = x_ref[...]
        ms = jnp.mean(x * x, axis=-1, keepdims=True)
        xn = (x * lax.rsqrt(ms + EPS) * nw_ref[...]).astype(BF16)
        xn_ref[...] = xn
        oaux_ref[...] = _dot(xn, waux_ref[...])

    o_ref[...] = _dot(xn_ref[...], w_ref[...]).astype(o_ref.dtype)


def norm_matmul(x, nw, w, waux, *, tm, tn):
    t, d = x.shape
    n = w.shape[1]
    na = waux.shape[1]
    return pl.pallas_call(
        _norm_mm_kernel,
        grid=(t // tm, n // tn),
        in_specs=[
            pl.BlockSpec((tm, d), lambda i, j: (i, 0)),
            pl.BlockSpec((1, d), lambda i, j: (0, 0)),
            pl.BlockSpec((d, tn), lambda i, j: (0, j)),
            pl.BlockSpec((d, na), lambda i, j: (0, 0)),
        ],
        out_specs=[
            pl.BlockSpec((tm, tn), lambda i, j: (i, j)),
            pl.BlockSpec((tm, na), lambda i, j: (i, 0)),
        ],
        out_shape=[jax.ShapeDtypeStruct((t, n), BF16), jax.ShapeDtypeStruct((t, na), F32)],
        scratch_shapes=[pltpu.VMEM((tm, d), BF16)],
        compiler_params=_cparams(("parallel", "arbitrary")),
        name="norm_matmul",
    )(x, nw.reshape(1, d), w, waux)


def _mem_attn_kernel(q_ref, kv_ref, o_ref):
    scale = HEAD_DIM ** -0.5
    mw = MEM_HEADS * HEAD_DIM
    for h in range(MEM_HEADS):
        lo = h * HEAD_DIM
        q = q_ref[:, lo:lo + HEAD_DIM]
        k = kv_ref[:, lo:lo + HEAD_DIM]
        v = kv_ref[:, mw + lo:mw + lo + HEAD_DIM]
        s = _dot_nt(q, k) * scale
        m = jnp.max(s, axis=-1, keepdims=True)
        p = jnp.exp(s - m)
        l = jnp.sum(p, axis=-1, keepdims=True)
        o = _dot(p.astype(BF16), v) / l
        o_ref[:, lo:lo + HEAD_DIM] = o.astype(o_ref.dtype)


def mem_attention(proj, kv, *, batch, seq, q_col_block, ts):
    mw = MEM_HEADS * HEAD_DIM
    m_tok = kv.shape[0] // batch
    nt = seq // ts
    return pl.pallas_call(
        _mem_attn_kernel,
        grid=(batch, nt),
        in_specs=[
            pl.BlockSpec((ts, mw), lambda b, i: (b * nt + i, q_col_block)),
            pl.BlockSpec((m_tok, 2 * mw), lambda b, i: (b, 0)),
        ],
        out_specs=pl.BlockSpec((ts, mw), lambda b, i: (b * nt + i, 0)),
        out_shape=jax.ShapeDtypeStruct((batch * seq, mw), BF16),
        compiler_params=_cparams(("parallel", "parallel")),
        name="mem_attention",
    )(proj, kv)


def _out_proj_kernel(x_ref, a_ref, b_ref, wa_ref, wb_ref, o_ref):
    o_ref[...] = x_ref[...] + _dot(a_ref[...], wa_ref[...]) + _dot(b_ref[...], wb_ref[...])


def out_proj(x, a, b, wa, wb, *, tm):
    t, d = x.shape
    ka, kb = a.shape[1], b.shape[1]
    return pl.pallas_call(
        _out_proj_kernel,
        grid=(t // tm,),
        in_specs=[
            pl.BlockSpec((tm, d), lambda i: (i, 0)),
            pl.BlockSpec((tm, ka), lambda i: (i, 0)),
            pl.BlockSpec((tm, kb), lambda i: (i, 0)),
            pl.BlockSpec((ka, d), lambda i: (0, 0)),
            pl.BlockSpec((kb, d), lambda i: (0, 0)),
        ],
        out_specs=pl.BlockSpec((tm, d), lambda i: (i, 0)),
        out_shape=jax.ShapeDtypeStruct((t, d), F32),
        compiler_params=_cparams(("parallel",)),
        name="out_proj",
    )(x, a, b, wa, wb)


def _ffn_up_kernel(x_ref, nw_ref, wg_ref, wv_ref, cw_ref, cb_ref, o_ref, xn_ref, gs_ref, carry_ref,
                   *, tiles_per_seq):
    i = pl.program_id(0)
    j = pl.program_id(1)
    tm = x_ref.shape[0]

    @pl.when(j == 0)
    def _():
        x = x_ref[...]
        ms = jnp.mean(x * x, axis=-1, keepdims=True)
        xn_ref[...] = (x * lax.rsqrt(ms + EPS) * nw_ref[...]).astype(BF16)

    xn = xn_ref[...]
    g = _dot(xn, wg_ref[...])
    v = _dot(xn, wv_ref[...])

    @pl.when(i % tiles_per_seq == 0)
    def _():
        gs_ref[0:8, :] = jnp.zeros((8, gs_ref.shape[1]), F32)

    @pl.when(i % tiles_per_seq != 0)
    def _():
        gs_ref[0:8, :] = carry_ref[j]

    gs_ref[8:, :] = g
    carry_ref[j] = g[tm - 8:, :]
    cw = cw_ref[...]
    conv = (gs_ref[6:6 + tm, :] * cw[0:1, :] + gs_ref[7:7 + tm, :] * cw[1:2, :]
            + g * cw[2:3, :] + cb_ref[...])
    o_ref[...] = (_silu(conv) * v).astype(o_ref.dtype)


def ffn_up(x, nw, wg, wv, cw, cb, *, seq, tm, tn):
    t, d = x.shape
    dff = wg.shape[1]
    nj = dff // tn
    kern = functools.partial(_ffn_up_kernel, tiles_per_seq=seq // tm)
    return pl.pallas_call(
        kern,
        grid=(t // tm, nj),
        in_specs=[
            pl.BlockSpec((tm, d), lambda i, j: (i, 0)),
            pl.BlockSpec((1, d), lambda i, j: (0, 0)),
            pl.BlockSpec((d, tn), lambda i, j: (0, j)),
            pl.BlockSpec((d, tn), lambda i, j: (0, j)),
            pl.BlockSpec((8, tn), lambda i, j: (0, j)),
            pl.BlockSpec((1, tn), lambda i, j: (0, j)),
        ],
        out_specs=pl.BlockSpec((tm, tn), lambda i, j: (i, j)),
        out_shape=jax.ShapeDtypeStruct((t, dff), BF16),
        scratch_shapes=[
            pltpu.VMEM((tm, d), BF16),
            pltpu.VMEM((tm + 8, tn), F32),
            pltpu.VMEM((nj, 8, tn), F32),
        ],
        compiler_params=_cparams(("arbitrary", "arbitrary")),
        name="ffn_up",
    )(x, nw.reshape(1, d), wg, wv, jnp.pad(cw, ((0, 8 - cw.shape[0]), (0, 0))), cb.reshape(1, dff))


def _ffn_down_kernel(x_ref, a_ref, w_ref, fw_ref, o_ref, *, final_norm):
    y = x_ref[...] + _dot(a_ref[...], w_ref[...])
    if final_norm:
        ms = jnp.mean(y * y, axis=-1, keepdims=True)
        y = y * lax.rsqrt(ms + EPS) * fw_ref[...]
    o_ref[...] = y


def ffn_down(x, a, w, fw, *, tm, final_norm):
    t, d = x.shape
    k = a.shape[1]
    return pl.pallas_call(
        functools.partial(_ffn_down_kernel, final_norm=final_norm),
        grid=(t // tm,),
        in_specs=[
            pl.BlockSpec((tm, d), lambda i: (i, 0)),
            pl.BlockSpec((tm, k), lambda i: (i, 0)),
            pl.BlockSpec((k, d), lambda i: (0, 0)),
            pl.BlockSpec((1, d), lambda i: (0, 0)),
        ],
        out_specs=pl.BlockSpec((tm, d), lambda i: (i, 0)),
        out_shape=jax.ShapeDtypeStruct((t, d), F32),
        compiler_params=_cparams(("parallel",)),
        name="ffn_down",
    )(x, a, w, fw.reshape(1, d))


def _gdn_kernel(q_ref, k_ref, v_ref, z_ref, gt_ref, cw_ref, arow_ref, dtrow_ref, nw_ref, o_ref,
                qs_ref, ks_ref, vs_ref, state_ref, *, n_chunks):
    hp = GDN_HEADS_PER_STEP
    dh = HEAD_DIM
    c = GDN_CHUNK
    lblk = q_ref.shape[0]
    sb = pl.program_id(2)

    @pl.when(sb == 0)
    def _():
        zero8 = jnp.zeros((8, hp * dh), F32)
        qs_ref[0:8, :] = zero8
        ks_ref[0:8, :] = zero8
        vs_ref[0:8, :] = zero8
        state_ref[...] = jnp.zeros(state_ref.shape, F32)

    for idx, (src, dst) in enumerate(((q_ref, qs_ref), (k_ref, ks_ref), (v_ref, vs_ref))):
        dst[8:, :] = src[...].astype(F32)
        w = cw_ref[idx]
        y = (dst[5:5 + lblk, :] * w[0:1, :] + dst[6:6 + lblk, :] * w[1:2, :]
             + dst[7:7 + lblk, :] * w[2:3, :] + dst[8:8 + lblk, :] * w[3:4, :])
        tail = dst[lblk:lblk + 8, :]
        dst[8:, :] = _silu(y)
        dst[0:8, :] = tail

    row = lax.broadcasted_iota(jnp.int32, (c, c), 0)
    col = lax.broadcasted_iota(jnp.int32, (c, c), 1)
    incl = row >= col
    strict = row > col
    tril = jnp.where(incl, 1.0, 0.0).astype(BF16)
    arow = arow_ref[0]
    dtrow = dtrow_ref[0]
    nw = nw_ref[...]

    def chunk_body(ci, carry):
        r0 = pl.multiple_of(ci * c, c)
        gt = gt_ref[pl.ds(r0, c), :]
        beta_all = _sigmoid(gt)
        g_all = -arow * _softplus(gt + dtrow)
        gh, gm, gl = _split3(g_all)
        gc_all = _dot(tril, gh) + _dot(tril, gm) + _dot(tril, gl)
        gc_t = gc_all.T
        for r in range(hp):
            lo = r * dh
            qh = qs_ref[pl.ds(r0 + 8, c), lo:lo + dh]
            kh = ks_ref[pl.ds(r0 + 8, c), lo:lo + dh]
            vh = vs_ref[pl.ds(r0 + 8, c), lo:lo + dh]
            qn = qh * lax.rsqrt(jnp.sum(qh * qh, axis=-1, keepdims=True) + EPS) * (dh ** -0.5)
            kn = kh * lax.rsqrt(jnp.sum(kh * kh, axis=-1, keepdims=True) + EPS)
            beta = beta_all[:, r:r + 1]
            gc_col = gc_all[:, hp + r:hp + r + 1]
            gc_row = gc_t[hp + r:hp + r + 1, :]
            gc_last = gc_row[:, c - 1:c]
            decay = jnp.where(incl, jnp.exp(jnp.where(incl, gc_col - gc_row, 0.0)), 0.0)
            eg = jnp.exp(gc_col)
            kb = kn * beta
            kn_b = kn.astype(BF16)
            lmat = jnp.where(strict, _dot_nt(kb.astype(BF16), kn_b) * decay, 0.0)
            a_intra = jnp.where(incl, _dot_nt(qn.astype(BF16), kn_b) * decay, 0.0)
            x = jnp.concatenate([vh * beta, kb * eg], axis=1)
            mp = -lmat
            for it in range(6):
                mp_b = mp.astype(BF16)
                x = x + _dot(mp_b, x.astype(BF16))
                if it < 5:
                    mp = _dot(mp_b, mp_b)
            u = x[:, :dh]
            w = x[:, dh:]
            state = state_ref[r]
            state_b = state.astype(BF16)
            v_new = u - _dot(w.astype(BF16), state_b)
            v_new_b = v_new.astype(BF16)
            o = _dot((qn * eg).astype(BF16), state_b) + _dot(a_intra.astype(BF16), v_new_b)
            k_dec = kn * jnp.exp(gc_last - gc_col)
            state_ref[r] = state * jnp.exp(gc_last) + _dot(k_dec.T.astype(BF16), v_new_b)
            ms = jnp.mean(o * o, axis=-1, keepdims=True)
            zz = z_ref[pl.ds(r0, c), lo:lo + dh].astype(F32)
            o = o * lax.rsqrt(ms + EPS) * nw * _silu(zz)
            o_ref[pl.ds(r0, c), lo:lo + dh] = o.astype(o_ref.dtype)
        return carry

    lax.fori_loop(0, n_chunks, chunk_body, 0)


def gdn_mixer(proj, gates, conv_w, a_log, dt_bias, norm_w, *, batch, seq, lblk):
    hp = GDN_HEADS_PER_STEP
    dh = HEAD_DIM
    n_heads = a_log.shape[0]
    ng = n_heads // hp
    wd = n_heads * dh
    bw = hp * dh
    nsb = seq // lblk
    arow = jnp.zeros((ng, 1, 128), F32).at[:, 0, hp:2 * hp].set(jnp.exp(a_log.astype(F32)).reshape(ng, hp))
    dtrow = jnp.zeros((ng, 1, 128), F32).at[:, 0, hp:2 * hp].set(dt_bias.astype(F32).reshape(ng, hp))
    cw = conv_w.reshape(GDN_CONV, 3, ng, bw).transpose(1, 2, 0, 3)
    cw = jnp.pad(cw, ((0, 0), (0, 0), (0, 8 - GDN_CONV), (0, 0)))
    kern = functools.partial(_gdn_kernel, n_chunks=lblk // GDN_CHUNK)
    row_map = lambda off: (lambda b, g, s: (b * nsb + s, off + g))
    return pl.pallas_call(
        kern,
        grid=(batch, ng, nsb),
        in_specs=[
            pl.BlockSpec((lblk, bw), row_map(0)),
            pl.BlockSpec((lblk, bw), row_map(ng)),
            pl.BlockSpec((lblk, bw), row_map(2 * ng)),
            pl.BlockSpec((lblk, bw), row_map(3 * ng)),
            pl.BlockSpec((lblk, 128), row_map(0)),
            pl.BlockSpec((3, None, 8, bw), lambda b, g, s: (0, g, 0, 0)),
            pl.BlockSpec((None, 1, 128), lambda b, g, s: (g, 0, 0)),
            pl.BlockSpec((None, 1, 128), lambda b, g, s: (g, 0, 0)),
            pl.BlockSpec((1, dh), lambda b, g, s: (0, 0)),
        ],
        out_specs=pl.BlockSpec((lblk, bw), row_map(0)),
        out_shape=jax.ShapeDtypeStruct((batch * seq, wd), BF16),
        scratch_shapes=[
            pltpu.VMEM((lblk + 8, bw), F32),
            pltpu.VMEM((lblk + 8, bw), F32),
            pltpu.VMEM((lblk + 8, bw), F32),
            pltpu.VMEM((hp, dh, dh), F32),
        ],
        compiler_params=_cparams(("parallel", "parallel", "arbitrary")),
        name="gdn_mixer",
    )(proj, proj, proj, proj, gates, cw, arow, dtrow, norm_w.reshape(1, dh).astype(F32))


def _nsa_compress_kernel(t_ref, pos_ref, w1_ref, w2_ref, o_ref):
    nc = t_ref.shape[0] // CMP_STRIDE
    dh = HEAD_DIM
    acc_a = jnp.zeros((nc, dh), F32)
    acc_b = jnp.zeros((nc, dh), F32)
    for i in range(CMP_STRIDE):
        xi = t_ref[pl.ds(i, nc, stride=CMP_STRIDE), :]
        xa = (xi + pos_ref[i:i + 1, :]).astype(BF16)
        xb = (xi + pos_ref[CMP_STRIDE + i:CMP_STRIDE + i + 1, :]).astype(BF16)
        acc_a = acc_a + _dot(xa, w1_ref[i * dh:(i + 1) * dh, :])
        acc_b = acc_b + _dot(xb, w1_ref[(CMP_STRIDE + i) * dh:(CMP_STRIDE + i + 1) * dh, :])
    h = acc_a + jnp.concatenate([acc_b[1:, :], acc_b[:1, :]], axis=0)
    o_ref[...] = _dot(_silu(h).astype(BF16), w2_ref[...]).astype(o_ref.dtype)


def nsa_compress(aux, pos, w1, w2, *, batch, seq, n_groups):
    dh = HEAD_DIM
    nc = seq // CMP_STRIDE
    return pl.pallas_call(
        _nsa_compress_kernel,
        grid=(batch, 2, n_groups),
        in_specs=[
            pl.BlockSpec((seq, dh), lambda b, j, g: (b, j * n_groups + g)),
            pl.BlockSpec((None, CMP_LEN, dh), lambda b, j, g: (j, 0, 0)),
            pl.BlockSpec((None, CMP_LEN * dh, dh), lambda b, j, g: (j, 0, 0)),
            pl.BlockSpec((None, dh, dh), lambda b, j, g: (j, 0, 0)),
        ],
        out_specs=pl.BlockSpec((None, None, None, nc, dh), lambda b, j, g: (j, b, g, 0, 0)),
        out_shape=jax.ShapeDtypeStruct((2, batch, n_groups, nc, dh), BF16),
        compiler_params=_cparams(("parallel", "parallel", "parallel")),
        name="nsa_compress",
    )(aux, pos, w1, w2)


def _t5_bucket_np(n):
    max_exact = NUM_BUCKETS // 2
    nf = np.maximum(n, 1).astype(np.float32)
    logv = np.log(nf / np.float32(max_exact)) / np.float32(math.log(MAX_DISTANCE / max_exact))
    large = max_exact + (logv * np.float32(NUM_BUCKETS - max_exact)).astype(np.int32)
    large = np.minimum(large, NUM_BUCKETS - 1)
    return np.where(n < max_exact, n, large).astype(np.int32)


def _t5_thresholds():
    n = np.arange(0, MAX_DISTANCE + 1, dtype=np.int32)
    b = _t5_bucket_np(n)
    half = NUM_BUCKETS // 2
    return tuple(int(np.min(n[b >= half + k])) for k in range(1, NUM_BUCKETS - half))


def _nsa_bias_kernel(tab_ref, dtab_ref, tt_ref, *, thresholds, nc):
    h = pl.program_id(0)
    qb = NSA_QB
    half = NUM_BUCKETS // 2
    c31 = tab_ref[NUM_BUCKETS - 1, h]

    def lookup(dist):
        n = jnp.maximum(dist, 0)
        big = jnp.full(n.shape, half, jnp.int32)
        for t in thresholds:
            big = big + jnp.where(n >= t, 1, 0)
        bucket = jnp.where(n < half, n, big)
        val = jnp.zeros(n.shape, F32)
        for b in range(NUM_BUCKETS):
            val = jnp.where(bucket == b, tab_ref[b, h], val)
        return val

    q = lax.broadcasted_iota(jnp.int32, (qb, qb), 0)
    kk = lax.broadcasted_iota(jnp.int32, (qb, qb), 1)
    dtab_ref[0] = jnp.where(q >= kk, (lookup(q - kk) - c31) * LOG2E, NEG)
    dtab_ref[1] = (lookup(q - kk + qb) - c31) * LOG2E
    dtab_ref[2] = jnp.where(kk > q, 0.0, NEG)
    nw = 2 * qb // CMP_STRIDE
    x = lax.broadcasted_iota(jnp.int32, (nw, qb), 0)
    ql = lax.broadcasted_iota(jnp.int32, (nw, qb), 1)
    dist = ql - CMP_STRIDE * x + (qb - CMP_LEN + 1)
    tt_ref[0:nc, :] = jnp.zeros((nc, qb), F32) + c31 * LOG2E
    tt_ref[nc:nc + nw, :] = jnp.where(dist >= 0, lookup(dist) * LOG2E, NEG)
    tt_ref[nc + nw:, :] = jnp.full((nc, qb), NEG, F32)


def nsa_bias_tables(rel_bias, seq):
    qb = NSA_QB
    nc = seq // CMP_STRIDE
    nw = 2 * qb // CMP_STRIDE
    n_heads = rel_bias.shape[1]
    kern = functools.partial(_nsa_bias_kernel, thresholds=_t5_thresholds(), nc=nc)
    return pl.pallas_call(
        kern,
        grid=(n_heads,),
        in_specs=[pl.BlockSpec(memory_space=pltpu.SMEM)],
        out_specs=[
            pl.BlockSpec((None, 3, qb, qb), lambda h: (h, 0, 0, 0)),
            pl.BlockSpec((None, 2 * nc + nw, qb), lambda h: (h, 0, 0)),
        ],
        out_shape=[jax.ShapeDtypeStruct((n_heads, 3, qb, qb), F32),
                   jax.ShapeDtypeStruct((n_heads, 2 * nc + nw, qb), F32)],
        compiler_params=_cparams(("parallel",)),
        name="nsa_bias_tables",
    )(rel_bias.astype(F32))


def _sel_map_t(seq):
    nc = seq // CMP_STRIDE
    n_slc = seq // SLC_BLOCK
    c_start = np.arange(nc, dtype=np.int32) * CMP_STRIDE
    c_end = c_start + CMP_LEN - 1
    s_start = np.arange(n_slc, dtype=np.int32) * SLC_BLOCK
    m = (c_start[None, :] < s_start[:, None] + SLC_BLOCK) & (s_start[:, None] <= c_end[None, :])
    return m.astype(np.float32)


def _nsa_attn_kernel(cst_ref, q_ref, ksl_ref, vsl_ref, kw_ref, vw_ref, kc_ref, vc_ref, gt_ref,
                     dtab_ref, tt_ref, smt_ref, o_ref,
                     kaug_ref, vaug_ref, qaug_ref, vct_ref, sc_ref, s_ref, p_ref, m_ref, alpha_ref,
                     acc_ref, osel_ref, *, n_sel):
    qb = NSA_QB
    dh = HEAD_DIM
    grp = NSA_GROUP
    nrows = grp * qb
    rblk = NSA_ROWBLK
    g = pl.program_id(1)
    qi = pl.program_id(2)
    seq = ksl_ref.shape[0]
    nc = kc_ref.shape[0]
    n_slc = seq // SLC_BLOCK
    blocks_per_chunk = qb // SLC_BLOCK

    @pl.when(qi == 0)
    def _():
        lane = lax.broadcasted_iota(jnp.int32, (seq, dh), 1)
        rb = lax.broadcasted_iota(jnp.int32, (seq, dh), 0) // SLC_BLOCK
        ones_tail = jnp.where(lane >= dh - 2, 1.0, 0.0)
        kaug_ref[0, :, 0:dh] = ksl_ref[...]
        kaug_ref[0, :, dh:] = (jnp.where(rb == lane, -SEL_PENALTY, 0.0) + ones_tail).astype(BF16)
        kaug_ref[1, :, 0:dh] = kw_ref[...]
        kaug_ref[1, :, dh:] = ones_tail.astype(BF16)
        ones = jnp.ones((seq, dh), BF16)
        vaug_ref[0, :, 0:dh] = vsl_ref[...]
        vaug_ref[0, :, dh:] = ones
        vaug_ref[1, :, 0:dh] = vw_ref[...]
        vaug_ref[1, :, dh:] = ones
        vct_ref[...] = vc_ref[...].astype(F32).T.astype(BF16)

    for r in range(grp):
        qaug_ref[r * qb:(r + 1) * qb, 0:dh] = q_ref[:, r * dh:(r + 1) * dh]
    q4 = qaug_ref[:, 0:dh]

    st = _dot_nt(kc_ref[...], q4)
    start = pl.multiple_of(nc + CMP_STRIDE - (qb // CMP_STRIDE) * qi, CMP_STRIDE)
    psum = jnp.zeros((nc, qb), F32)
    o_cmp = []
    for r in range(grp):
        bias = tt_ref[r, pl.ds(start, nc), :]
        s = st[:, r * qb:(r + 1) * qb] + bias
        valid = bias > 0.5 * NEG
        m = jnp.max(s, axis=0, keepdims=True)
        p = jnp.where(valid, jnp.exp2(s - m), 0.0)
        p = p / jnp.maximum(jnp.sum(p, axis=0, keepdims=True), 1e-30)
        psum = psum + p
        o_cmp.append(_dot(vct_ref[...], p.astype(BF16)).T)

    ph = psum.astype(BF16)
    pl_ = (psum - ph.astype(F32)).astype(BF16)
    smt = smt_ref[...]
    imp = _dot(smt, ph) + _dot(smt, pl_)
    jb = lax.broadcasted_iota(jnp.int32, (n_slc, qb), 0)
    tb = qi * blocks_per_chunk + lax.broadcasted_iota(jnp.int32, (n_slc, qb), 1) // SLC_BLOCK
    forced = (jb == 0) | (jb == tb) | (jb == tb - 1)
    score = jnp.where(jb <= tb, jnp.where(forced, jnp.inf, imp), -jnp.inf)
    sc_ref[...] = score

    def rank_body(j, rank):
        rowv = jnp.broadcast_to(sc_ref[pl.ds(j, 1), :], (n_slc, qb))
        ahead = (rowv > score) | ((rowv == score) & (jb > j))
        return rank + jnp.where(ahead, 1.0, 0.0)

    rank = lax.fori_loop(0, n_slc, rank_body, jnp.zeros((n_slc, qb), F32))
    notsel = jnp.where((rank < n_sel) & (score > -jnp.inf), 0.0, 1.0)
    notsel = jnp.concatenate([notsel, jnp.zeros((dh - n_slc, qb), F32)], axis=0).T
    lane1 = lax.broadcasted_iota(jnp.int32, (1, dh), 1)
    for r in range(grp):
        h = g * grp + r
        cvec = jnp.where(lane1 == dh - 2, cst_ref[0, h], jnp.where(lane1 == dh - 1, cst_ref[1, h], 0.0))
        qaug_ref[r * qb:(r + 1) * qb, dh:] = (notsel + cvec).astype(BF16)

    def reset():
        m_ref[...] = jnp.full(m_ref.shape, NEG, F32)
        acc_ref[...] = jnp.zeros(acc_ref.shape, F32)

    def softmax_rows(tile_idx):
        for rbi in range(nrows // rblk):
            rows = slice(rbi * rblk, (rbi + 1) * rblk)
            s0 = s_ref[rows, 0:dh]
            s1 = s_ref[rows, dh:]
            if tile_idx is not None:
                head = (rbi * rblk) // qb
                t0 = (rbi * rblk) % qb
                s0 = s0 + dtab_ref[head, tile_idx, t0:t0 + rblk, 0:dh]
                s1 = s1 + dtab_ref[head, tile_idx, t0:t0 + rblk, dh:]
            m_old = m_ref[rows, :]
            m_new = jnp.maximum(m_old, jnp.max(jnp.maximum(s0, s1), axis=-1, keepdims=True))
            alpha_ref[rows, :] = jnp.exp2(m_old - m_new)
            p_ref[rows, 0:dh] = jnp.exp2(s0 - m_new).astype(BF16)
            p_ref[rows, dh:] = jnp.exp2(s1 - m_new).astype(BF16)
            m_ref[rows, :] = m_new

    def flash_step(branch, kchunk, tile_idx):
        k0 = pl.multiple_of(kchunk * qb, qb)
        s_ref[...] = _dot_nt(qaug_ref[...], kaug_ref[branch, pl.ds(k0, qb), :])
        softmax_rows(tile_idx)
        pv = _dot(p_ref[...], vaug_ref[branch, pl.ds(k0, qb), :])
        alpha = alpha_ref[...]
        acc_ref[:, 0:dh] = acc_ref[:, 0:dh] * alpha + pv[:, 0:dh]
        acc_ref[:, dh:] = acc_ref[:, dh:] * alpha + pv[:, dh:]

    reset()

    def far_body(kchunk, carry):
        flash_step(0, kchunk, None)
        return carry

    lax.fori_loop(0, jnp.maximum(qi - 1, 0), far_body, 0)

    def near_body(w, carry):
        is_win = w >= 2
        branch = jnp.where(is_win, 1, 0)
        kchunk = jnp.where(is_win, qi - 4 + w, qi - 1 + w)
        tile_idx = jnp.where(is_win, 4 - w, 1 - w)

        @pl.when(w == 2)
        def _():
            osel_ref[...] = acc_ref[:, 0:dh] / acc_ref[:, dh:]
            reset()

        @pl.when(kchunk >= 0)
        def _():
            flash_step(branch, kchunk, tile_idx)

        return carry

    lax.fori_loop(0, 5, near_body, 0)

    gates = _sigmoid(gt_ref[...])
    for r in range(grp):
        rows = slice(r * qb, (r + 1) * qb)
        o_win = acc_ref[rows, 0:dh] / acc_ref[rows, dh:]
        o = (gates[:, 3 * r:3 * r + 1] * o_cmp[r] + gates[:, 3 * r + 1:3 * r + 2] * osel_ref[rows, :]
             + gates[:, 3 * r + 2:3 * r + 3] * o_win)
        o_ref[:, r * dh:(r + 1) * dh] = o.astype(o_ref.dtype)


def nsa_attention(proj, aux, kvc, rel_bias, *, batch, seq, n_groups):
    qb = NSA_QB
    dh = HEAD_DIM
    grp = NSA_GROUP
    nq = seq // qb
    nc = seq // CMP_STRIDE
    n_slc = seq // SLC_BLOCK
    n_sel = min(SLC_TOPK, n_slc)
    assert n_slc <= dh - 2
    nrows = grp * qb
    dtab, tt = nsa_bias_tables(rel_bias, seq)
    smt = jnp.asarray(_sel_map_t(seq), BF16)
    c31 = rel_bias[NUM_BUCKETS - 1].astype(F32) * LOG2E
    c31_hi = c31.astype(BF16).astype(F32)
    cst = jnp.stack([c31_hi, c31 - c31_hi])
    kv_base = n_groups * grp
    kv_map = lambda j: (lambda b, g, i, c: (b, kv_base + j * n_groups + g))
    grid_spec = pltpu.PrefetchScalarGridSpec(
        num_scalar_prefetch=1,
        grid=(batch, n_groups, nq),
        in_specs=[
            pl.BlockSpec((qb, grp * dh), lambda b, g, i, c: (b * nq + i, g)),
            pl.BlockSpec((seq, dh), kv_map(0)),
            pl.BlockSpec((seq, dh), kv_map(1)),
            pl.BlockSpec((seq, dh), kv_map(2)),
            pl.BlockSpec((seq, dh), kv_map(3)),
            pl.BlockSpec((None, None, None, nc, dh), lambda b, g, i, c: (0, b, g, 0, 0)),
            pl.BlockSpec((None, None, None, nc, dh), lambda b, g, i, c: (1, b, g, 0, 0)),
            pl.BlockSpec((qb, 128), lambda b, g, i, c: (b * nq + i, 2 * n_groups + g)),
            pl.BlockSpec((grp, 3, qb, qb), lambda b, g, i, c: (g, 0, 0, 0)),
            pl.BlockSpec((grp, tt.shape[1], qb), lambda b, g, i, c: (g, 0, 0)),
            pl.BlockSpec((n_slc, nc), lambda b, g, i, c: (0, 0)),
        ],
        out_specs=pl.BlockSpec((qb, grp * dh), lambda b, g, i, c: (b * nq + i, g)),
        scratch_shapes=[
            pltpu.VMEM((2, seq, 2 * dh), BF16),
            pltpu.VMEM((2, seq, 2 * dh), BF16),
            pltpu.VMEM((nrows, 2 * dh), BF16),
            pltpu.VMEM((dh, nc), BF16),
            pltpu.VMEM((n_slc, qb), F32),
            pltpu.VMEM((nrows, qb), F32),
            pltpu.VMEM((nrows, qb), BF16),
            pltpu.VMEM((nrows, dh), F32),
            pltpu.VMEM((nrows, dh), F32),
            pltpu.VMEM((nrows, 2 * dh), F32),
            pltpu.VMEM((nrows, dh), F32),
        ],
    )
    return pl.pallas_call(
        functools.partial(_nsa_attn_kernel, n_sel=n_sel),
        grid_spec=grid_spec,
        out_shape=jax.ShapeDtypeStruct((batch * seq, n_groups * grp * dh), BF16),
        compiler_params=_cparams(("parallel", "parallel", "arbitrary")),
        name="nsa_attention",
    )(cst, proj, proj, proj, proj, proj, kvc, kvc, aux, dtab, tt, smt)


def _gdn_in_weights(w_in, n_heads):
    hp = GDN_HEADS_PER_STEP
    wd = n_heads * HEAD_DIM
    ng = n_heads // hp
    main = jnp.concatenate([w_in[:, :4 * wd], w_in[:, 4 * wd + 2 * n_heads:]], axis=1)
    wb = w_in[:, 4 * wd:4 * wd + n_heads].reshape(-1, ng, hp)
    wa = w_in[:, 4 * wd + n_heads:4 * wd + 2 * n_heads].reshape(-1, ng, hp)
    gate = jnp.concatenate([wb, wa, jnp.zeros((w_in.shape[0], ng, 128 - 2 * hp), w_in.dtype)], axis=2)
    return main.astype(BF16), gate.reshape(w_in.shape[0], ng * 128).astype(BF16)


def _nsa_in_weights(w_in, n_heads):
    grp = NSA_GROUP
    ng = n_heads // grp
    qw = n_heads * HEAD_DIM
    kvw = ng * HEAD_DIM
    main = jnp.concatenate([w_in[:, :qw] * (HEAD_DIM ** -0.5 * LOG2E), w_in[:, qw + 2 * kvw:qw + 6 * kvw],
                            w_in[:, qw + 6 * kvw + 3 * n_heads:]], axis=1)
    cmp_w = w_in[:, qw:qw + 2 * kvw]
    wg = w_in[:, qw + 6 * kvw:qw + 6 * kvw + 3 * n_heads].reshape(-1, ng, 3 * grp)
    gate = jnp.concatenate([wg, jnp.zeros((w_in.shape[0], ng, 128 - 3 * grp), w_in.dtype)], axis=2)
    aux = jnp.concatenate([cmp_w, gate.reshape(w_in.shape[0], ng * 128)], axis=1)
    return main.astype(BF16), aux.astype(BF16)


def _pick(n, candidates):
    for c in candidates:
        if n % c == 0:
            return c
    return n


def kernel(x, mem, rel_bias, norm_mix_w, norm_ffn_w, final_norm_w, mem_norm_w, mem_w_kv, w_out, gdn_w_in, gdn_conv_w, gdn_a_log, gdn_dt_bias, gdn_norm_w, nsa_w_in, nsa_cmp_pos_k, nsa_cmp_w1_k, nsa_cmp_w2_k, nsa_cmp_pos_v, nsa_cmp_w1_v, nsa_cmp_w2_v, ffn_w_up, ffn_conv_w, ffn_conv_b, ffn_w_down):
    batch, seq, d_model = x.shape
    depth = norm_mix_w.shape[0]
    n_heads = d_model // HEAD_DIM
    m_tok = mem.shape[1]
    mw = MEM_HEADS * HEAD_DIM
    d_ff = ffn_w_down.shape[1]
    t = batch * seq
    xf = x.reshape(t, d_model)
    memf = mem.reshape(batch * m_tok, d_model)
    tm = _pick(seq, (1024, 512, 256, 128))
    tm_small = _pick(seq, (512, 256, 128))

    for i in range(depth):
        j = i // 2
        kvw = mem_w_kv[i].astype(BF16)
        kv, _ = norm_matmul(memf, mem_norm_w[i], kvw, kvw[:, :128],
                            tm=_pick(batch * m_tok, (512, 256)), tn=kvw.shape[1])
        if i % 2 == 0:
            w_main, w_aux = _gdn_in_weights(gdn_w_in[j], n_heads)
            proj, aux = norm_matmul(xf, norm_mix_w[i], w_main, w_aux, tm=tm,
                                    tn=_pick(w_main.shape[1], (1152, 768, 512, 256)))
            mix = gdn_mixer(proj, aux, gdn_conv_w[j], gdn_a_log[j], gdn_dt_bias[j], gdn_norm_w[j],
                            batch=batch, seq=seq, lblk=_pick(seq, (512, 256, 128, 64)))
            qm_block = 4 * n_heads * HEAD_DIM // mw
        else:
            ng = n_heads // NSA_GROUP
            w_main, w_aux = _nsa_in_weights(nsa_w_in[j], n_heads)
            proj, aux = norm_matmul(xf, norm_mix_w[i], w_main, w_aux, tm=tm,
                                    tn=_pick(w_main.shape[1], (1280, 640, 512, 256)))
            pos = jnp.stack([nsa_cmp_pos_k[j], nsa_cmp_pos_v[j]]).astype(F32)
            w1 = jnp.stack([nsa_cmp_w1_k[j], nsa_cmp_w1_v[j]]).astype(BF16)
            w2 = jnp.stack([nsa_cmp_w2_k[j], nsa_cmp_w2_v[j]]).astype(BF16)
            kvc = nsa_compress(aux, pos, w1, w2, batch=batch, seq=seq, n_groups=ng)
            mix = nsa_attention(proj, aux, kvc, rel_bias, batch=batch, seq=seq, n_groups=ng)
            qm_block = (n_heads * HEAD_DIM + 4 * ng * HEAD_DIM) // mw
        mo = mem_attention(proj, kv, batch=batch, seq=seq, q_col_block=qm_block, ts=tm)
        wo = w_out[i].astype(BF16)
        xf = out_proj(xf, mix, mo, wo[:n_heads * HEAD_DIM], wo[n_heads * HEAD_DIM:], tm=tm_small)
        wu = ffn_w_up[i].astype(BF16)
        act = ffn_up(xf, norm_ffn_w[i], wu[:, :d_ff], wu[:, d_ff:], ffn_conv_w[i], ffn_conv_b[i],
                     seq=seq, tm=tm, tn=_pick(d_ff, (256, 128)))
        xf = ffn_down(xf, act, ffn_w_down[i].astype(BF16), final_norm_w, tm=tm_small,
                      final_norm=(i == depth - 1))
    return xf.reshape(batch, seq, d_model)
```

```python
import functools
import math

import jax
import jax.numpy as jnp
import numpy as np
from jax import lax
from jax.experimental import pallas as pl
from jax.experimental.pallas import tpu as pltpu

F32 = jnp.float32
BF16 = jnp.bfloat16

HEAD_DIM = 128
GDN_CONV = 4
GDN_CHUNK = 64
GDN_HEADS_PER_STEP = 4
GDN_SUPER = 256
NSA_GROUP = 4
CMP_LEN = 32
CMP_STRIDE = 16
SLC_BLOCK = 64
SLC_TOPK = 16
WINDOW = 512
NSA_QB = 256
NSA_ROWBLK = 128
MEM_HEADS = 4
NUM_BUCKETS = 32
MAX_DISTANCE = 128
FFN_CONV = 3
EPS = 1e-6
LOG2E = math.log2(math.e)
NEG = -1e30
SEL_PENALTY = 32768.0
VMEM_LIMIT = 56 * 1024 * 1024


def _cparams(sem):
    return pltpu.CompilerParams(dimension_semantics=sem, vmem_limit_bytes=VMEM_LIMIT)


def _dot(a, b):
    return jnp.dot(a, b, preferred_element_type=F32)


def _dot_nt(a, b):
    return lax.dot_general(a, b, (((1,), (1,)), ((), ())), preferred_element_type=F32)


def _silu(x):
    return x * (1.0 / (1.0 + jnp.exp(-x)))


def _sigmoid(x):
    return 1.0 / (1.0 + jnp.exp(-x))


def _softplus(x):
    return jnp.maximum(x, 0.0) + jnp.log(1.0 + jnp.exp(-jnp.abs(x)))


def _split3(x):
    h = x.astype(BF16)
    r = x - h.astype(F32)
    m = r.astype(BF16)
    l = (r - m.astype(F32)).astype(BF16)
    return h, m, l


def _norm_mm_kernel(x_ref, nw_ref, w_ref, waux_ref, o_ref, oaux_ref, xn_ref):
    @pl.when(pl.program_id(1) == 0)
    def _():
        x = x_ref[...]
        ms = jnp.mean(x * x, axis=-1, keepdims=True)
        xn = (x * lax.rsqrt(ms + EPS) * nw_ref[...]).astype(BF16)
        xn_ref[...] = xn
        oaux_ref[...] = _dot(xn, waux_ref[...])

    o_ref[...] = _dot(xn_ref[...], w_ref[...]).astype(o_ref.dtype)


def norm_matmul(x, nw, w, waux, *, tm, tn):
    t, d = x.shape
    n = w.shape[1]
    na = waux.shape[1]
    return pl.pallas_call(
        _norm_mm_kernel,
        grid=(t // tm, n // tn),
        in_specs=[
            pl.BlockSpec((tm, d), lambda i, j: (i, 0)),
            pl.BlockSpec((1, d), lambda i, j: (0, 0)),
            pl.BlockSpec((d, tn), lambda i, j: (0, j)),
            pl.BlockSpec((d, na), lambda i, j: (0, 0)),
        ],
        out_specs=[
            pl.BlockSpec((tm, tn), lambda i, j: (i, j)),
            pl.BlockSpec((tm, na), lambda i, j: (i, 0)),
        ],
        out_shape=[jax.ShapeDtypeStruct((t, n), BF16), jax.ShapeDtypeStruct((t, na), F32)],
        scratch_shapes=[pltpu.VMEM((tm, d), BF16)],
        compiler_params=_cparams(("parallel", "arbitrary")),
        name="norm_matmul",
    )(x, nw.reshape(1, d), w, waux)


def _mem_attn_kernel(q_ref, kv_ref, o_ref):
    scale = HEAD_DIM ** -0.5
    mw = MEM_HEADS * HEAD_DIM
    for h in range(MEM_HEADS):
        lo = h * HEAD_DIM
        q = q_ref[:, lo:lo + HEAD_DIM]
        k = kv_ref[:, lo:lo + HEAD_DIM]
        v = kv_ref[:, mw + lo:mw + lo + HEAD_DIM]
        s = _dot_nt(q, k) * scale
        m = jnp.max(s, axis=-1, keepdims=True)
        p = jnp.exp(s - m)
        l = jnp.sum(p, axis=-1, keepdims=True)
        o = _dot(p.astype(BF16), v) / l
        o_ref[:, lo:lo + HEAD_DIM] = o.astype(o_ref.dtype)


def mem_attention(proj, kv, *, batch, seq, q_col_block, ts):
    mw = MEM_HEADS * HEAD_DIM
    m_tok = kv.shape[0] // batch
    nt = seq // ts
    return pl.pallas_call(
        _mem_attn_kernel,
        grid=(batch, nt),
        in_specs=[
            pl.BlockSpec((ts, mw), lambda b, i: (b * nt + i, q_col_block)),
            pl.BlockSpec((m_tok, 2 * mw), lambda b, i: (b, 0)),
        ],
        out_specs=pl.BlockSpec((ts, mw), lambda b, i: (b * nt + i, 0)),
        out_shape=jax.ShapeDtypeStruct((batch * seq, mw), BF16),
        compiler_params=_cparams(("parallel", "parallel")),
        name="mem_attention",
    )(proj, kv)


def _out_proj_kernel(x_ref, a_ref, b_ref, wa_ref, wb_ref, o_ref):
    o_ref[...] = x_ref[...] + _dot(a_ref[...], wa_ref[...]) + _dot(b_ref[...], wb_ref[...])


def out_proj(x, a, b, wa, wb, *, tm):
    t, d = x.shape
    ka, kb = a.shape[1], b.shape[1]
    return pl.pallas_call(
        _out_proj_kernel,
        grid=(t // tm,),
        in_specs=[
            pl.BlockSpec((tm, d), lambda i: (i, 0)),
            pl.BlockSpec((tm, ka), lambda i: (i, 0)),
            pl.BlockSpec((tm, kb), lambda i: (i, 0)),
            pl.BlockSpec((ka, d), lambda i: (0, 0)),
            pl.BlockSpec((kb, d), lambda i: (0, 0)),
        ],
        out_specs=pl.BlockSpec((tm, d), lambda i: (i, 0)),
        out_shape=jax.ShapeDtypeStruct((t, d), F32),
        compiler_params=_cparams(("parallel",)),
        name="out_proj",
    )(x, a, b, wa, wb)


def _ffn_up_kernel(x_ref, nw_ref, wg_ref, wv_ref, cw_ref, cb_ref, o_ref, xn_ref, gs_ref, carry_ref,
                   *, tiles_per_seq):
    i = pl.program_id(0)
    j = pl.program_id(1)
    tm = x_ref.shape[0]

    @pl.when(j == 0)
    def _():
        x = x_ref[...]
        ms = jnp.mean(x * x, axis=-1, keepdims=True)
        xn_ref[...] = (x * lax.rsqrt(ms + EPS) * nw_ref[...]).astype(BF16)

    xn = xn_ref[...]
    g = _dot(xn, wg_ref[...])
    v = _dot(xn, wv_ref[...])

    @pl.when(i % tiles_per_seq == 0)
    def _():
        gs_ref[0:8, :] = jnp.zeros((8, gs_ref.shape[1]), F32)

    @pl.when(i % tiles_per_seq != 0)
    def _():
        gs_ref[0:8, :] = carry_ref[j]

    gs_ref[8:, :] = g
    carry_ref[j] = g[tm - 8:, :]
    cw = cw_ref[...]
    conv = (gs_ref[6:6 + tm, :] * cw[0:1, :] + gs_ref[7:7 + tm, :] * cw[1:2, :]
            + g * cw[2:3, :] + cb_ref[...])
    o_ref[...] = (_silu(conv) * v).astype(o_ref.dtype)


def ffn_up(x, nw, wg, wv, cw, cb, *, seq, tm, tn):
    t, d = x.shape
    dff = wg.shape[1]
    nj = dff // tn
    kern = functools.partial(_ffn_up_kernel, tiles_per_seq=seq // tm)
    return pl.pallas_call(
        kern,
        grid=(t // tm, nj),
        in_specs=[
            pl.BlockSpec((tm, d), lambda i, j: (i, 0)),
            pl.BlockSpec((1, d), lambda i, j: (0, 0)),
            pl.BlockSpec((d, tn), lambda i, j: (0, j)),
            pl.BlockSpec((d, tn), lambda i, j: (0, j)),
            pl.BlockSpec((8, tn), lambda i, j: (0, j)),
            pl.BlockSpec((1, tn), lambda i, j: (0, j)),
        ],
        out_specs=pl.BlockSpec((tm, tn), lambda i, j: (i, j)),
        out_shape=jax.ShapeDtypeStruct((t, dff), BF16),
        scratch_shapes=[
            pltpu.VMEM((tm, d), BF16),
            pltpu.VMEM((tm + 8, tn), F32),
            pltpu.VMEM((nj, 8, tn), F32),
        ],
        compiler_params=_cparams(("arbitrary", "arbitrary")),
        name="ffn_up",
    )(x, nw.reshape(1, d), wg, wv, jnp.pad(cw, ((0, 8 - cw.shape[0]), (0, 0))), cb.reshape(1, dff))


def _ffn_down_kernel(x_ref, a_ref, w_ref, fw_ref, o_ref, *, final_norm):
    y = x_ref[...] + _dot(a_ref[...], w_ref[...])
    if final_norm:
        ms = jnp.mean(y * y, axis=-1, keepdims=True)
        y = y * lax.rsqrt(ms + EPS) * fw_ref[...]
    o_ref[...] = y


def ffn_down(x, a, w, fw, *, tm, final_norm):
    t, d = x.shape
    k = a.shape[1]
    return pl.pallas_call(
        functools.partial(_ffn_down_kernel, final_norm=final_norm),
        grid=(t // tm,),
        in_specs=[
            pl.BlockSpec((tm, d), lambda i: (i, 0)),
            pl.BlockSpec((tm, k), lambda i: (i, 0)),
            pl.BlockSpec((k, d), lambda i: (0, 0)),
            pl.BlockSpec((1, d), lambda i: (0, 0)),
        ],
        out_specs=pl.BlockSpec((tm, d), lambda i: (i, 0)),
        out_shape=jax.ShapeDtypeStruct((t, d), F32),
        compiler_params=_cparams(("parallel",)),
        name="ffn_down",
    )(x, a, w, fw.reshape(1, d))


def _gdn_kernel(q_ref, k_ref, v_ref, z_ref, gt_ref, cw_ref, arow_ref, dtrow_ref, nw_ref, o_ref,
                qs_ref, ks_ref, vs_ref, state_ref, *, n_chunks):
    hp = GDN_HEADS_PER_STEP
    dh = HEAD_DIM
    c = GDN_CHUNK
    lblk = q_ref.shape[0]
    sb = pl.program_id(2)

    @pl.when(sb == 0)
    def _():
        zero8 = jnp.zeros((8, hp * dh), F32)
        qs_ref[0:8, :] = zero8
        ks_ref[0:8, :] = zero8
        vs_ref[0:8, :] = zero8
        state_ref[...] = jnp.zeros(state_ref.shape, F32)

    for idx, (src, dst) in enumerate(((q_ref, qs_ref), (k_ref, ks_ref), (v_ref, vs_ref))):
        dst[8:, :] = src[...].astype(F32)
        w = cw_ref[idx]
        y = (dst[5:5 + lblk, :] * w[0:1, :] + dst[6:6 + lblk, :] * w[1:2, :]
             + dst[7:7 + lblk, :] * w[2:3, :] + dst[8:8 + lblk, :] * w[3:4, :])
        tail = dst[lblk:lblk + 8, :]
        dst[8:, :] = _silu(y)
        dst[0:8, :] = tail

    sl = GDN_SUPER
    cps = sl // c
    row = lax.broadcasted_iota(jnp.int32, (sl, sl), 0)
    col = lax.broadcasted_iota(jnp.int32, (sl, sl), 1)
    same = (row // c) == (col // c)
    incl = same & (row >= col)
    strict = same & (row > col)
    tril = jnp.where(incl, 1.0, 0.0).astype(BF16)
    ones_blk = jnp.where(same, 1.0, 0.0).astype(BF16)
    arow = arow_ref[0]
    dtrow = dtrow_ref[0]
    nw = nw_ref[...]
    states = [state_ref[r] for r in range(hp)]
    zeros_c = jnp.zeros((c, dh), BF16)

    for sci in range(lblk // sl):
        r0 = sci * sl
        gt = gt_ref[r0:r0 + sl, :]
        beta_all = _sigmoid(gt)
        g_all = -arow * _softplus(gt + dtrow)
        gh, gm, gl = _split3(g_all)
        gc_all = _dot(tril, gh) + _dot(tril, gm) + _dot(tril, gl)
        gend_all = _dot(ones_blk, gh) + _dot(ones_blk, gm) + _dot(ones_blk, gl)
        gc_t = gc_all.T
        pre = []
        for r in range(hp):
            lo = r * dh
            qh = qs_ref[r0 + 8:r0 + 8 + sl, lo:lo + dh]
            kh = ks_ref[r0 + 8:r0 + 8 + sl, lo:lo + dh]
            vh = vs_ref[r0 + 8:r0 + 8 + sl, lo:lo + dh]
            qn = qh * lax.rsqrt(jnp.sum(qh * qh, axis=-1, keepdims=True) + EPS) * (dh ** -0.5)
            kn = kh * lax.rsqrt(jnp.sum(kh * kh, axis=-1, keepdims=True) + EPS)
            beta = beta_all[:, r:r + 1]
            gc_col = gc_all[:, hp + r:hp + r + 1]
            gc_row = gc_t[hp + r:hp + r + 1, :]
            gend = gend_all[:, hp + r:hp + r + 1]
            decay = jnp.where(incl, jnp.exp(jnp.where(incl, gc_col - gc_row, 0.0)), 0.0)
            eg = jnp.exp(gc_col)
            kb = kn * beta
            kn_b = kn.astype(BF16)
            lmat = jnp.where(strict, _dot_nt(kb.astype(BF16), kn_b) * decay, 0.0)
            a_intra = jnp.where(incl, _dot_nt(qn.astype(BF16), kn_b) * decay, 0.0).astype(BF16)
            x = jnp.concatenate([vh * beta, kb * eg], axis=1)
            mp = -lmat
            for it in range(6):
                mp_b = mp.astype(BF16)
                x = x + _dot(mp_b, x.astype(BF16))
                if it < 5:
                    mp = _dot(mp_b, mp_b)
            pre.append((x[:, :dh], x[:, dh:], qn * eg, kn * jnp.exp(gend - gc_col), a_intra,
                        jnp.exp(gend)))

        for ci in range(cps):
            rows = slice(ci * c, (ci + 1) * c)
            for r in range(hp):
                lo = r * dh
                u, w, qd, kd, a_intra, egend = pre[r]
                state = states[r]
                wq = jnp.concatenate([w[rows], qd[rows]], axis=0).astype(BF16)
                wqs = _dot(wq, state.astype(BF16))
                v_new = u[rows] - wqs[:c]
                v_new_b = v_new.astype(BF16)
                vcat = jnp.concatenate([zeros_c] * ci + [v_new_b] + [zeros_c] * (cps - 1 - ci), axis=0)
                o = wqs[c:] + _dot(a_intra[rows], vcat)
                states[r] = (state * egend[ci * c:ci * c + 1, :]
                             + _dot(kd[rows].T.astype(BF16), v_new_b))
                ms = jnp.mean(o * o, axis=-1, keepdims=True)
                zz = z_ref[r0 + ci * c:r0 + (ci + 1) * c, lo:lo + dh].astype(F32)
                o = o * lax.rsqrt(ms + EPS) * nw * _silu(zz)
                o_ref[r0 + ci * c:r0 + (ci + 1) * c, lo:lo + dh] = o.astype(o_ref.dtype)

    for r in range(hp):
        state_ref[r] = states[r]


def gdn_mixer(proj, gates, conv_w, a_log, dt_bias, norm_w, *, batch, seq, lblk):
    hp = GDN_HEADS_PER_STEP
    dh = HEAD_DIM
    n_heads = a_log.shape[0]
    ng = n_heads // hp
    wd = n_heads * dh
    bw = hp * dh
    nsb = seq // lblk
    arow = jnp.zeros((ng, 1, 128), F32).at[:, 0, hp:2 * hp].set(jnp.exp(a_log.astype(F32)).reshape(ng, hp))
    dtrow = jnp.zeros((ng, 1, 128), F32).at[:, 0, hp:2 * hp].set(dt_bias.astype(F32).reshape(ng, hp))
    cw = conv_w.reshape(GDN_CONV, 3, ng, bw).transpose(1, 2, 0, 3)
    cw = jnp.pad(cw, ((0, 0), (0, 0), (0, 8 - GDN_CONV), (0, 0)))
    kern = functools.partial(_gdn_kernel, n_chunks=lblk // GDN_CHUNK)
    row_map = lambda off: (lambda b, g, s: (b * nsb + s, off + g))
    return pl.pallas_call(
        kern,
        grid=(batch, ng, nsb),
        in_specs=[
            pl.BlockSpec((lblk, bw), row_map(0)),
            pl.BlockSpec((lblk, bw), row_map(ng)),
            pl.BlockSpec((lblk, bw), row_map(2 * ng)),
            pl.BlockSpec((lblk, bw), row_map(3 * ng)),
            pl.BlockSpec((lblk, 128), row_map(0)),
            pl.BlockSpec((3, None, 8, bw), lambda b, g, s: (0, g, 0, 0)),
            pl.BlockSpec((None, 1, 128), lambda b, g, s: (g, 0, 0)),
            pl.BlockSpec((None, 1, 128), lambda b, g, s: (g, 0, 0)),
            pl.BlockSpec((1, dh), lambda b, g, s: (0, 0)),
        ],
        out_specs=pl.BlockSpec((lblk, bw), row_map(0)),
        out_shape=jax.ShapeDtypeStruct((batch * seq, wd), BF16),
        scratch_shapes=[
            pltpu.VMEM((lblk + 8, bw), F32),
            pltpu.VMEM((lblk + 8, bw), F32),
            pltpu.VMEM((lblk + 8, bw), F32),
            pltpu.VMEM((hp, dh, dh), F32),
        ],
        compiler_params=_cparams(("parallel", "parallel", "arbitrary")),
        name="gdn_mixer",
    )(proj, proj, proj, proj, gates, cw, arow, dtrow, norm_w.reshape(1, dh).astype(F32))


def _nsa_compress_kernel(t_ref, pos_ref, w1_ref, w2_ref, o_ref):
    nc = t_ref.shape[0] // CMP_STRIDE
    dh = HEAD_DIM
    acc_a = jnp.zeros((nc, dh), F32)
    acc_b = jnp.zeros((nc, dh), F32)
    for i in range(CMP_STRIDE):
        xi = t_ref[pl.ds(i, nc, stride=CMP_STRIDE), :]
        xa = (xi + pos_ref[i:i + 1, :]).astype(BF16)
        xb = (xi + pos_ref[CMP_STRIDE + i:CMP_STRIDE + i + 1, :]).astype(BF16)
        acc_a = acc_a + _dot(xa, w1_ref[i * dh:(i + 1) * dh, :])
        acc_b = acc_b + _dot(xb, w1_ref[(CMP_STRIDE + i) * dh:(CMP_STRIDE + i + 1) * dh, :])
    h = acc_a + jnp.concatenate([acc_b[1:, :], acc_b[:1, :]], axis=0)
    o_ref[...] = _dot(_silu(h).astype(BF16), w2_ref[...]).astype(o_ref.dtype)


def nsa_compress(aux, pos, w1, w2, *, batch, seq, n_groups):
    dh = HEAD_DIM
    nc = seq // CMP_STRIDE
    return pl.pallas_call(
        _nsa_compress_kernel,
        grid=(batch, 2, n_groups),
        in_specs=[
            pl.BlockSpec((seq, dh), lambda b, j, g: (b, j * n_groups + g)),
            pl.BlockSpec((None, CMP_LEN, dh), lambda b, j, g: (j, 0, 0)),
            pl.BlockSpec((None, CMP_LEN * dh, dh), lambda b, j, g: (j, 0, 0)),
            pl.BlockSpec((None, dh, dh), lambda b, j, g: (j, 0, 0)),
        ],
        out_specs=pl.BlockSpec((None, None, None, nc, dh), lambda b, j, g: (j, b, g, 0, 0)),
        out_shape=jax.ShapeDtypeStruct((2, batch, n_groups, nc, dh), BF16),
        compiler_params=_cparams(("parallel", "parallel", "parallel")),
        name="nsa_compress",
    )(aux, pos, w1, w2)


def _t5_bucket_np(n):
    max_exact = NUM_BUCKETS // 2
    nf = np.maximum(n, 1).astype(np.float32)
    logv = np.log(nf / np.float32(max_exact)) / np.float32(math.log(MAX_DISTANCE / max_exact))
    large = max_exact + (logv * np.float32(NUM_BUCKETS - max_exact)).astype(np.int32)
    large = np.minimum(large, NUM_BUCKETS - 1)
    return np.where(n < max_exact, n, large).astype(np.int32)


def _t5_thresholds():
    n = np.arange(0, MAX_DISTANCE + 1, dtype=np.int32)
    b = _t5_bucket_np(n)
    half = NUM_BUCKETS // 2
    return tuple(int(np.min(n[b >= half + k])) for k in range(1, NUM_BUCKETS - half))


def _nsa_bias_kernel(tab_ref, dtab_ref, tt_ref, *, thresholds, nc):
    h = pl.program_id(0)
    qb = NSA_QB
    half = NUM_BUCKETS // 2
    c31 = tab_ref[NUM_BUCKETS - 1, h]

    def lookup(dist):
        n = jnp.maximum(dist, 0)
        big = jnp.full(n.shape, half, jnp.int32)
        for t in thresholds:
            big = big + jnp.where(n >= t, 1, 0)
        bucket = jnp.where(n < half, n, big)
        val = jnp.zeros(n.shape, F32)
        for b in range(NUM_BUCKETS):
            val = jnp.where(bucket == b, tab_ref[b, h], val)
        return val

    q = lax.broadcasted_iota(jnp.int32, (qb, qb), 0)
    kk = lax.broadcasted_iota(jnp.int32, (qb, qb), 1)
    dtab_ref[0] = jnp.where(q >= kk, (lookup(q - kk) - c31) * LOG2E, NEG)
    dtab_ref[1] = (lookup(q - kk + qb) - c31) * LOG2E
    dtab_ref[2] = jnp.where(kk > q, 0.0, NEG)
    nw = 2 * qb // CMP_STRIDE
    x = lax.broadcasted_iota(jnp.int32, (nw, qb), 0)
    ql = lax.broadcasted_iota(jnp.int32, (nw, qb), 1)
    dist = ql - CMP_STRIDE * x + (qb - CMP_LEN + 1)
    tt_ref[0:nc, :] = jnp.zeros((nc, qb), F32) + c31 * LOG2E
    tt_ref[nc:nc + nw, :] = jnp.where(dist >= 0, lookup(dist) * LOG2E, NEG)
    tt_ref[nc + nw:, :] = jnp.full((nc, qb), NEG, F32)


def nsa_bias_tables(rel_bias, seq):
    qb = NSA_QB
    nc = seq // CMP_STRIDE
    nw = 2 * qb // CMP_STRIDE
    n_heads = rel_bias.shape[1]
    kern = functools.partial(_nsa_bias_kernel, thresholds=_t5_thresholds(), nc=nc)
    return pl.pallas_call(
        kern,
        grid=(n_heads,),
        in_specs=[pl.BlockSpec(memory_space=pltpu.SMEM)],
        out_specs=[
            pl.BlockSpec((None, 3, qb, qb), lambda h: (h, 0, 0, 0)),
            pl.BlockSpec((None, 2 * nc + nw, qb), lambda h: (h, 0, 0)),
        ],
        out_shape=[jax.ShapeDtypeStruct((n_heads, 3, qb, qb), F32),
                   jax.ShapeDtypeStruct((n_heads, 2 * nc + nw, qb), F32)],
        compiler_params=_cparams(("parallel",)),
        name="nsa_bias_tables",
    )(rel_bias.astype(F32))


def _sel_map_t(seq):
    nc = seq // CMP_STRIDE
    n_slc = seq // SLC_BLOCK
    c_start = np.arange(nc, dtype=np.int32) * CMP_STRIDE
    c_end = c_start + CMP_LEN - 1
    s_start = np.arange(n_slc, dtype=np.int32) * SLC_BLOCK
    m = (c_start[None, :] < s_start[:, None] + SLC_BLOCK) & (s_start[:, None] <= c_end[None, :])
    return m.astype(np.float32)


def _nsa_attn_kernel(cst_ref, q_ref, ksl_ref, vsl_ref, kw_ref, vw_ref, kc_ref, vc_ref, gt_ref,
                     dtab_ref, tt_ref, smt_ref, o_ref,
                     kaug_ref, vaug_ref, qaug_ref, vct_ref, sc_ref, s_ref, p_ref, m_ref, alpha_ref,
                     acc_ref, osel_ref, *, n_sel):
    qb = NSA_QB
    dh = HEAD_DIM
    grp = NSA_GROUP
    nrows = grp * qb
    rblk = NSA_ROWBLK
    g = pl.program_id(1)
    qi = pl.program_id(2)
    seq = ksl_ref.shape[0]
    nc = kc_ref.shape[0]
    n_slc = seq // SLC_BLOCK
    blocks_per_chunk = qb // SLC_BLOCK

    @pl.when(qi == 0)
    def _():
        lane = lax.broadcasted_iota(jnp.int32, (seq, dh), 1)
        rb = lax.broadcasted_iota(jnp.int32, (seq, dh), 0) // SLC_BLOCK
        ones_tail = jnp.where(lane >= dh - 2, 1.0, 0.0)
        kaug_ref[0, :, 0:dh] = ksl_ref[...]
        kaug_ref[0, :, dh:] = (jnp.where(rb == lane, -SEL_PENALTY, 0.0) + ones_tail).astype(BF16)
        kaug_ref[1, :, 0:dh] = kw_ref[...]
        kaug_ref[1, :, dh:] = ones_tail.astype(BF16)
        ones = jnp.ones((seq, dh), BF16)
        vaug_ref[0, :, 0:dh] = vsl_ref[...]
        vaug_ref[0, :, dh:] = ones
        vaug_ref[1, :, 0:dh] = vw_ref[...]
        vaug_ref[1, :, dh:] = ones
        vct_ref[...] = vc_ref[...].astype(F32).T.astype(BF16)

    for r in range(grp):
        qaug_ref[r * qb:(r + 1) * qb, 0:dh] = q_ref[:, r * dh:(r + 1) * dh]
    q4 = qaug_ref[:, 0:dh]

    st = _dot_nt(kc_ref[...], q4)
    start = pl.multiple_of(nc + CMP_STRIDE - (qb // CMP_STRIDE) * qi, CMP_STRIDE)
    psum = jnp.zeros((nc, qb), F32)
    o_cmp = []
    for r in range(grp):
        bias = tt_ref[r, pl.ds(start, nc), :]
        s = st[:, r * qb:(r + 1) * qb] + bias
        valid = bias > 0.5 * NEG
        m = jnp.max(s, axis=0, keepdims=True)
        p = jnp.where(valid, jnp.exp2(s - m), 0.0)
        p = p / jnp.maximum(jnp.sum(p, axis=0, keepdims=True), 1e-30)
        psum = psum + p
        o_cmp.append(_dot(vct_ref[...], p.astype(BF16)).T)

    ph = psum.astype(BF16)
    pl_ = (psum - ph.astype(F32)).astype(BF16)
    smt = smt_ref[...]
    imp = _dot(smt, ph) + _dot(smt, pl_)
    jb = lax.broadcasted_iota(jnp.int32, (n_slc, qb), 0)
    tb = qi * blocks_per_chunk + lax.broadcasted_iota(jnp.int32, (n_slc, qb), 1) // SLC_BLOCK
    forced = (jb == 0) | (jb == tb) | (jb == tb - 1)
    score = jnp.where(jb <= tb, jnp.where(forced, jnp.inf, imp), -jnp.inf)
    sc_ref[...] = score

    def rank_body(j, rank):
        rowv = jnp.broadcast_to(sc_ref[pl.ds(j, 1), :], (n_slc, qb))
        ahead = (rowv > score) | ((rowv == score) & (jb > j))
        return rank + jnp.where(ahead, 1.0, 0.0)

    rank = lax.fori_loop(0, n_slc, rank_body, jnp.zeros((n_slc, qb), F32))
    notsel = jnp.where((rank < n_sel) & (score > -jnp.inf), 0.0, 1.0)
    notsel = jnp.concatenate([notsel, jnp.zeros((dh - n_slc, qb), F32)], axis=0).T
    lane1 = lax.broadcasted_iota(jnp.int32, (1, dh), 1)
    for r in range(grp):
        h = g * grp + r
        cvec = jnp.where(lane1 == dh - 2, cst_ref[0, h], jnp.where(lane1 == dh - 1, cst_ref[1, h], 0.0))
        qaug_ref[r * qb:(r + 1) * qb, dh:] = (notsel + cvec).astype(BF16)

    def reset():
        m_ref[...] = jnp.full(m_ref.shape, NEG, F32)
        acc_ref[...] = jnp.zeros(acc_ref.shape, F32)

    def softmax_rows(tile_idx):
        for rbi in range(nrows // rblk):
            rows = slice(rbi * rblk, (rbi + 1) * rblk)
            s0 = s_ref[rows, 0:dh]
            s1 = s_ref[rows, dh:]
            if tile_idx is not None:
                head = (rbi * rblk) // qb
                t0 = (rbi * rblk) % qb
                s0 = s0 + dtab_ref[head, tile_idx, t0:t0 + rblk, 0:dh]
                s1 = s1 + dtab_ref[head, tile_idx, t0:t0 + rblk, dh:]
            m_old = m_ref[rows, :]
            m_new = jnp.maximum(m_old, jnp.max(jnp.maximum(s0, s1), axis=-1, keepdims=True))
            alpha_ref[rows, :] = jnp.exp2(m_old - m_new)
            p_ref[rows, 0:dh] = jnp.exp2(s0 - m_new).astype(BF16)
            p_ref[rows, dh:] = jnp.exp2(s1 - m_new).astype(BF16)
            m_ref[rows, :] = m_new

    def flash_step(branch, kchunk, tile_idx):
        k0 = pl.multiple_of(kchunk * qb, qb)
        s_ref[...] = _dot_nt(qaug_ref[...], kaug_ref[branch, pl.ds(k0, qb), :])
        softmax_rows(tile_idx)
        pv = _dot(p_ref[...], vaug_ref[branch, pl.ds(k0, qb), :])
        alpha = alpha_ref[...]
        acc_ref[:, 0:dh] = acc_ref[:, 0:dh] * alpha + pv[:, 0:dh]
        acc_ref[:, dh:] = acc_ref[:, dh:] * alpha + pv[:, dh:]

    reset()

    def far_body(kchunk, carry):
        flash_step(0, kchunk, None)
        return carry

    lax.fori_loop(0, jnp.maximum(qi - 1, 0), far_body, 0)

    def near_body(w, carry):
        is_win = w >= 2
        branch = jnp.where(is_win, 1, 0)
        kchunk = jnp.where(is_win, qi - 4 + w, qi - 1 + w)
        tile_idx = jnp.where(is_win, 4 - w, 1 - w)

        @pl.when(w == 2)
        def _():
            osel_ref[...] = acc_ref[:, 0:dh] / acc_ref[:, dh:]
            reset()

        @pl.when(kchunk >= 0)
        def _():
            flash_step(branch, kchunk, tile_idx)

        return carry

    lax.fori_loop(0, 5, near_body, 0)

    gates = _sigmoid(gt_ref[...])
    for r in range(grp):
        rows = slice(r * qb, (r + 1) * qb)
        o_win = acc_ref[rows, 0:dh] / acc_ref[rows, dh:]
        o = (gates[:, 3 * r:3 * r + 1] * o_cmp[r] + gates[:, 3 * r + 1:3 * r + 2] * osel_ref[rows, :]
             + gates[:, 3 * r + 2:3 * r + 3] * o_win)
        o_ref[:, r * dh:(r + 1) * dh] = o.astype(o_ref.dtype)


def nsa_attention(proj, aux, kvc, rel_bias, *, batch, seq, n_groups):
    qb = NSA_QB
    dh = HEAD_DIM
    grp = NSA_GROUP
    nq = seq // qb
    nc = seq // CMP_STRIDE
    n_slc = seq // SLC_BLOCK
    n_sel = min(SLC_TOPK, n_slc)
    assert n_slc <= dh - 2
    nrows = grp * qb
    dtab, tt = nsa_bias_tables(rel_bias, seq)
    smt = jnp.asarray(_sel_map_t(seq), BF16)
    c31 = rel_bias[NUM_BUCKETS - 1].astype(F32) * LOG2E
    c31_hi = c31.astype(BF16).astype(F32)
    cst = jnp.stack([c31_hi, c31 - c31_hi])
    kv_base = n_groups * grp
    kv_map = lambda j: (lambda b, g, i, c: (b, kv_base + j * n_groups + g))
    grid_spec = pltpu.PrefetchScalarGridSpec(
        num_scalar_prefetch=1,
        grid=(batch, n_groups, nq),
        in_specs=[
            pl.BlockSpec((qb, grp * dh), lambda b, g, i, c: (b * nq + i, g)),
            pl.BlockSpec((seq, dh), kv_map(0)),
            pl.BlockSpec((seq, dh), kv_map(1)),
            pl.BlockSpec((seq, dh), kv_map(2)),
            pl.BlockSpec((seq, dh), kv_map(3)),
            pl.BlockSpec((None, None, None, nc, dh), lambda b, g, i, c: (0, b, g, 0, 0)),
            pl.BlockSpec((None, None, None, nc, dh), lambda b, g, i, c: (1, b, g, 0, 0)),
            pl.BlockSpec((qb, 128), lambda b, g, i, c: (b * nq + i, 2 * n_groups + g)),
            pl.BlockSpec((grp, 3, qb, qb), lambda b, g, i, c: (g, 0, 0, 0)),
            pl.BlockSpec((grp, tt.shape[1], qb), lambda b, g, i, c: (g, 0, 0)),
            pl.BlockSpec((n_slc, nc), lambda b, g, i, c: (0, 0)),
        ],
        out_specs=pl.BlockSpec((qb, grp * dh), lambda b, g, i, c: (b * nq + i, g)),
        scratch_shapes=[
            pltpu.VMEM((2, seq, 2 * dh), BF16),
            pltpu.VMEM((2, seq, 2 * dh), BF16),
            pltpu.VMEM((nrows, 2 * dh), BF16),
            pltpu.VMEM((dh, nc), BF16),
            pltpu.VMEM((n_slc, qb), F32),
            pltpu.VMEM((nrows, qb), F32),
            pltpu.VMEM((nrows, qb), BF16),
            pltpu.VMEM((nrows, dh), F32),
            pltpu.VMEM((nrows, dh), F32),
            pltpu.VMEM((nrows, 2 * dh), F32),
            pltpu.VMEM((nrows, dh), F32),
        ],
    )
    return pl.pallas_call(
        functools.partial(_nsa_attn_kernel, n_sel=n_sel),
        grid_spec=grid_spec,
        out_shape=jax.ShapeDtypeStruct((batch * seq, n_groups * grp * dh), BF16),
        compiler_params=_cparams(("parallel", "parallel", "arbitrary")),
        name="nsa_attention",
    )(cst, proj, proj, proj, proj, proj, kvc, kvc, aux, dtab, tt, smt)


def _gdn_in_weights(w_in, n_heads):
    hp = GDN_HEADS_PER_STEP
    wd = n_heads * HEAD_DIM
    ng = n_heads // hp
    main = jnp.concatenate([w_in[:, :4 * wd], w_in[:, 4 * wd + 2 * n_heads:]], axis=1)
    wb = w_in[:, 4 * wd:4 * wd + n_heads].reshape(-1, ng, hp)
    wa = w_in[:, 4 * wd + n_heads:4 * wd + 2 * n_heads].reshape(-1, ng, hp)
    gate = jnp.concatenate([wb, wa, jnp.zeros((w_in.shape[0], ng, 128 - 2 * hp), w_in.dtype)], axis=2)
    return main.astype(BF16), gate.reshape(w_in.shape[0], ng * 128).astype(BF16)


def _nsa_in_weights(w_in, n_heads):
    grp = NSA_GROUP
    ng = n_heads // grp
    qw = n_heads * HEAD_DIM
    kvw = ng * HEAD_DIM
    main = jnp.concatenate([w_in[:, :qw] * (HEAD_DIM ** -0.5 * LOG2E), w_in[:, qw + 2 * kvw:qw + 6 * kvw],
                            w_in[:, qw + 6 * kvw + 3 * n_heads:]], axis=1)
    cmp_w = w_in[:, qw:qw + 2 * kvw]
    wg = w_in[:, qw + 6 * kvw:qw + 6 * kvw + 3 * n_heads].reshape(-1, ng, 3 * grp)
    gate = jnp.concatenate([wg, jnp.zeros((w_in.shape[0], ng, 128 - 3 * grp), w_in.dtype)], axis=2)
    aux = jnp.concatenate([cmp_w, gate.reshape(w_in.shape[0], ng * 128)], axis=1)
    return main.astype(BF16), aux.astype(BF16)


def _pick(n, candidates):
    for c in candidates:
        if n % c == 0:
            return c
    return n


def kernel(x, mem, rel_bias, norm_mix_w, norm_ffn_w, final_norm_w, mem_norm_w, mem_w_kv, w_out, gdn_w_in, gdn_conv_w, gdn_a_log, gdn_dt_bias, gdn_norm_w, nsa_w_in, nsa_cmp_pos_k, nsa_cmp_w1_k, nsa_cmp_w2_k, nsa_cmp_pos_v, nsa_cmp_w1_v, nsa_cmp_w2_v, ffn_w_up, ffn_conv_w, ffn_conv_b, ffn_w_down):
    batch, seq, d_model = x.shape
    depth = norm_mix_w.shape[0]
    n_heads = d_model // HEAD_DIM
    m_tok = mem.shape[1]
    mw = MEM_HEADS * HEAD_DIM
    d_ff = ffn_w_down.shape[1]
    t = batch * seq
    xf = x.reshape(t, d_model)
    memf = mem.reshape(batch * m_tok, d_model)
    tm = _pick(seq, (1024, 512, 256, 128))
    tm_small = _pick(seq, (512, 256, 128))

    for i in range(depth):
        j = i // 2
        kvw = mem_w_kv[i].astype(BF16)
        kv, _ = norm_matmul(memf, mem_norm_w[i], kvw, kvw[:, :128],
                            tm=_pick(batch * m_tok, (512, 256)), tn=kvw.shape[1])
        if i % 2 == 0:
            w_main, w_aux = _gdn_in_weights(gdn_w_in[j], n_heads)
            proj, aux = norm_matmul(xf, norm_mix_w[i], w_main, w_aux, tm=tm,
                                    tn=_pick(w_main.shape[1], (1152, 768, 512, 256)))
            mix = gdn_mixer(proj, aux, gdn_conv_w[j], gdn_a_log[j], gdn_dt_bias[j], gdn_norm_w[j],
                            batch=batch, seq=seq, lblk=_pick(seq, (512, 256, 128, 64)))
            qm_block = 4 * n_heads * HEAD_DIM // mw
        else:
            ng = n_heads // NSA_GROUP
            w_main, w_aux = _nsa_in_weights(nsa_w_in[j], n_heads)
            proj, aux = norm_matmul(xf, norm_mix_w[i], w_main, w_aux, tm=tm,
                                    tn=_pick(w_main.shape[1], (1280, 640, 512, 256)))
            pos = jnp.stack([nsa_cmp_pos_k[j], nsa_cmp_pos_v[j]]).astype(F32)
            w1 = jnp.stack([nsa_cmp_w1_k[j], nsa_cmp_w1_v[j]]).astype(BF16)
            w2 = jnp.stack([nsa_cmp_w2_k[j], nsa_cmp_w2_v[j]]).astype(BF16)
            kvc = nsa_compress(aux, pos, w1, w2, batch=batch, seq=seq, n_groups=ng)
            mix = nsa_attention(proj, aux, kvc, rel_bias, batch=batch, seq=seq, n_groups=ng)
            qm_block = (n_heads * HEAD_DIM + 4 * ng * HEAD_DIM) // mw
        mo = mem_attention(proj, kv, batch=batch, seq=seq, q_col_block=qm_block, ts=tm)
        wo = w_out[i].astype(BF16)
        xf = out_proj(xf, mix, mo, wo[:n_heads * HEAD_DIM], wo[n_heads * HEAD_DIM:], tm=tm_small)
        wu = ffn_w_up[i].astype(BF16)
        act = ffn_up(xf, norm_ffn_w[i], wu[:, :d_ff], wu[:, d_ff:], ffn_conv_w[i], ffn_conv_b[i],
                     seq=seq, tm=tm, tn=_pick(d_ff, (256, 128)))
        xf = ffn_down(xf, act, ffn_w_down[i].astype(BF16), final_norm_w, tm=tm_small,
                      final_norm=(i == depth - 1))
    return xf.reshape(batch, seq, d_model)
```

```python
import functools
import math

import jax
import jax.numpy as jnp
import numpy as np
from jax import lax
from jax.experimental import pallas as pl
from jax.experimental.pallas import tpu as pltpu

F32 = jnp.float32
BF16 = jnp.bfloat16

HEAD_DIM = 128
GDN_CONV = 4
GDN_CHUNK = 64
GDN_HEADS_PER_STEP = 4
GDN_SUPER = 256
NSA_GROUP = 4
CMP_LEN = 32
CMP_STRIDE = 16
SLC_BLOCK = 64
SLC_TOPK = 16
WINDOW = 512
NSA_QB = 256
NSA_ROWBLK = 128
MEM_HEADS = 4
NUM_BUCKETS = 32
MAX_DISTANCE = 128
FFN_CONV = 3
EPS = 1e-6
LOG2E = math.log2(math.e)
NEG = -1e30
SEL_PENALTY = 32768.0
VMEM_LIMIT = 56 * 1024 * 1024


def _cparams(sem):
    return pltpu.CompilerParams(dimension_semantics=sem, vmem_limit_bytes=VMEM_LIMIT)


def _dot(a, b):
    return jnp.dot(a, b, preferred_element_type=F32)


def _dot_nt(a, b):
    return lax.dot_general(a, b, (((1,), (1,)), ((), ())), preferred_element_type=F32)


def _silu(x):
    return x * (1.0 / (1.0 + jnp.exp(-x)))


def _sigmoid(x):
    return 1.0 / (1.0 + jnp.exp(-x))


def _softplus(x):
    return jnp.maximum(x, 0.0) + jnp.log(1.0 + jnp.exp(-jnp.abs(x)))


def _split3(x):
    h = x.astype(BF16)
    r = x - h.astype(F32)
    m = r.astype(BF16)
    l = (r - m.astype(F32)).astype(BF16)
    return h, m, l


def _norm_mm_kernel(x_ref, nw_ref, w_ref, waux_ref, o_ref, oaux_ref, *, rs, tn):
    for r0 in range(0, x_ref.shape[0], rs):
        rows = slice(r0, r0 + rs)
        x = x_ref[rows, :]
        ms = jnp.mean(x * x, axis=-1, keepdims=True)
        xn = (x * lax.rsqrt(ms + EPS) * nw_ref[...]).astype(BF16)
        oaux_ref[rows, :] = _dot(xn, waux_ref[...])
        for c0 in range(0, o_ref.shape[1], tn):
            o_ref[rows, c0:c0 + tn] = _dot(xn, w_ref[:, c0:c0 + tn]).astype(o_ref.dtype)


def norm_matmul(x, nw, w, waux, *, tm, tn):
    t, d = x.shape
    n = w.shape[1]
    na = waux.shape[1]
    whole = lambda i: (0, 0)
    return pl.pallas_call(
        functools.partial(_norm_mm_kernel, rs=min(tm, 256), tn=tn),
        grid=(t // tm,),
        in_specs=[
            pl.BlockSpec((tm, d), lambda i: (i, 0)),
            pl.BlockSpec((1, d), whole),
            pl.BlockSpec((d, n), whole),
            pl.BlockSpec((d, na), whole),
        ],
        out_specs=[
            pl.BlockSpec((tm, n), lambda i: (i, 0)),
            pl.BlockSpec((tm, na), lambda i: (i, 0)),
        ],
        out_shape=[jax.ShapeDtypeStruct((t, n), BF16), jax.ShapeDtypeStruct((t, na), F32)],
        compiler_params=_cparams(("parallel",)),
        name="norm_matmul",
    )(x, nw.reshape(1, d), w, waux)


def _mem_attn_kernel(q_ref, kv_ref, o_ref):
    scale = HEAD_DIM ** -0.5
    mw = MEM_HEADS * HEAD_DIM
    for h in range(MEM_HEADS):
        lo = h * HEAD_DIM
        q = q_ref[:, lo:lo + HEAD_DIM]
        k = kv_ref[:, lo:lo + HEAD_DIM]
        v = kv_ref[:, mw + lo:mw + lo + HEAD_DIM]
        s = _dot_nt(q, k) * scale
        m = jnp.max(s, axis=-1, keepdims=True)
        p = jnp.exp(s - m)
        l = jnp.sum(p, axis=-1, keepdims=True)
        o = _dot(p.astype(BF16), v) / l
        o_ref[:, lo:lo + HEAD_DIM] = o.astype(o_ref.dtype)


def mem_attention(proj, kv, *, batch, seq, q_col_block, ts):
    mw = MEM_HEADS * HEAD_DIM
    m_tok = kv.shape[0] // batch
    nt = seq // ts
    return pl.pallas_call(
        _mem_attn_kernel,
        grid=(batch, nt),
        in_specs=[
            pl.BlockSpec((ts, mw), lambda b, i: (b * nt + i, q_col_block)),
            pl.BlockSpec((m_tok, 2 * mw), lambda b, i: (b, 0)),
        ],
        out_specs=pl.BlockSpec((ts, mw), lambda b, i: (b * nt + i, 0)),
        out_shape=jax.ShapeDtypeStruct((batch * seq, mw), BF16),
        compiler_params=_cparams(("parallel", "parallel")),
        name="mem_attention",
    )(proj, kv)


def _out_proj_kernel(x_ref, a_ref, b_ref, wa_ref, wb_ref, o_ref):
    o_ref[...] = x_ref[...] + _dot(a_ref[...], wa_ref[...]) + _dot(b_ref[...], wb_ref[...])


def out_proj(x, a, b, wa, wb, *, tm):
    t, d = x.shape
    ka, kb = a.shape[1], b.shape[1]
    return pl.pallas_call(
        _out_proj_kernel,
        grid=(t // tm,),
        in_specs=[
            pl.BlockSpec((tm, d), lambda i: (i, 0)),
            pl.BlockSpec((tm, ka), lambda i: (i, 0)),
            pl.BlockSpec((tm, kb), lambda i: (i, 0)),
            pl.BlockSpec((ka, d), lambda i: (0, 0)),
            pl.BlockSpec((kb, d), lambda i: (0, 0)),
        ],
        out_specs=pl.BlockSpec((tm, d), lambda i: (i, 0)),
        out_shape=jax.ShapeDtypeStruct((t, d), F32),
        compiler_params=_cparams(("parallel",)),
        name="out_proj",
    )(x, a, b, wa, wb)


def _ffn_up_kernel(x_ref, nw_ref, wg_ref, wv_ref, cw_ref, cb_ref, o_ref, gs_ref, carry_ref,
                   *, tiles_per_seq, tn):
    i = pl.program_id(0)
    tm = x_ref.shape[0]
    dff = o_ref.shape[1]
    x = x_ref[...]
    ms = jnp.mean(x * x, axis=-1, keepdims=True)
    xn = (x * lax.rsqrt(ms + EPS) * nw_ref[...]).astype(BF16)
    first = i % tiles_per_seq == 0
    for j in range(dff // tn):
        cols = slice(j * tn, (j + 1) * tn)
        g = _dot(xn, wg_ref[:, cols])
        v = _dot(xn, wv_ref[:, cols])
        gs_ref[0:8, cols] = jnp.where(first, 0.0, carry_ref[:, cols])
        gs_ref[8:, cols] = g
        carry_ref[:, cols] = g[tm - 8:, :]
        conv = (gs_ref[6:6 + tm, cols] * cw_ref[0:1, cols] + gs_ref[7:7 + tm, cols] * cw_ref[1:2, cols]
                + g * cw_ref[2:3, cols] + cb_ref[:, cols])
        o_ref[:, cols] = (_silu(conv) * v).astype(o_ref.dtype)


def ffn_up(x, nw, wg, wv, cw, cb, *, seq, tm, tn):
    t, d = x.shape
    dff = wg.shape[1]
    kern = functools.partial(_ffn_up_kernel, tiles_per_seq=seq // tm, tn=tn)
    whole = lambda i: (0, 0)
    return pl.pallas_call(
        kern,
        grid=(t // tm,),
        in_specs=[
            pl.BlockSpec((tm, d), lambda i: (i, 0)),
            pl.BlockSpec((1, d), whole),
            pl.BlockSpec((d, dff), whole),
            pl.BlockSpec((d, dff), whole),
            pl.BlockSpec((8, dff), whole),
            pl.BlockSpec((1, dff), whole),
        ],
        out_specs=pl.BlockSpec((tm, dff), lambda i: (i, 0)),
        out_shape=jax.ShapeDtypeStruct((t, dff), BF16),
        scratch_shapes=[
            pltpu.VMEM((tm + 8, dff), F32),
            pltpu.VMEM((8, dff), F32),
        ],
        compiler_params=_cparams(("arbitrary",)),
        name="ffn_up",
    )(x, nw.reshape(1, d), wg, wv, jnp.pad(cw, ((0, 8 - cw.shape[0]), (0, 0))), cb.reshape(1, dff))


def _ffn_down_kernel(x_ref, a_ref, w_ref, fw_ref, o_ref, *, final_norm):
    y = x_ref[...] + _dot(a_ref[...], w_ref[...])
    if final_norm:
        ms = jnp.mean(y * y, axis=-1, keepdims=True)
        y = y * lax.rsqrt(ms + EPS) * fw_ref[...]
    o_ref[...] = y


def ffn_down(x, a, w, fw, *, tm, final_norm):
    t, d = x.shape
    k = a.shape[1]
    return pl.pallas_call(
        functools.partial(_ffn_down_kernel, final_norm=final_norm),
        grid=(t // tm,),
        in_specs=[
            pl.BlockSpec((tm, d), lambda i: (i, 0)),
            pl.BlockSpec((tm, k), lambda i: (i, 0)),
            pl.BlockSpec((k, d), lambda i: (0, 0)),
            pl.BlockSpec((1, d), lambda i: (0, 0)),
        ],
        out_specs=pl.BlockSpec((tm, d), lambda i: (i, 0)),
        out_shape=jax.ShapeDtypeStruct((t, d), F32),
        compiler_params=_cparams(("parallel",)),
        name="ffn_down",
    )(x, a, w, fw.reshape(1, d))


def _gdn_kernel(q_ref, k_ref, v_ref, z_ref, gt_ref, cw_ref, arow_ref, dtrow_ref, nw_ref, o_ref,
                qs_ref, ks_ref, vs_ref, state_ref, *, n_chunks):
    hp = GDN_HEADS_PER_STEP
    dh = HEAD_DIM
    c = GDN_CHUNK
    lblk = q_ref.shape[0]
    sb = pl.program_id(2)

    @pl.when(sb == 0)
    def _():
        zero8 = jnp.zeros((8, hp * dh), F32)
        qs_ref[0:8, :] = zero8
        ks_ref[0:8, :] = zero8
        vs_ref[0:8, :] = zero8
        state_ref[...] = jnp.zeros(state_ref.shape, F32)

    for idx, (src, dst) in enumerate(((q_ref, qs_ref), (k_ref, ks_ref), (v_ref, vs_ref))):
        dst[8:, :] = src[...].astype(F32)
        w = cw_ref[idx]
        y = (dst[5:5 + lblk, :] * w[0:1, :] + dst[6:6 + lblk, :] * w[1:2, :]
             + dst[7:7 + lblk, :] * w[2:3, :] + dst[8:8 + lblk, :] * w[3:4, :])
        tail = dst[lblk:lblk + 8, :]
        dst[8:, :] = _silu(y)
        dst[0:8, :] = tail

    sl = GDN_SUPER
    cps = sl // c
    row = lax.broadcasted_iota(jnp.int32, (sl, sl), 0)
    col = lax.broadcasted_iota(jnp.int32, (sl, sl), 1)
    same = (row // c) == (col // c)
    incl = same & (row >= col)
    strict = same & (row > col)
    tril = jnp.where(incl, 1.0, 0.0).astype(BF16)
    ones_blk = jnp.where(same, 1.0, 0.0).astype(BF16)
    arow = arow_ref[0]
    dtrow = dtrow_ref[0]
    nw = nw_ref[...]
    states = [state_ref[r] for r in range(hp)]
    zeros_c = jnp.zeros((c, dh), BF16)

    for sci in range(lblk // sl):
        r0 = sci * sl
        gt = gt_ref[r0:r0 + sl, :]
        beta_all = _sigmoid(gt)
        g_all = -arow * _softplus(gt + dtrow)
        gh, gm, gl = _split3(g_all)
        gc_all = _dot(tril, gh) + _dot(tril, gm) + _dot(tril, gl)
        gend_all = _dot(ones_blk, gh) + _dot(ones_blk, gm) + _dot(ones_blk, gl)
        gc_t = gc_all.T
        pre = []
        for r in range(hp):
            lo = r * dh
            qh = qs_ref[r0 + 8:r0 + 8 + sl, lo:lo + dh]
            kh = ks_ref[r0 + 8:r0 + 8 + sl, lo:lo + dh]
            vh = vs_ref[r0 + 8:r0 + 8 + sl, lo:lo + dh]
            qn = qh * lax.rsqrt(jnp.sum(qh * qh, axis=-1, keepdims=True) + EPS) * (dh ** -0.5)
            kn = kh * lax.rsqrt(jnp.sum(kh * kh, axis=-1, keepdims=True) + EPS)
            beta = beta_all[:, r:r + 1]
            gc_col = gc_all[:, hp + r:hp + r + 1]
            gc_row = gc_t[hp + r:hp + r + 1, :]
            gend = gend_all[:, hp + r:hp + r + 1]
            decay = jnp.where(incl, jnp.exp(jnp.where(incl, gc_col - gc_row, 0.0)), 0.0)
            eg = jnp.exp(gc_col)
            kb = kn * beta
            kn_b = kn.astype(BF16)
            lmat = jnp.where(strict, _dot_nt(kb.astype(BF16), kn_b) * decay, 0.0)
            a_intra = jnp.where(incl, _dot_nt(qn.astype(BF16), kn_b) * decay, 0.0).astype(BF16)
            x = jnp.concatenate([vh * beta, kb * eg], axis=1)
            mp = -lmat
            for it in range(6):
                mp_b = mp.astype(BF16)
                x = x + _dot(mp_b, x.astype(BF16))
                if it < 5:
                    mp = _dot(mp_b, mp_b)
            pre.append((x[:, :dh], x[:, dh:], qn * eg, kn * jnp.exp(gend - gc_col), a_intra,
                        jnp.exp(gend)))

        for ci in range(cps):
            rows = slice(ci * c, (ci + 1) * c)
            for r in range(hp):
                lo = r * dh
                u, w, qd, kd, a_intra, egend = pre[r]
                state = states[r]
                wq = jnp.concatenate([w[rows], qd[rows]], axis=0).astype(BF16)
                wqs = _dot(wq, state.astype(BF16))
                v_new = u[rows] - wqs[:c]
                v_new_b = v_new.astype(BF16)
                vcat = jnp.concatenate([zeros_c] * ci + [v_new_b] + [zeros_c] * (cps - 1 - ci), axis=0)
                o = wqs[c:] + _dot(a_intra[rows], vcat)
                states[r] = (state * egend[ci * c:ci * c + 1, :]
                             + _dot(kd[rows].T.astype(BF16), v_new_b))
                ms = jnp.mean(o * o, axis=-1, keepdims=True)
                zz = z_ref[r0 + ci * c:r0 + (ci + 1) * c, lo:lo + dh].astype(F32)
                o = o * lax.rsqrt(ms + EPS) * nw * _silu(zz)
                o_ref[r0 + ci * c:r0 + (ci + 1) * c, lo:lo + dh] = o.astype(o_ref.dtype)

    for r in range(hp):
        state_ref[r] = states[r]


def gdn_mixer(proj, gates, conv_w, a_log, dt_bias, norm_w, *, batch, seq, lblk):
    hp = GDN_HEADS_PER_STEP
    dh = HEAD_DIM
    n_heads = a_log.shape[0]
    ng = n_heads // hp
    wd = n_heads * dh
    bw = hp * dh
    nsb = seq // lblk
    arow = jnp.zeros((ng, 1, 128), F32).at[:, 0, hp:2 * hp].set(jnp.exp(a_log.astype(F32)).reshape(ng, hp))
    dtrow = jnp.zeros((ng, 1, 128), F32).at[:, 0, hp:2 * hp].set(dt_bias.astype(F32).reshape(ng, hp))
    cw = conv_w.reshape(GDN_CONV, 3, ng, bw).transpose(1, 2, 0, 3)
    cw = jnp.pad(cw, ((0, 0), (0, 0), (0, 8 - GDN_CONV), (0, 0)))
    kern = functools.partial(_gdn_kernel, n_chunks=lblk // GDN_CHUNK)
    row_map = lambda off: (lambda b, g, s: (b * nsb + s, off + g))
    return pl.pallas_call(
        kern,
        grid=(batch, ng, nsb),
        in_specs=[
            pl.BlockSpec((lblk, bw), row_map(0)),
            pl.BlockSpec((lblk, bw), row_map(ng)),
            pl.BlockSpec((lblk, bw), row_map(2 * ng)),
            pl.BlockSpec((lblk, bw), row_map(3 * ng)),
            pl.BlockSpec((lblk, 128), row_map(0)),
            pl.BlockSpec((3, None, 8, bw), lambda b, g, s: (0, g, 0, 0)),
            pl.BlockSpec((None, 1, 128), lambda b, g, s: (g, 0, 0)),
            pl.BlockSpec((None, 1, 128), lambda b, g, s: (g, 0, 0)),
            pl.BlockSpec((1, dh), lambda b, g, s: (0, 0)),
        ],
        out_specs=pl.BlockSpec((lblk, bw), row_map(0)),
        out_shape=jax.ShapeDtypeStruct((batch * seq, wd), BF16),
        scratch_shapes=[
            pltpu.VMEM((lblk + 8, bw), F32),
            pltpu.VMEM((lblk + 8, bw), F32),
            pltpu.VMEM((lblk + 8, bw), F32),
            pltpu.VMEM((hp, dh, dh), F32),
        ],
        compiler_params=_cparams(("parallel", "parallel", "arbitrary")),
        name="gdn_mixer",
    )(proj, proj, proj, proj, gates, cw, arow, dtrow, norm_w.reshape(1, dh).astype(F32))


def _nsa_compress_kernel(t_ref, pos_ref, w1_ref, w2_ref, o_ref):
    nc = t_ref.shape[0] // CMP_STRIDE
    dh = HEAD_DIM
    acc_a = jnp.zeros((nc, dh), F32)
    acc_b = jnp.zeros((nc, dh), F32)
    for i in range(CMP_STRIDE):
        xi = t_ref[pl.ds(i, nc, stride=CMP_STRIDE), :]
        xa = (xi + pos_ref[i:i + 1, :]).astype(BF16)
        xb = (xi + pos_ref[CMP_STRIDE + i:CMP_STRIDE + i + 1, :]).astype(BF16)
        acc_a = acc_a + _dot(xa, w1_ref[i * dh:(i + 1) * dh, :])
        acc_b = acc_b + _dot(xb, w1_ref[(CMP_STRIDE + i) * dh:(CMP_STRIDE + i + 1) * dh, :])
    h = acc_a + jnp.concatenate([acc_b[1:, :], acc_b[:1, :]], axis=0)
    o_ref[...] = _dot(_silu(h).astype(BF16), w2_ref[...]).astype(o_ref.dtype)


def nsa_compress(aux, pos, w1, w2, *, batch, seq, n_groups):
    dh = HEAD_DIM
    nc = seq // CMP_STRIDE
    return pl.pallas_call(
        _nsa_compress_kernel,
        grid=(batch, 2, n_groups),
        in_specs=[
            pl.BlockSpec((seq, dh), lambda b, j, g: (b, j * n_groups + g)),
            pl.BlockSpec((None, CMP_LEN, dh), lambda b, j, g: (j, 0, 0)),
            pl.BlockSpec((None, CMP_LEN * dh, dh), lambda b, j, g: (j, 0, 0)),
            pl.BlockSpec((None, dh, dh), lambda b, j, g: (j, 0, 0)),
        ],
        out_specs=pl.BlockSpec((None, None, None, nc, dh), lambda b, j, g: (j, b, g, 0, 0)),
        out_shape=jax.ShapeDtypeStruct((2, batch, n_groups, nc, dh), BF16),
        compiler_params=_cparams(("parallel", "parallel", "parallel")),
        name="nsa_compress",
    )(aux, pos, w1, w2)


def _t5_bucket_np(n):
    max_exact = NUM_BUCKETS // 2
    nf = np.maximum(n, 1).astype(np.float32)
    logv = np.log(nf / np.float32(max_exact)) / np.float32(math.log(MAX_DISTANCE / max_exact))
    large = max_exact + (logv * np.float32(NUM_BUCKETS - max_exact)).astype(np.int32)
    large = np.minimum(large, NUM_BUCKETS - 1)
    return np.where(n < max_exact, n, large).astype(np.int32)


def _t5_thresholds():
    n = np.arange(0, MAX_DISTANCE + 1, dtype=np.int32)
    b = _t5_bucket_np(n)
    half = NUM_BUCKETS // 2
    return tuple(int(np.min(n[b >= half + k])) for k in range(1, NUM_BUCKETS - half))


def _nsa_bias_kernel(tab_ref, dtab_ref, tt_ref, *, thresholds, nc):
    h = pl.program_id(0)
    qb = NSA_QB
    half = NUM_BUCKETS // 2
    c31 = tab_ref[NUM_BUCKETS - 1, h]

    def lookup(dist):
        n = jnp.maximum(dist, 0)
        big = jnp.full(n.shape, half, jnp.int32)
        for t in thresholds:
            big = big + jnp.where(n >= t, 1, 0)
        bucket = jnp.where(n < half, n, big)
        val = jnp.zeros(n.shape, F32)
        for b in range(NUM_BUCKETS):
            val = jnp.where(bucket == b, tab_ref[b, h], val)
        return val

    q = lax.broadcasted_iota(jnp.int32, (qb, qb), 0)
    kk = lax.broadcasted_iota(jnp.int32, (qb, qb), 1)
    dtab_ref[0] = jnp.where(q >= kk, (lookup(q - kk) - c31) * LOG2E, NEG)
    dtab_ref[1] = (lookup(q - kk + qb) - c31) * LOG2E
    dtab_ref[2] = jnp.where(kk > q, 0.0, NEG)
    nw = 2 * qb // CMP_STRIDE
    x = lax.broadcasted_iota(jnp.int32, (nw, qb), 0)
    ql = lax.broadcasted_iota(jnp.int32, (nw, qb), 1)
    dist = ql - CMP_STRIDE * x + (qb - CMP_LEN + 1)
    tt_ref[0:nc, :] = jnp.zeros((nc, qb), F32) + c31 * LOG2E
    tt_ref[nc:nc + nw, :] = jnp.where(dist >= 0, lookup(dist) * LOG2E, NEG)
    tt_ref[nc + nw:, :] = jnp.full((nc, qb), NEG, F32)


def nsa_bias_tables(rel_bias, seq):
    qb = NSA_QB
    nc = seq // CMP_STRIDE
    nw = 2 * qb // CMP_STRIDE
    n_heads = rel_bias.shape[1]
    kern = functools.partial(_nsa_bias_kernel, thresholds=_t5_thresholds(), nc=nc)
    return pl.pallas_call(
        kern,
        grid=(n_heads,),
        in_specs=[pl.BlockSpec(memory_space=pltpu.SMEM)],
        out_specs=[
            pl.BlockSpec((None, 3, qb, qb), lambda h: (h, 0, 0, 0)),
            pl.BlockSpec((None, 2 * nc + nw, qb), lambda h: (h, 0, 0)),
        ],
        out_shape=[jax.ShapeDtypeStruct((n_heads, 3, qb, qb), F32),
                   jax.ShapeDtypeStruct((n_heads, 2 * nc + nw, qb), F32)],
        compiler_params=_cparams(("parallel",)),
        name="nsa_bias_tables",
    )(rel_bias.astype(F32))


def _sel_map_t(seq):
    nc = seq // CMP_STRIDE
    n_slc = seq // SLC_BLOCK
    c_start = np.arange(nc, dtype=np.int32) * CMP_STRIDE
    c_end = c_start + CMP_LEN - 1
    s_start = np.arange(n_slc, dtype=np.int32) * SLC_BLOCK
    m = (c_start[None, :] < s_start[:, None] + SLC_BLOCK) & (s_start[:, None] <= c_end[None, :])
    return m.astype(np.float32)


def _nsa_attn_kernel(cst_ref, q_ref, ksl_ref, vsl_ref, kw_ref, vw_ref, kc_ref, vc_ref, gt_ref,
                     dtab_ref, tt_ref, smt_ref, o_ref,
                     kaug_ref, vaug_ref, qaug_ref, vct_ref, s_ref, p_ref, m_ref, alpha_ref,
                     acc_ref, osel_ref, *, n_sel):
    qb = NSA_QB
    dh = HEAD_DIM
    grp = NSA_GROUP
    nrows = grp * qb
    rblk = NSA_ROWBLK
    g = pl.program_id(1)
    qi = pl.program_id(2)
    seq = ksl_ref.shape[0]
    nc = kc_ref.shape[0]
    n_slc = seq // SLC_BLOCK
    blocks_per_chunk = qb // SLC_BLOCK

    @pl.when(qi == 0)
    def _():
        lane = lax.broadcasted_iota(jnp.int32, (seq, dh), 1)
        rb = lax.broadcasted_iota(jnp.int32, (seq, dh), 0) // SLC_BLOCK
        ones_tail = jnp.where(lane >= dh - 2, 1.0, 0.0)
        kaug_ref[0, :, 0:dh] = ksl_ref[...]
        kaug_ref[0, :, dh:] = (jnp.where(rb == lane, -SEL_PENALTY, 0.0) + ones_tail).astype(BF16)
        kaug_ref[1, :, 0:dh] = kw_ref[...]
        kaug_ref[1, :, dh:] = ones_tail.astype(BF16)
        ones = jnp.ones((seq, dh), BF16)
        vaug_ref[0, :, 0:dh] = vsl_ref[...]
        vaug_ref[0, :, dh:] = ones
        vaug_ref[1, :, 0:dh] = vw_ref[...]
        vaug_ref[1, :, dh:] = ones
        vct_ref[...] = vc_ref[...].astype(F32).T.astype(BF16)

    for r in range(grp):
        qaug_ref[r * qb:(r + 1) * qb, 0:dh] = q_ref[:, r * dh:(r + 1) * dh]
    q4 = qaug_ref[:, 0:dh]

    st = _dot_nt(kc_ref[...], q4)
    start = pl.multiple_of(nc + CMP_STRIDE - (qb // CMP_STRIDE) * qi, CMP_STRIDE)
    psum = jnp.zeros((nc, qb), F32)
    o_cmp = []
    for r in range(grp):
        bias = tt_ref[r, pl.ds(start, nc), :]
        s = st[:, r * qb:(r + 1) * qb] + bias
        valid = bias > 0.5 * NEG
        m = jnp.max(s, axis=0, keepdims=True)
        p = jnp.where(valid, jnp.exp2(s - m), 0.0)
        p = p / jnp.maximum(jnp.sum(p, axis=0, keepdims=True), 1e-30)
        psum = psum + p
        o_cmp.append(_dot(vct_ref[...], p.astype(BF16)).T)

    ph = psum.astype(BF16)
    pl_ = (psum - ph.astype(F32)).astype(BF16)
    smt = smt_ref[...]
    imp = _dot(smt, ph) + _dot(smt, pl_)
    jb = lax.broadcasted_iota(jnp.int32, (n_slc, qb), 0)
    tb = qi * blocks_per_chunk + lax.broadcasted_iota(jnp.int32, (n_slc, qb), 1) // SLC_BLOCK
    forced = (jb == 0) | (jb == tb) | (jb == tb - 1)
    score = jnp.where(jb <= tb, jnp.where(forced, jnp.inf, imp), -jnp.inf)
    jbf = jb.astype(F32)
    notsel = jnp.ones((n_slc, qb), F32)
    for _ in range(n_sel):
        top = jnp.max(score, axis=0, keepdims=True)
        first = jnp.min(jnp.where(score == top, jbf, float(n_slc)), axis=0, keepdims=True)
        pick = (jbf == first) & (top > -jnp.inf)
        notsel = jnp.where(pick, 0.0, notsel)
        score = jnp.where(pick, -jnp.inf, score)
    notsel = jnp.concatenate([notsel, jnp.zeros((dh - n_slc, qb), F32)], axis=0).T
    lane1 = lax.broadcasted_iota(jnp.int32, (1, dh), 1)
    for r in range(grp):
        h = g * grp + r
        cvec = jnp.where(lane1 == dh - 2, cst_ref[0, h], jnp.where(lane1 == dh - 1, cst_ref[1, h], 0.0))
        qaug_ref[r * qb:(r + 1) * qb, dh:] = (notsel + cvec).astype(BF16)

    def reset():
        m_ref[...] = jnp.full(m_ref.shape, NEG, F32)
        acc_ref[...] = jnp.zeros(acc_ref.shape, F32)

    def softmax_rows(tile_idx):
        for rbi in range(nrows // rblk):
            rows = slice(rbi * rblk, (rbi + 1) * rblk)
            s0 = s_ref[rows, 0:dh]
            s1 = s_ref[rows, dh:]
            if tile_idx is not None:
                head = (rbi * rblk) // qb
                t0 = (rbi * rblk) % qb
                s0 = s0 + dtab_ref[head, tile_idx, t0:t0 + rblk, 0:dh]
                s1 = s1 + dtab_ref[head, tile_idx, t0:t0 + rblk, dh:]
            m_old = m_ref[rows, :]
            m_new = jnp.maximum(m_old, jnp.max(jnp.maximum(s0, s1), axis=-1, keepdims=True))
            alpha_ref[rows, :] = jnp.exp2(m_old - m_new)
            p_ref[rows, 0:dh] = jnp.exp2(s0 - m_new).astype(BF16)
            p_ref[rows, dh:] = jnp.exp2(s1 - m_new).astype(BF16)
            m_ref[rows, :] = m_new

    def flash_step(branch, kchunk, tile_idx):
        k0 = pl.multiple_of(kchunk * qb, qb)
        s_ref[...] = _dot_nt(qaug_ref[...], kaug_ref[branch, pl.ds(k0, qb), :])
        softmax_rows(tile_idx)
        pv = _dot(p_ref[...], vaug_ref[branch, pl.ds(k0, qb), :])
        alpha = alpha_ref[...]
        acc_ref[:, 0:dh] = acc_ref[:, 0:dh] * alpha + pv[:, 0:dh]
        acc_ref[:, dh:] = acc_ref[:, dh:] * alpha + pv[:, dh:]

    reset()

    def far_body(kchunk, carry):
        flash_step(0, kchunk, None)
        return carry

    lax.fori_loop(0, jnp.maximum(qi - 1, 0), far_body, 0)

    def near_body(w, carry):
        is_win = w >= 2
        branch = jnp.where(is_win, 1, 0)
        kchunk = jnp.where(is_win, qi - 4 + w, qi - 1 + w)
        tile_idx = jnp.where(is_win, 4 - w, 1 - w)

        @pl.when(w == 2)
        def _():
            osel_ref[...] = acc_ref[:, 0:dh] / acc_ref[:, dh:]
            reset()

        @pl.when(kchunk >= 0)
        def _():
            flash_step(branch, kchunk, tile_idx)

        return carry

    lax.fori_loop(0, 5, near_body, 0)

    gates = _sigmoid(gt_ref[...])
    for r in range(grp):
        rows = slice(r * qb, (r + 1) * qb)
        o_win = acc_ref[rows, 0:dh] / acc_ref[rows, dh:]
        o = (gates[:, 3 * r:3 * r + 1] * o_cmp[r] + gates[:, 3 * r + 1:3 * r + 2] * osel_ref[rows, :]
             + gates[:, 3 * r + 2:3 * r + 3] * o_win)
        o_ref[:, r * dh:(r + 1) * dh] = o.astype(o_ref.dtype)


def nsa_attention(proj, aux, kvc, rel_bias, *, batch, seq, n_groups):
    qb = NSA_QB
    dh = HEAD_DIM
    grp = NSA_GROUP
    nq = seq // qb
    nc = seq // CMP_STRIDE
    n_slc = seq // SLC_BLOCK
    n_sel = min(SLC_TOPK, n_slc)
    assert n_slc <= dh - 2
    nrows = grp * qb
    dtab, tt = nsa_bias_tables(rel_bias, seq)
    smt = jnp.asarray(_sel_map_t(seq), BF16)
    c31 = rel_bias[NUM_BUCKETS - 1].astype(F32) * LOG2E
    c31_hi = c31.astype(BF16).astype(F32)
    cst = jnp.stack([c31_hi, c31 - c31_hi])
    kv_base = n_groups * grp
    kv_map = lambda j: (lambda b, g, i, c: (b, kv_base + j * n_groups + g))
    grid_spec = pltpu.PrefetchScalarGridSpec(
        num_scalar_prefetch=1,
        grid=(batch, n_groups, nq),
        in_specs=[
            pl.BlockSpec((qb, grp * dh), lambda b, g, i, c: (b * nq + i, g)),
            pl.BlockSpec((seq, dh), kv_map(0)),
            pl.BlockSpec((seq, dh), kv_map(1)),
            pl.BlockSpec((seq, dh), kv_map(2)),
            pl.BlockSpec((seq, dh), kv_map(3)),
            pl.BlockSpec((None, None, None, nc, dh), lambda b, g, i, c: (0, b, g, 0, 0)),
            pl.BlockSpec((None, None, None, nc, dh), lambda b, g, i, c: (1, b, g, 0, 0)),
            pl.BlockSpec((qb, 128), lambda b, g, i, c: (b * nq + i, 2 * n_groups + g)),
            pl.BlockSpec((grp, 3, qb, qb), lambda b, g, i, c: (g, 0, 0, 0)),
            pl.BlockSpec((grp, tt.shape[1], qb), lambda b, g, i, c: (g, 0, 0)),
            pl.BlockSpec((n_slc, nc), lambda b, g, i, c: (0, 0)),
        ],
        out_specs=pl.BlockSpec((qb, grp * dh), lambda b, g, i, c: (b * nq + i, g)),
        scratch_shapes=[
            pltpu.VMEM((2, seq, 2 * dh), BF16),
            pltpu.VMEM((2, seq, 2 * dh), BF16),
            pltpu.VMEM((nrows, 2 * dh), BF16),
            pltpu.VMEM((dh, nc), BF16),
            pltpu.VMEM((nrows, qb), F32),
            pltpu.VMEM((nrows, qb), BF16),
            pltpu.VMEM((nrows, dh), F32),
            pltpu.VMEM((nrows, dh), F32),
            pltpu.VMEM((nrows, 2 * dh), F32),
            pltpu.VMEM((nrows, dh), F32),
        ],
    )
    return pl.pallas_call(
        functools.partial(_nsa_attn_kernel, n_sel=n_sel),
        grid_spec=grid_spec,
        out_shape=jax.ShapeDtypeStruct((batch * seq, n_groups * grp * dh), BF16),
        compiler_params=_cparams(("parallel", "parallel", "arbitrary")),
        name="nsa_attention",
    )(cst, proj, proj, proj, proj, proj, kvc, kvc, aux, dtab, tt, smt)


def _gdn_in_weights(w_in, n_heads):
    hp = GDN_HEADS_PER_STEP
    wd = n_heads * HEAD_DIM
    ng = n_heads // hp
    main = jnp.concatenate([w_in[:, :4 * wd], w_in[:, 4 * wd + 2 * n_heads:]], axis=1)
    wb = w_in[:, 4 * wd:4 * wd + n_heads].reshape(-1, ng, hp)
    wa = w_in[:, 4 * wd + n_heads:4 * wd + 2 * n_heads].reshape(-1, ng, hp)
    gate = jnp.concatenate([wb, wa, jnp.zeros((w_in.shape[0], ng, 128 - 2 * hp), w_in.dtype)], axis=2)
    return main.astype(BF16), gate.reshape(w_in.shape[0], ng * 128).astype(BF16)


def _nsa_in_weights(w_in, n_heads):
    grp = NSA_GROUP
    ng = n_heads // grp
    qw = n_heads * HEAD_DIM
    kvw = ng * HEAD_DIM
    main = jnp.concatenate([w_in[:, :qw] * (HEAD_DIM ** -0.5 * LOG2E), w_in[:, qw + 2 * kvw:qw + 6 * kvw],
                            w_in[:, qw + 6 * kvw + 3 * n_heads:]], axis=1)
    cmp_w = w_in[:, qw:qw + 2 * kvw]
    wg = w_in[:, qw + 6 * kvw:qw + 6 * kvw + 3 * n_heads].reshape(-1, ng, 3 * grp)
    gate = jnp.concatenate([wg, jnp.zeros((w_in.shape[0], ng, 128 - 3 * grp), w_in.dtype)], axis=2)
    aux = jnp.concatenate([cmp_w, gate.reshape(w_in.shape[0], ng * 128)], axis=1)
    return main.astype(BF16), aux.astype(BF16)


def _pick(n, candidates):
    for c in candidates:
        if n % c == 0:
            return c
    return n


def kernel(x, mem, rel_bias, norm_mix_w, norm_ffn_w, final_norm_w, mem_norm_w, mem_w_kv, w_out, gdn_w_in, gdn_conv_w, gdn_a_log, gdn_dt_bias, gdn_norm_w, nsa_w_in, nsa_cmp_pos_k, nsa_cmp_w1_k, nsa_cmp_w2_k, nsa_cmp_pos_v, nsa_cmp_w1_v, nsa_cmp_w2_v, ffn_w_up, ffn_conv_w, ffn_conv_b, ffn_w_down):
    batch, seq, d_model = x.shape
    depth = norm_mix_w.shape[0]
    n_heads = d_model // HEAD_DIM
    m_tok = mem.shape[1]
    mw = MEM_HEADS * HEAD_DIM
    d_ff = ffn_w_down.shape[1]
    t = batch * seq
    xf = x.reshape(t, d_model)
    memf = mem.reshape(batch * m_tok, d_model)
    tm = _pick(seq, (1024, 512, 256, 128))
    tm_small = _pick(seq, (512, 256, 128))

    for i in range(depth):
        j = i // 2
        kvw = mem_w_kv[i].astype(BF16)
        kv, _ = norm_matmul(memf, mem_norm_w[i], kvw, kvw[:, :128],
                            tm=_pick(batch * m_tok, (512, 256)), tn=_pick(kvw.shape[1], (512, 256)))
        if i % 2 == 0:
            w_main, w_aux = _gdn_in_weights(gdn_w_in[j], n_heads)
            proj, aux = norm_matmul(xf, norm_mix_w[i], w_main, w_aux, tm=tm_small,
                                    tn=_pick(w_main.shape[1], (512, 256)))
            mix = gdn_mixer(proj, aux, gdn_conv_w[j], gdn_a_log[j], gdn_dt_bias[j], gdn_norm_w[j],
                            batch=batch, seq=seq, lblk=_pick(seq, (512, 256, 128, 64)))
            qm_block = 4 * n_heads * HEAD_DIM // mw
        else:
            ng = n_heads // NSA_GROUP
            w_main, w_aux = _nsa_in_weights(nsa_w_in[j], n_heads)
            proj, aux = norm_matmul(xf, norm_mix_w[i], w_main, w_aux, tm=tm_small,
                                    tn=_pick(w_main.shape[1], (512, 256)))
            pos = jnp.stack([nsa_cmp_pos_k[j], nsa_cmp_pos_v[j]]).astype(F32)
            w1 = jnp.stack([nsa_cmp_w1_k[j], nsa_cmp_w1_v[j]]).astype(BF16)
            w2 = jnp.stack([nsa_cmp_w2_k[j], nsa_cmp_w2_v[j]]).astype(BF16)
            kvc = nsa_compress(aux, pos, w1, w2, batch=batch, seq=seq, n_groups=ng)
            mix = nsa_attention(proj, aux, kvc, rel_bias, batch=batch, seq=seq, n_groups=ng)
            qm_block = (n_heads * HEAD_DIM + 4 * ng * HEAD_DIM) // mw
        mo = mem_attention(proj, kv, batch=batch, seq=seq, q_col_block=qm_block, ts=tm)
        wo = w_out[i].astype(BF16)
        xf = out_proj(xf, mix, mo, wo[:n_heads * HEAD_DIM], wo[n_heads * HEAD_DIM:], tm=tm_small)
        wu = ffn_w_up[i].astype(BF16)
        act = ffn_up(xf, norm_ffn_w[i], wu[:, :d_ff], wu[:, d_ff:], ffn_conv_w[i], ffn_conv_b[i],
                     seq=seq, tm=tm_small, tn=_pick(d_ff, (256, 128)))
        xf = ffn_down(xf, act, ffn_w_down[i].astype(BF16), final_norm_w, tm=tm_small,
                      final_norm=(i == depth - 1))
    return xf.reshape(batch, seq, d_model)
```

```python
import functools
import math

import jax
import jax.numpy as jnp
import numpy as np
from jax import lax
from jax.experimental import pallas as pl
from jax.experimental.pallas import tpu as pltpu

F32 = jnp.float32
BF16 = jnp.bfloat16

HEAD_DIM = 128
GDN_CONV = 4
GDN_CHUNK = 64
GDN_HEADS_PER_STEP = 4
GDN_SUPER = 256
NSA_GROUP = 4
CMP_LEN = 32
CMP_STRIDE = 16
SLC_BLOCK = 64
SLC_TOPK = 16
WINDOW = 512
NSA_QB = 256
NSA_ROWBLK = 128
NSA_STEPS = 5
MEM_HEADS = 4
NUM_BUCKETS = 32
MAX_DISTANCE = 128
FFN_CONV = 3
EPS = 1e-6
LOG2E = math.log2(math.e)
NEG = -1e30
SEL_PENALTY = 32768.0
VMEM_LIMIT = 56 * 1024 * 1024


def _cparams(sem):
    return pltpu.CompilerParams(dimension_semantics=sem, vmem_limit_bytes=VMEM_LIMIT)


def _dot(a, b):
    return jnp.dot(a, b, preferred_element_type=F32)


def _dot_nt(a, b):
    return lax.dot_general(a, b, (((1,), (1,)), ((), ())), preferred_element_type=F32)


def _silu(x):
    return x * (1.0 / (1.0 + jnp.exp(-x)))


def _sigmoid(x):
    return 1.0 / (1.0 + jnp.exp(-x))


def _softplus(x):
    return jnp.maximum(x, 0.0) + jnp.log(1.0 + jnp.exp(-jnp.abs(x)))


def _split3(x):
    h = x.astype(BF16)
    r = x - h.astype(F32)
    m = r.astype(BF16)
    l = (r - m.astype(F32)).astype(BF16)
    return h, m, l


def _norm_mm_kernel(x_ref, nw_ref, w_ref, waux_ref, o_ref, oaux_ref, *, rs, tn):
    for r0 in range(0, x_ref.shape[0], rs):
        rows = slice(r0, r0 + rs)
        x = x_ref[rows, :]
        ms = jnp.mean(x * x, axis=-1, keepdims=True)
        xn = (x * lax.rsqrt(ms + EPS) * nw_ref[...]).astype(BF16)
        oaux_ref[rows, :] = _dot(xn, waux_ref[...])
        for c0 in range(0, o_ref.shape[1], tn):
            o_ref[rows, c0:c0 + tn] = _dot(xn, w_ref[:, c0:c0 + tn]).astype(o_ref.dtype)


def norm_matmul(x, nw, w, waux, *, tm, tn):
    t, d = x.shape
    n = w.shape[1]
    na = waux.shape[1]
    whole = lambda i: (0, 0)
    return pl.pallas_call(
        functools.partial(_norm_mm_kernel, rs=min(tm, 256), tn=tn),
        grid=(t // tm,),
        in_specs=[
            pl.BlockSpec((tm, d), lambda i: (i, 0)),
            pl.BlockSpec((1, d), whole),
            pl.BlockSpec((d, n), whole),
            pl.BlockSpec((d, na), whole),
        ],
        out_specs=[
            pl.BlockSpec((tm, n), lambda i: (i, 0)),
            pl.BlockSpec((tm, na), lambda i: (i, 0)),
        ],
        out_shape=[jax.ShapeDtypeStruct((t, n), BF16), jax.ShapeDtypeStruct((t, na), F32)],
        compiler_params=_cparams(("parallel",)),
        name="norm_matmul",
    )(x, nw.reshape(1, d), w, waux)


def _mem_attn_kernel(q_ref, kv_ref, o_ref):
    scale = HEAD_DIM ** -0.5
    mw = MEM_HEADS * HEAD_DIM
    for h in range(MEM_HEADS):
        lo = h * HEAD_DIM
        q = q_ref[:, lo:lo + HEAD_DIM]
        k = kv_ref[:, lo:lo + HEAD_DIM]
        v = kv_ref[:, mw + lo:mw + lo + HEAD_DIM]
        s = _dot_nt(q, k) * scale
        m = jnp.max(s, axis=-1, keepdims=True)
        p = jnp.exp(s - m)
        l = jnp.sum(p, axis=-1, keepdims=True)
        o = _dot(p.astype(BF16), v) / l
        o_ref[:, lo:lo + HEAD_DIM] = o.astype(o_ref.dtype)


def mem_attention(proj, kv, *, batch, seq, q_col_block, ts):
    mw = MEM_HEADS * HEAD_DIM
    m_tok = kv.shape[0] // batch
    nt = seq // ts
    return pl.pallas_call(
        _mem_attn_kernel,
        grid=(batch, nt),
        in_specs=[
            pl.BlockSpec((ts, mw), lambda b, i: (b * nt + i, q_col_block)),
            pl.BlockSpec((m_tok, 2 * mw), lambda b, i: (b, 0)),
        ],
        out_specs=pl.BlockSpec((ts, mw), lambda b, i: (b * nt + i, 0)),
        out_shape=jax.ShapeDtypeStruct((batch * seq, mw), BF16),
        compiler_params=_cparams(("parallel", "parallel")),
        name="mem_attention",
    )(proj, kv)


def _out_proj_kernel(x_ref, a_ref, b_ref, wa_ref, wb_ref, o_ref):
    o_ref[...] = x_ref[...] + _dot(a_ref[...], wa_ref[...]) + _dot(b_ref[...], wb_ref[...])


def out_proj(x, a, b, wa, wb, *, tm):
    t, d = x.shape
    ka, kb = a.shape[1], b.shape[1]
    return pl.pallas_call(
        _out_proj_kernel,
        grid=(t // tm,),
        in_specs=[
            pl.BlockSpec((tm, d), lambda i: (i, 0)),
            pl.BlockSpec((tm, ka), lambda i: (i, 0)),
            pl.BlockSpec((tm, kb), lambda i: (i, 0)),
            pl.BlockSpec((ka, d), lambda i: (0, 0)),
            pl.BlockSpec((kb, d), lambda i: (0, 0)),
        ],
        out_specs=pl.BlockSpec((tm, d), lambda i: (i, 0)),
        out_shape=jax.ShapeDtypeStruct((t, d), F32),
        compiler_params=_cparams(("parallel",)),
        name="out_proj",
    )(x, a, b, wa, wb)


def _ffn_up_kernel(x_ref, nw_ref, wg_ref, wv_ref, cw_ref, cb_ref, o_ref, gs_ref, carry_ref,
                   *, tiles_per_seq, tn):
    i = pl.program_id(0)
    tm = x_ref.shape[0]
    dff = o_ref.shape[1]
    x = x_ref[...]
    ms = jnp.mean(x * x, axis=-1, keepdims=True)
    xn = (x * lax.rsqrt(ms + EPS) * nw_ref[...]).astype(BF16)
    first = i % tiles_per_seq == 0
    for j in range(dff // tn):
        cols = slice(j * tn, (j + 1) * tn)
        g = _dot(xn, wg_ref[:, cols])
        v = _dot(xn, wv_ref[:, cols])
        gs_ref[0:8, cols] = jnp.where(first, 0.0, carry_ref[:, cols])
        gs_ref[8:, cols] = g
        carry_ref[:, cols] = g[tm - 8:, :]
        conv = (gs_ref[6:6 + tm, cols] * cw_ref[0:1, cols] + gs_ref[7:7 + tm, cols] * cw_ref[1:2, cols]
                + g * cw_ref[2:3, cols] + cb_ref[:, cols])
        o_ref[:, cols] = (_silu(conv) * v).astype(o_ref.dtype)


def ffn_up(x, nw, wg, wv, cw, cb, *, seq, tm, tn):
    t, d = x.shape
    dff = wg.shape[1]
    kern = functools.partial(_ffn_up_kernel, tiles_per_seq=seq // tm, tn=tn)
    whole = lambda i: (0, 0)
    return pl.pallas_call(
        kern,
        grid=(t // tm,),
        in_specs=[
            pl.BlockSpec((tm, d), lambda i: (i, 0)),
            pl.BlockSpec((1, d), whole),
            pl.BlockSpec((d, dff), whole),
            pl.BlockSpec((d, dff), whole),
            pl.BlockSpec((8, dff), whole),
            pl.BlockSpec((1, dff), whole),
        ],
        out_specs=pl.BlockSpec((tm, dff), lambda i: (i, 0)),
        out_shape=jax.ShapeDtypeStruct((t, dff), BF16),
        scratch_shapes=[
            pltpu.VMEM((tm + 8, dff), F32),
            pltpu.VMEM((8, dff), F32),
        ],
        compiler_params=_cparams(("arbitrary",)),
        name="ffn_up",
    )(x, nw.reshape(1, d), wg, wv, jnp.pad(cw, ((0, 8 - cw.shape[0]), (0, 0))), cb.reshape(1, dff))


def _ffn_down_kernel(x_ref, a_ref, w_ref, fw_ref, o_ref, *, final_norm):
    y = x_ref[...] + _dot(a_ref[...], w_ref[...])
    if final_norm:
        ms = jnp.mean(y * y, axis=-1, keepdims=True)
        y = y * lax.rsqrt(ms + EPS) * fw_ref[...]
    o_ref[...] = y


def ffn_down(x, a, w, fw, *, tm, final_norm):
    t, d = x.shape
    k = a.shape[1]
    return pl.pallas_call(
        functools.partial(_ffn_down_kernel, final_norm=final_norm),
        grid=(t // tm,),
        in_specs=[
            pl.BlockSpec((tm, d), lambda i: (i, 0)),
            pl.BlockSpec((tm, k), lambda i: (i, 0)),
            pl.BlockSpec((k, d), lambda i: (0, 0)),
            pl.BlockSpec((1, d), lambda i: (0, 0)),
        ],
        out_specs=pl.BlockSpec((tm, d), lambda i: (i, 0)),
        out_shape=jax.ShapeDtypeStruct((t, d), F32),
        compiler_params=_cparams(("parallel",)),
        name="ffn_down",
    )(x, a, w, fw.reshape(1, d))


def _gdn_kernel(q_ref, k_ref, v_ref, z_ref, gt_ref, cw_ref, arow_ref, dtrow_ref, nw_ref, o_ref,
                qs_ref, ks_ref, vs_ref, state_ref, *, n_chunks):
    hp = GDN_HEADS_PER_STEP
    dh = HEAD_DIM
    c = GDN_CHUNK
    lblk = q_ref.shape[0]
    sb = pl.program_id(2)

    @pl.when(sb == 0)
    def _():
        zero8 = jnp.zeros((8, hp * dh), F32)
        qs_ref[0:8, :] = zero8
        ks_ref[0:8, :] = zero8
        vs_ref[0:8, :] = zero8
        state_ref[...] = jnp.zeros(state_ref.shape, F32)

    for idx, (src, dst) in enumerate(((q_ref, qs_ref), (k_ref, ks_ref), (v_ref, vs_ref))):
        dst[8:, :] = src[...].astype(F32)
        w = cw_ref[idx]
        y = (dst[5:5 + lblk, :] * w[0:1, :] + dst[6:6 + lblk, :] * w[1:2, :]
             + dst[7:7 + lblk, :] * w[2:3, :] + dst[8:8 + lblk, :] * w[3:4, :])
        tail = dst[lblk:lblk + 8, :]
        dst[8:, :] = _silu(y)
        dst[0:8, :] = tail

    sl = GDN_SUPER
    cps = sl // c
    row = lax.broadcasted_iota(jnp.int32, (sl, sl), 0)
    col = lax.broadcasted_iota(jnp.int32, (sl, sl), 1)
    same = (row // c) == (col // c)
    incl = same & (row >= col)
    strict = same & (row > col)
    tril = jnp.where(incl, 1.0, 0.0).astype(BF16)
    ones_blk = jnp.where(same, 1.0, 0.0).astype(BF16)
    arow = arow_ref[0]
    dtrow = dtrow_ref[0]
    nw = nw_ref[...]
    states = [state_ref[r] for r in range(hp)]
    zeros_c = jnp.zeros((c, dh), BF16)

    for sci in range(lblk // sl):
        r0 = sci * sl
        gt = gt_ref[r0:r0 + sl, :]
        beta_all = _sigmoid(gt)
        g_all = -arow * _softplus(gt + dtrow)
        gh, gm, gl = _split3(g_all)
        gc_all = _dot(tril, gh) + _dot(tril, gm) + _dot(tril, gl)
        gend_all = _dot(ones_blk, gh) + _dot(ones_blk, gm) + _dot(ones_blk, gl)
        gc_t = gc_all.T
        pre = []
        for r in range(hp):
            lo = r * dh
            qh = qs_ref[r0 + 8:r0 + 8 + sl, lo:lo + dh]
            kh = ks_ref[r0 + 8:r0 + 8 + sl, lo:lo + dh]
            vh = vs_ref[r0 + 8:r0 + 8 + sl, lo:lo + dh]
            qn = qh * lax.rsqrt(jnp.sum(qh * qh, axis=-1, keepdims=True) + EPS) * (dh ** -0.5)
            kn = kh * lax.rsqrt(jnp.sum(kh * kh, axis=-1, keepdims=True) + EPS)
            beta = beta_all[:, r:r + 1]
            gc_col = gc_all[:, hp + r:hp + r + 1]
            gc_row = gc_t[hp + r:hp + r + 1, :]
            gend = gend_all[:, hp + r:hp + r + 1]
            decay = jnp.where(incl, jnp.exp(jnp.where(incl, gc_col - gc_row, 0.0)), 0.0)
            eg = jnp.exp(gc_col)
            kb = kn * beta
            kn_b = kn.astype(BF16)
            lmat = jnp.where(strict, _dot_nt(kb.astype(BF16), kn_b) * decay, 0.0)
            a_intra = jnp.where(incl, _dot_nt(qn.astype(BF16), kn_b) * decay, 0.0).astype(BF16)
            x = jnp.concatenate([vh * beta, kb * eg], axis=1)
            mp = -lmat
            for it in range(6):
                mp_b = mp.astype(BF16)
                x = x + _dot(mp_b, x.astype(BF16))
                if it < 5:
                    mp = _dot(mp_b, mp_b)
            pre.append((x[:, :dh], x[:, dh:], qn * eg, kn * jnp.exp(gend - gc_col), a_intra,
                        jnp.exp(gend)))

        for ci in range(cps):
            rows = slice(ci * c, (ci + 1) * c)
            for r in range(hp):
                lo = r * dh
                u, w, qd, kd, a_intra, egend = pre[r]
                state = states[r]
                wq = jnp.concatenate([w[rows], qd[rows]], axis=0).astype(BF16)
                wqs = _dot(wq, state.astype(BF16))
                v_new = u[rows] - wqs[:c]
                v_new_b = v_new.astype(BF16)
                vcat = jnp.concatenate([zeros_c] * ci + [v_new_b] + [zeros_c] * (cps - 1 - ci), axis=0)
                o = wqs[c:] + _dot(a_intra[rows], vcat)
                states[r] = (state * egend[ci * c:ci * c + 1, :]
                             + _dot(kd[rows].T.astype(BF16), v_new_b))
                ms = jnp.mean(o * o, axis=-1, keepdims=True)
                zz = z_ref[r0 + ci * c:r0 + (ci + 1) * c, lo:lo + dh].astype(F32)
                o = o * lax.rsqrt(ms + EPS) * nw * _silu(zz)
                o_ref[r0 + ci * c:r0 + (ci + 1) * c, lo:lo + dh] = o.astype(o_ref.dtype)

    for r in range(hp):
        state_ref[r] = states[r]


def gdn_mixer(proj, gates, conv_w, a_log, dt_bias, norm_w, *, batch, seq, lblk):
    hp = GDN_HEADS_PER_STEP
    dh = HEAD_DIM
    n_heads = a_log.shape[0]
    ng = n_heads // hp
    wd = n_heads * dh
    bw = hp * dh
    nsb = seq // lblk
    arow = jnp.zeros((ng, 1, 128), F32).at[:, 0, hp:2 * hp].set(jnp.exp(a_log.astype(F32)).reshape(ng, hp))
    dtrow = jnp.zeros((ng, 1, 128), F32).at[:, 0, hp:2 * hp].set(dt_bias.astype(F32).reshape(ng, hp))
    cw = conv_w.reshape(GDN_CONV, 3, ng, bw).transpose(1, 2, 0, 3)
    cw = jnp.pad(cw, ((0, 0), (0, 0), (0, 8 - GDN_CONV), (0, 0)))
    kern = functools.partial(_gdn_kernel, n_chunks=lblk // GDN_CHUNK)
    row_map = lambda off: (lambda b, g, s: (b * nsb + s, off + g))
    return pl.pallas_call(
        kern,
        grid=(batch, ng, nsb),
        in_specs=[
            pl.BlockSpec((lblk, bw), row_map(0)),
            pl.BlockSpec((lblk, bw), row_map(ng)),
            pl.BlockSpec((lblk, bw), row_map(2 * ng)),
            pl.BlockSpec((lblk, bw), row_map(3 * ng)),
            pl.BlockSpec((lblk, 128), row_map(0)),
            pl.BlockSpec((3, None, 8, bw), lambda b, g, s: (0, g, 0, 0)),
            pl.BlockSpec((None, 1, 128), lambda b, g, s: (g, 0, 0)),
            pl.BlockSpec((None, 1, 128), lambda b, g, s: (g, 0, 0)),
            pl.BlockSpec((1, dh), lambda b, g, s: (0, 0)),
        ],
        out_specs=pl.BlockSpec((lblk, bw), row_map(0)),
        out_shape=jax.ShapeDtypeStruct((batch * seq, wd), BF16),
        scratch_shapes=[
            pltpu.VMEM((lblk + 8, bw), F32),
            pltpu.VMEM((lblk + 8, bw), F32),
            pltpu.VMEM((lblk + 8, bw), F32),
            pltpu.VMEM((hp, dh, dh), F32),
        ],
        compiler_params=_cparams(("parallel", "parallel", "arbitrary")),
        name="gdn_mixer",
    )(proj, proj, proj, proj, gates, cw, arow, dtrow, norm_w.reshape(1, dh).astype(F32))


def _nsa_compress_kernel(t_ref, pos_ref, w1_ref, w2_ref, o_ref):
    nc = t_ref.shape[0] // CMP_STRIDE
    dh = HEAD_DIM
    acc_a = jnp.zeros((nc, dh), F32)
    acc_b = jnp.zeros((nc, dh), F32)
    for i in range(CMP_STRIDE):
        xi = t_ref[pl.ds(i, nc, stride=CMP_STRIDE), :]
        xa = (xi + pos_ref[i:i + 1, :]).astype(BF16)
        xb = (xi + pos_ref[CMP_STRIDE + i:CMP_STRIDE + i + 1, :]).astype(BF16)
        acc_a = acc_a + _dot(xa, w1_ref[i * dh:(i + 1) * dh, :])
        acc_b = acc_b + _dot(xb, w1_ref[(CMP_STRIDE + i) * dh:(CMP_STRIDE + i + 1) * dh, :])
    h = acc_a + jnp.concatenate([acc_b[1:, :], acc_b[:1, :]], axis=0)
    o_ref[...] = _dot(_silu(h).astype(BF16), w2_ref[...]).astype(o_ref.dtype)


def nsa_compress(aux, pos, w1, w2, *, batch, seq, n_groups):
    dh = HEAD_DIM
    nc = seq // CMP_STRIDE
    return pl.pallas_call(
        _nsa_compress_kernel,
        grid=(batch, 2, n_groups),
        in_specs=[
            pl.BlockSpec((seq, dh), lambda b, j, g: (b, j * n_groups + g)),
            pl.BlockSpec((None, CMP_LEN, dh), lambda b, j, g: (j, 0, 0)),
            pl.BlockSpec((None, CMP_LEN * dh, dh), lambda b, j, g: (j, 0, 0)),
            pl.BlockSpec((None, dh, dh), lambda b, j, g: (j, 0, 0)),
        ],
        out_specs=pl.BlockSpec((None, None, None, nc, dh), lambda b, j, g: (j, b, g, 0, 0)),
        out_shape=jax.ShapeDtypeStruct((2, batch, n_groups, nc, dh), BF16),
        compiler_params=_cparams(("parallel", "parallel", "parallel")),
        name="nsa_compress",
    )(aux, pos, w1, w2)


def _t5_bucket_np(n):
    max_exact = NUM_BUCKETS // 2
    nf = np.maximum(n, 1).astype(np.float32)
    logv = np.log(nf / np.float32(max_exact)) / np.float32(math.log(MAX_DISTANCE / max_exact))
    large = max_exact + (logv * np.float32(NUM_BUCKETS - max_exact)).astype(np.int32)
    large = np.minimum(large, NUM_BUCKETS - 1)
    return np.where(n < max_exact, n, large).astype(np.int32)


def _t5_thresholds():
    n = np.arange(0, MAX_DISTANCE + 1, dtype=np.int32)
    b = _t5_bucket_np(n)
    half = NUM_BUCKETS // 2
    return tuple(int(np.min(n[b >= half + k])) for k in range(1, NUM_BUCKETS - half))


def _nsa_bias_kernel(tab_ref, dtab_ref, tt_ref, *, thresholds, nc):
    h = pl.program_id(0)
    qb = NSA_QB
    half = NUM_BUCKETS // 2
    c31 = tab_ref[NUM_BUCKETS - 1, h]

    def lookup(dist):
        n = jnp.maximum(dist, 0)
        big = jnp.full(n.shape, half, jnp.int32)
        for t in thresholds:
            big = big + jnp.where(n >= t, 1, 0)
        bucket = jnp.where(n < half, n, big)
        val = jnp.zeros(n.shape, F32)
        for b in range(NUM_BUCKETS):
            val = jnp.where(bucket == b, tab_ref[b, h], val)
        return val

    q = lax.broadcasted_iota(jnp.int32, (qb, qb), 0)
    kk = lax.broadcasted_iota(jnp.int32, (qb, qb), 1)
    dtab_ref[0] = jnp.where(q >= kk, (lookup(q - kk) - c31) * LOG2E, NEG)
    dtab_ref[1] = (lookup(q - kk + qb) - c31) * LOG2E
    dtab_ref[2] = jnp.where(kk > q, 0.0, NEG)
    dtab_ref[3] = jnp.full((qb, qb), NEG, F32)
    nw = 2 * qb // CMP_STRIDE
    x = lax.broadcasted_iota(jnp.int32, (nw, qb), 0)
    ql = lax.broadcasted_iota(jnp.int32, (nw, qb), 1)
    dist = ql - CMP_STRIDE * x + (qb - CMP_LEN + 1)
    tt_ref[0:nc, :] = jnp.zeros((nc, qb), F32) + c31 * LOG2E
    tt_ref[nc:nc + nw, :] = jnp.where(dist >= 0, lookup(dist) * LOG2E, NEG)
    tt_ref[nc + nw:, :] = jnp.full((nc, qb), NEG, F32)


def nsa_bias_tables(rel_bias, seq):
    qb = NSA_QB
    nc = seq // CMP_STRIDE
    nw = 2 * qb // CMP_STRIDE
    n_heads = rel_bias.shape[1]
    kern = functools.partial(_nsa_bias_kernel, thresholds=_t5_thresholds(), nc=nc)
    return pl.pallas_call(
        kern,
        grid=(n_heads,),
        in_specs=[pl.BlockSpec(memory_space=pltpu.SMEM)],
        out_specs=[
            pl.BlockSpec((None, 4, qb, qb), lambda h: (h, 0, 0, 0)),
            pl.BlockSpec((None, 2 * nc + nw, qb), lambda h: (h, 0, 0)),
        ],
        out_shape=[jax.ShapeDtypeStruct((n_heads, 4, qb, qb), F32),
                   jax.ShapeDtypeStruct((n_heads, 2 * nc + nw, qb), F32)],
        compiler_params=_cparams(("parallel",)),
        name="nsa_bias_tables",
    )(rel_bias.astype(F32))


def _sel_map_t(seq):
    nc = seq // CMP_STRIDE
    n_slc = seq // SLC_BLOCK
    c_start = np.arange(nc, dtype=np.int32) * CMP_STRIDE
    c_end = c_start + CMP_LEN - 1
    s_start = np.arange(n_slc, dtype=np.int32) * SLC_BLOCK
    m = (c_start[None, :] < s_start[:, None] + SLC_BLOCK) & (s_start[:, None] <= c_end[None, :])
    return m.astype(np.float32)


def _nsa_attn_kernel(cst_ref, q_ref, ksl_ref, vsl_ref, kw_ref, vw_ref, kc_ref, vc_ref, gt_ref,
                     dtab_ref, tt_ref, smt_ref, o_ref,
                     kaug_ref, vaug_ref, qaug_ref, vct_ref, s_ref, p_ref, m_ref, alpha_ref,
                     acc_ref, *, n_sel):
    qb = NSA_QB
    dh = HEAD_DIM
    grp = NSA_GROUP
    nrows = grp * qb
    rblk = NSA_ROWBLK
    g = pl.program_id(1)
    qi = pl.program_id(2)
    seq = ksl_ref.shape[0]
    nc = kc_ref.shape[0]
    n_slc = seq // SLC_BLOCK
    blocks_per_chunk = qb // SLC_BLOCK

    @pl.when(qi == 0)
    def _():
        lane = lax.broadcasted_iota(jnp.int32, (seq, dh), 1)
        rb = lax.broadcasted_iota(jnp.int32, (seq, dh), 0) // SLC_BLOCK
        ones_tail = jnp.where(lane >= dh - 2, 1.0, 0.0)
        kaug_ref[0, :, 0:dh] = ksl_ref[...]
        kaug_ref[0, :, dh:] = (jnp.where(rb == lane, -SEL_PENALTY, 0.0) + ones_tail).astype(BF16)
        kaug_ref[1, :, 0:dh] = kw_ref[...]
        kaug_ref[1, :, dh:] = ones_tail.astype(BF16)
        ones = jnp.ones((seq, dh), BF16)
        vaug_ref[0, :, 0:dh] = vsl_ref[...]
        vaug_ref[0, :, dh:] = ones
        vaug_ref[1, :, 0:dh] = vw_ref[...]
        vaug_ref[1, :, dh:] = ones
        vct_ref[...] = vc_ref[...].astype(F32).T.astype(BF16)

    for r in range(grp):
        qaug_ref[r * qb:(r + 1) * qb, 0:dh] = q_ref[:, r * dh:(r + 1) * dh]
    q4 = qaug_ref[:, 0:dh]

    st = _dot_nt(kc_ref[...], q4)
    start = pl.multiple_of(nc + CMP_STRIDE - (qb // CMP_STRIDE) * qi, CMP_STRIDE)
    psum = jnp.zeros((nc, qb), F32)
    o_cmp = []
    for r in range(grp):
        bias = tt_ref[r, pl.ds(start, nc), :]
        s = st[:, r * qb:(r + 1) * qb] + bias
        valid = bias > 0.5 * NEG
        m = jnp.max(s, axis=0, keepdims=True)
        p = jnp.where(valid, jnp.exp2(s - m), 0.0)
        p = p / jnp.maximum(jnp.sum(p, axis=0, keepdims=True), 1e-30)
        psum = psum + p
        o_cmp.append(_dot(vct_ref[...], p.astype(BF16)).T)

    ph = psum.astype(BF16)
    pl_ = (psum - ph.astype(F32)).astype(BF16)
    smt = smt_ref[...]
    imp = _dot(smt, ph) + _dot(smt, pl_)
    jb = lax.broadcasted_iota(jnp.int32, (n_slc, qb), 0)
    tb = qi * blocks_per_chunk + lax.broadcasted_iota(jnp.int32, (n_slc, qb), 1) // SLC_BLOCK
    forced = (jb == 0) | (jb == tb) | (jb == tb - 1)
    score = jnp.where(jb <= tb, jnp.where(forced, jnp.inf, imp), -jnp.inf)
    jbf = jb.astype(F32)
    notsel = jnp.ones((n_slc, qb), F32)
    for _ in range(n_sel):
        top = jnp.max(score, axis=0, keepdims=True)
        first = jnp.min(jnp.where(score == top, jbf, float(n_slc)), axis=0, keepdims=True)
        pick = (jbf == first) & (top > -jnp.inf)
        notsel = jnp.where(pick, 0.0, notsel)
        score = jnp.where(pick, -jnp.inf, score)
    notsel = jnp.concatenate([notsel, jnp.zeros((dh - n_slc, qb), F32)], axis=0).T
    lane1 = lax.broadcasted_iota(jnp.int32, (1, dh), 1)
    for r in range(grp):
        h = g * grp + r
        cvec = jnp.where(lane1 == dh - 2, cst_ref[0, h], jnp.where(lane1 == dh - 1, cst_ref[1, h], 0.0))
        qaug_ref[r * qb:(r + 1) * qb, dh:] = (notsel + cvec).astype(BF16)

    m_ref[...] = jnp.full(m_ref.shape, NEG, F32)
    acc_ref[...] = jnp.zeros(acc_ref.shape, F32)

    def flash_steps(steps):
        for n, (_, branch, kchunk, _) in enumerate(steps):
            k0 = pl.multiple_of(kchunk * qb, qb)
            s_ref[n] = _dot_nt(qaug_ref[...], kaug_ref[branch, pl.ds(k0, qb), :])
        for n, (a, branch, kchunk, tile_idx) in enumerate(steps):
            for rbi in range(nrows // rblk):
                rows = slice(rbi * rblk, (rbi + 1) * rblk)
                s0 = s_ref[n, rows, 0:dh]
                s1 = s_ref[n, rows, dh:]
                if tile_idx is not None:
                    head = (rbi * rblk) // qb
                    t0 = (rbi * rblk) % qb
                    s0 = s0 + dtab_ref[head, tile_idx, t0:t0 + rblk, 0:dh]
                    s1 = s1 + dtab_ref[head, tile_idx, t0:t0 + rblk, dh:]
                m_old = m_ref[a, rows, :]
                m_new = jnp.maximum(m_old, jnp.max(jnp.maximum(s0, s1), axis=-1, keepdims=True))
                alpha_ref[n, rows, :] = jnp.exp2(m_old - m_new)
                p_ref[n, rows, 0:dh] = jnp.exp2(s0 - m_new).astype(BF16)
                p_ref[n, rows, dh:] = jnp.exp2(s1 - m_new).astype(BF16)
                m_ref[a, rows, :] = m_new
            k0 = pl.multiple_of(kchunk * qb, qb)
            pv = _dot(p_ref[n], vaug_ref[branch, pl.ds(k0, qb), :])
            alpha = alpha_ref[n]
            acc_ref[a, :, 0:dh] = acc_ref[a, :, 0:dh] * alpha + pv[:, 0:dh]
            acc_ref[a, :, dh:] = acc_ref[a, :, dh:] * alpha + pv[:, dh:]

    n_far = jnp.maximum(qi - 1, 0)

    def pair_body(i, carry):
        flash_steps([(0, 0, 2 * i, None), (0, 0, 2 * i + 1, None)])
        return carry

    lax.fori_loop(0, n_far // 2, pair_body, 0)

    @pl.when(n_far % 2 == 1)
    def _():
        flash_steps([(0, 0, n_far - 1, None)])

    def near(kchunk, tile):
        return jnp.maximum(kchunk, 0), jnp.where(kchunk >= 0, tile, 3)

    k_s1, t_s1 = near(qi - 1, 1)
    k_w2, t_w2 = near(qi - 2, 2)
    flash_steps([(0, 0, k_s1, t_s1), (0, 0, qi, 0),
                 (1, 1, k_w2, t_w2), (1, 1, k_s1, t_s1), (1, 1, qi, 0)])

    gates = _sigmoid(gt_ref[...])
    for r in range(grp):
        rows = slice(r * qb, (r + 1) * qb)
        o_sel = acc_ref[0, rows, 0:dh] / acc_ref[0, rows, dh:]
        o_win = acc_ref[1, rows, 0:dh] / acc_ref[1, rows, dh:]
        o = (gates[:, 3 * r:3 * r + 1] * o_cmp[r] + gates[:, 3 * r + 1:3 * r + 2] * o_sel
             + gates[:, 3 * r + 2:3 * r + 3] * o_win)
        o_ref[:, r * dh:(r + 1) * dh] = o.astype(o_ref.dtype)


def nsa_attention(proj, aux, kvc, rel_bias, *, batch, seq, n_groups):
    qb = NSA_QB
    dh = HEAD_DIM
    grp = NSA_GROUP
    nq = seq // qb
    nc = seq // CMP_STRIDE
    n_slc = seq // SLC_BLOCK
    n_sel = min(SLC_TOPK, n_slc)
    assert n_slc <= dh - 2
    nrows = grp * qb
    dtab, tt = nsa_bias_tables(rel_bias, seq)
    smt = jnp.asarray(_sel_map_t(seq), BF16)
    c31 = rel_bias[NUM_BUCKETS - 1].astype(F32) * LOG2E
    c31_hi = c31.astype(BF16).astype(F32)
    cst = jnp.stack([c31_hi, c31 - c31_hi])
    kv_base = n_groups * grp
    kv_map = lambda j: (lambda b, g, i, c: (b, kv_base + j * n_groups + g))
    grid_spec = pltpu.PrefetchScalarGridSpec(
        num_scalar_prefetch=1,
        grid=(batch, n_groups, nq),
        in_specs=[
            pl.BlockSpec((qb, grp * dh), lambda b, g, i, c: (b * nq + i, g)),
            pl.BlockSpec((seq, dh), kv_map(0)),
            pl.BlockSpec((seq, dh), kv_map(1)),
            pl.BlockSpec((seq, dh), kv_map(2)),
            pl.BlockSpec((seq, dh), kv_map(3)),
            pl.BlockSpec((None, None, None, nc, dh), lambda b, g, i, c: (0, b, g, 0, 0)),
            pl.BlockSpec((None, None, None, nc, dh), lambda b, g, i, c: (1, b, g, 0, 0)),
            pl.BlockSpec((qb, 128), lambda b, g, i, c: (b * nq + i, 2 * n_groups + g)),
            pl.BlockSpec((grp, 4, qb, qb), lambda b, g, i, c: (g, 0, 0, 0)),
            pl.BlockSpec((grp, tt.shape[1], qb), lambda b, g, i, c: (g, 0, 0)),
            pl.BlockSpec((n_slc, nc), lambda b, g, i, c: (0, 0)),
        ],
        out_specs=pl.BlockSpec((qb, grp * dh), lambda b, g, i, c: (b * nq + i, g)),
        scratch_shapes=[
            pltpu.VMEM((2, seq, 2 * dh), BF16),
            pltpu.VMEM((2, seq, 2 * dh), BF16),
            pltpu.VMEM((nrows, 2 * dh), BF16),
            pltpu.VMEM((dh, nc), BF16),
            pltpu.VMEM((NSA_STEPS, nrows, qb), F32),
            pltpu.VMEM((NSA_STEPS, nrows, qb), BF16),
            pltpu.VMEM((2, nrows, dh), F32),
            pltpu.VMEM((NSA_STEPS, nrows, dh), F32),
            pltpu.VMEM((2, nrows, 2 * dh), F32),
        ],
    )
    return pl.pallas_call(
        functools.partial(_nsa_attn_kernel, n_sel=n_sel),
        grid_spec=grid_spec,
        out_shape=jax.ShapeDtypeStruct((batch * seq, n_groups * grp * dh), BF16),
        compiler_params=_cparams(("parallel", "parallel", "arbitrary")),
        name="nsa_attention",
    )(cst, proj, proj, proj, proj, proj, kvc, kvc, aux, dtab, tt, smt)


def _gdn_in_weights(w_in, n_heads):
    hp = GDN_HEADS_PER_STEP
    wd = n_heads * HEAD_DIM
    ng = n_heads // hp
    main = jnp.concatenate([w_in[:, :4 * wd], w_in[:, 4 * wd + 2 * n_heads:]], axis=1)
    wb = w_in[:, 4 * wd:4 * wd + n_heads].reshape(-1, ng, hp)
    wa = w_in[:, 4 * wd + n_heads:4 * wd + 2 * n_heads].reshape(-1, ng, hp)
    gate = jnp.concatenate([wb, wa, jnp.zeros((w_in.shape[0], ng, 128 - 2 * hp), w_in.dtype)], axis=2)
    return main.astype(BF16), gate.reshape(w_in.shape[0], ng * 128).astype(BF16)


def _nsa_in_weights(w_in, n_heads):
    grp = NSA_GROUP
    ng = n_heads // grp
    qw = n_heads * HEAD_DIM
    kvw = ng * HEAD_DIM
    main = jnp.concatenate([w_in[:, :qw] * (HEAD_DIM ** -0.5 * LOG2E), w_in[:, qw + 2 * kvw:qw + 6 * kvw],
                            w_in[:, qw + 6 * kvw + 3 * n_heads:]], axis=1)
    cmp_w = w_in[:, qw:qw + 2 * kvw]
    wg = w_in[:, qw + 6 * kvw:qw + 6 * kvw + 3 * n_heads].reshape(-1, ng, 3 * grp)
    gate = jnp.concatenate([wg, jnp.zeros((w_in.shape[0], ng, 128 - 3 * grp), w_in.dtype)], axis=2)
    aux = jnp.concatenate([cmp_w, gate.reshape(w_in.shape[0], ng * 128)], axis=1)
    return main.astype(BF16), aux.astype(BF16)


def _pick(n, candidates):
    for c in candidates:
        if n % c == 0:
            return c
    return n


def kernel(x, mem, rel_bias, norm_mix_w, norm_ffn_w, final_norm_w, mem_norm_w, mem_w_kv, w_out, gdn_w_in, gdn_conv_w, gdn_a_log, gdn_dt_bias, gdn_norm_w, nsa_w_in, nsa_cmp_pos_k, nsa_cmp_w1_k, nsa_cmp_w2_k, nsa_cmp_pos_v, nsa_cmp_w1_v, nsa_cmp_w2_v, ffn_w_up, ffn_conv_w, ffn_conv_b, ffn_w_down):
    batch, seq, d_model = x.shape
    depth = norm_mix_w.shape[0]
    n_heads = d_model // HEAD_DIM
    m_tok = mem.shape[1]
    mw = MEM_HEADS * HEAD_DIM
    d_ff = ffn_w_down.shape[1]
    t = batch * seq
    xf = x.reshape(t, d_model)
    memf = mem.reshape(batch * m_tok, d_model)
    tm = _pick(seq, (1024, 512, 256, 128))
    tm_small = _pick(seq, (512, 256, 128))

    for i in range(depth):
        j = i // 2
        kvw = mem_w_kv[i].astype(BF16)
        kv, _ = norm_matmul(memf, mem_norm_w[i], kvw, kvw[:, :128],
                            tm=_pick(batch * m_tok, (512, 256)), tn=_pick(kvw.shape[1], (512, 256)))
        if i % 2 == 0:
            w_main, w_aux = _gdn_in_weights(gdn_w_in[j], n_heads)
            proj, aux = norm_matmul(xf, norm_mix_w[i], w_main, w_aux, tm=tm_small,
                                    tn=_pick(w_main.shape[1], (512, 256)))
            mix = gdn_mixer(proj, aux, gdn_conv_w[j], gdn_a_log[j], gdn_dt_bias[j], gdn_norm_w[j],
                            batch=batch, seq=seq, lblk=_pick(seq, (512, 256, 128, 64)))
            qm_block = 4 * n_heads * HEAD_DIM // mw
        else:
            ng = n_heads // NSA_GROUP
            w_main, w_aux = _nsa_in_weights(nsa_w_in[j], n_heads)
            proj, aux = norm_matmul(xf, norm_mix_w[i], w_main, w_aux, tm=tm_small,
                                    tn=_pick(w_main.shape[1], (512, 256)))
            pos = jnp.stack([nsa_cmp_pos_k[j], nsa_cmp_pos_v[j]]).astype(F32)
            w1 = jnp.stack([nsa_cmp_w1_k[j], nsa_cmp_w1_v[j]]).astype(BF16)
            w2 = jnp.stack([nsa_cmp_w2_k[j], nsa_cmp_w2_v[j]]).astype(BF16)
            kvc = nsa_compress(aux, pos, w1, w2, batch=batch, seq=seq, n_groups=ng)
            mix = nsa_attention(proj, aux, kvc, rel_bias, batch=batch, seq=seq, n_groups=ng)
            qm_block = (n_heads * HEAD_DIM + 4 * ng * HEAD_DIM) // mw
        mo = mem_attention(proj, kv, batch=batch, seq=seq, q_col_block=qm_block, ts=tm)
        wo = w_out[i].astype(BF16)
        xf = out_proj(xf, mix, mo, wo[:n_heads * HEAD_DIM], wo[n_heads * HEAD_DIM:], tm=tm_small)
        wu = ffn_w_up[i].astype(BF16)
        act = ffn_up(xf, norm_ffn_w[i], wu[:, :d_ff], wu[:, d_ff:], ffn_conv_w[i], ffn_conv_b[i],
                     seq=seq, tm=tm_small, tn=_pick(d_ff, (256, 128)))
        xf = ffn_down(xf, act, ffn_w_down[i].astype(BF16), final_norm_w, tm=tm_small,
                      final_norm=(i == depth - 1))
    return xf.reshape(batch, seq, d_model)
```

```python
import functools
import math

import jax
import jax.numpy as jnp
import numpy as np
from jax import lax
from jax.experimental import pallas as pl
from jax.experimental.pallas import tpu as pltpu

F32 = jnp.float32
BF16 = jnp.bfloat16

HEAD_DIM = 128
GDN_CONV = 4
GDN_CHUNK = 64
GDN_HEADS_PER_STEP = 4
GDN_SUPER = 256
NSA_GROUP = 4
CMP_LEN = 32
CMP_STRIDE = 16
SLC_BLOCK = 64
SLC_TOPK = 16
WINDOW = 512
NSA_QB = 256
NSA_ROWBLK = 128
NSA_STEPS = 5
MEM_HEADS = 4
NUM_BUCKETS = 32
MAX_DISTANCE = 128
FFN_CONV = 3
EPS = 1e-6
LOG2E = math.log2(math.e)
NEG = -1e30
SEL_PENALTY = 32768.0
VMEM_LIMIT = 56 * 1024 * 1024


def _cparams(sem):
    return pltpu.CompilerParams(dimension_semantics=sem, vmem_limit_bytes=VMEM_LIMIT)


def _dot(a, b):
    return jnp.dot(a, b, preferred_element_type=F32)


def _dot_nt(a, b):
    return lax.dot_general(a, b, (((1,), (1,)), ((), ())), preferred_element_type=F32)


def _silu(x):
    return x * (1.0 / (1.0 + jnp.exp(-x)))


def _sigmoid(x):
    return 1.0 / (1.0 + jnp.exp(-x))


def _softplus(x):
    return jnp.maximum(x, 0.0) + jnp.log(1.0 + jnp.exp(-jnp.abs(x)))


def _split3(x):
    h = x.astype(BF16)
    r = x - h.astype(F32)
    m = r.astype(BF16)
    l = (r - m.astype(F32)).astype(BF16)
    return h, m, l


def _norm_mm_kernel(x_ref, nw_ref, w_ref, waux_ref, o_ref, oaux_ref, *, rs, tn):
    for r0 in range(0, x_ref.shape[0], rs):
        rows = slice(r0, r0 + rs)
        x = x_ref[rows, :]
        ms = jnp.mean(x * x, axis=-1, keepdims=True)
        xn = (x * lax.rsqrt(ms + EPS) * nw_ref[...]).astype(BF16)
        oaux_ref[rows, :] = _dot(xn, waux_ref[...])
        for c0 in range(0, o_ref.shape[1], tn):
            o_ref[rows, c0:c0 + tn] = _dot(xn, w_ref[:, c0:c0 + tn]).astype(o_ref.dtype)


def norm_matmul(x, nw, w, waux, *, tm, tn):
    t, d = x.shape
    n = w.shape[1]
    na = waux.shape[1]
    whole = lambda i: (0, 0)
    return pl.pallas_call(
        functools.partial(_norm_mm_kernel, rs=min(tm, 256), tn=tn),
        grid=(t // tm,),
        in_specs=[
            pl.BlockSpec((tm, d), lambda i: (i, 0)),
            pl.BlockSpec((1, d), whole),
            pl.BlockSpec((d, n), whole),
            pl.BlockSpec((d, na), whole),
        ],
        out_specs=[
            pl.BlockSpec((tm, n), lambda i: (i, 0)),
            pl.BlockSpec((tm, na), lambda i: (i, 0)),
        ],
        out_shape=[jax.ShapeDtypeStruct((t, n), BF16), jax.ShapeDtypeStruct((t, na), F32)],
        compiler_params=_cparams(("parallel",)),
        name="norm_matmul",
    )(x, nw.reshape(1, d), w, waux)


def _mem_attn_kernel(q_ref, kv_ref, o_ref):
    scale = HEAD_DIM ** -0.5
    mw = MEM_HEADS * HEAD_DIM
    for h in range(MEM_HEADS):
        lo = h * HEAD_DIM
        q = q_ref[:, lo:lo + HEAD_DIM]
        k = kv_ref[:, lo:lo + HEAD_DIM]
        v = kv_ref[:, mw + lo:mw + lo + HEAD_DIM]
        s = _dot_nt(q, k) * scale
        m = jnp.max(s, axis=-1, keepdims=True)
        p = jnp.exp(s - m)
        l = jnp.sum(p, axis=-1, keepdims=True)
        o = _dot(p.astype(BF16), v) / l
        o_ref[:, lo:lo + HEAD_DIM] = o.astype(o_ref.dtype)


def mem_attention(proj, kv, *, batch, seq, q_col_block, ts):
    mw = MEM_HEADS * HEAD_DIM
    m_tok = kv.shape[0] // batch
    nt = seq // ts
    return pl.pallas_call(
        _mem_attn_kernel,
        grid=(batch, nt),
        in_specs=[
            pl.BlockSpec((ts, mw), lambda b, i: (b * nt + i, q_col_block)),
            pl.BlockSpec((m_tok, 2 * mw), lambda b, i: (b, 0)),
        ],
        out_specs=pl.BlockSpec((ts, mw), lambda b, i: (b * nt + i, 0)),
        out_shape=jax.ShapeDtypeStruct((batch * seq, mw), BF16),
        compiler_params=_cparams(("parallel", "parallel")),
        name="mem_attention",
    )(proj, kv)


def _out_proj_kernel(x_ref, a_ref, b_ref, wa_ref, wb_ref, o_ref):
    o_ref[...] = x_ref[...] + _dot(a_ref[...], wa_ref[...]) + _dot(b_ref[...], wb_ref[...])


def out_proj(x, a, b, wa, wb, *, tm):
    t, d = x.shape
    ka, kb = a.shape[1], b.shape[1]
    return pl.pallas_call(
        _out_proj_kernel,
        grid=(t // tm,),
        in_specs=[
            pl.BlockSpec((tm, d), lambda i: (i, 0)),
            pl.BlockSpec((tm, ka), lambda i: (i, 0)),
            pl.BlockSpec((tm, kb), lambda i: (i, 0)),
            pl.BlockSpec((ka, d), lambda i: (0, 0)),
            pl.BlockSpec((kb, d), lambda i: (0, 0)),
        ],
        out_specs=pl.BlockSpec((tm, d), lambda i: (i, 0)),
        out_shape=jax.ShapeDtypeStruct((t, d), F32),
        compiler_params=_cparams(("parallel",)),
        name="out_proj",
    )(x, a, b, wa, wb)


def _ffn_up_kernel(x_ref, nw_ref, wg_ref, wv_ref, cw_ref, cb_ref, o_ref, gs_ref,
                   *, tiles_per_seq, tn, rs):
    tm = x_ref.shape[0]
    dff = o_ref.shape[1]

    @pl.when(pl.program_id(0) % tiles_per_seq == 0)
    def _():
        gs_ref[0:8, :] = jnp.zeros((8, dff), F32)

    for r0 in range(0, tm, rs):
        rows = slice(r0, r0 + rs)
        x = x_ref[rows, :]
        ms = jnp.mean(x * x, axis=-1, keepdims=True)
        xn = (x * lax.rsqrt(ms + EPS) * nw_ref[...]).astype(BF16)
        for j in range(dff // tn):
            cols = slice(j * tn, (j + 1) * tn)
            g = _dot(xn, wg_ref[:, cols])
            v = _dot(xn, wv_ref[:, cols])
            gs_ref[8 + r0:8 + r0 + rs, cols] = g
            conv = (gs_ref[6 + r0:6 + r0 + rs, cols] * cw_ref[0:1, cols]
                    + gs_ref[7 + r0:7 + r0 + rs, cols] * cw_ref[1:2, cols]
                    + g * cw_ref[2:3, cols] + cb_ref[:, cols])
            o_ref[rows, cols] = (_silu(conv) * v).astype(o_ref.dtype)
    gs_ref[0:8, :] = gs_ref[tm:tm + 8, :]


def ffn_up(x, nw, wg, wv, cw, cb, *, seq, tm, tn):
    t, d = x.shape
    dff = wg.shape[1]
    kern = functools.partial(_ffn_up_kernel, tiles_per_seq=seq // tm, tn=tn, rs=min(tm, 256))
    whole = lambda i: (0, 0)
    return pl.pallas_call(
        kern,
        grid=(t // tm,),
        in_specs=[
            pl.BlockSpec((tm, d), lambda i: (i, 0)),
            pl.BlockSpec((1, d), whole),
            pl.BlockSpec((d, dff), whole),
            pl.BlockSpec((d, dff), whole),
            pl.BlockSpec((8, dff), whole),
            pl.BlockSpec((1, dff), whole),
        ],
        out_specs=pl.BlockSpec((tm, dff), lambda i: (i, 0)),
        out_shape=jax.ShapeDtypeStruct((t, dff), BF16),
        scratch_shapes=[
            pltpu.VMEM((tm + 8, dff), F32),
        ],
        compiler_params=_cparams(("arbitrary",)),
        name="ffn_up",
    )(x, nw.reshape(1, d), wg, wv, jnp.pad(cw, ((0, 8 - cw.shape[0]), (0, 0))), cb.reshape(1, dff))


def _ffn_down_kernel(x_ref, a_ref, w_ref, fw_ref, o_ref, *, final_norm):
    y = x_ref[...] + _dot(a_ref[...], w_ref[...])
    if final_norm:
        ms = jnp.mean(y * y, axis=-1, keepdims=True)
        y = y * lax.rsqrt(ms + EPS) * fw_ref[...]
    o_ref[...] = y


def ffn_down(x, a, w, fw, *, tm, final_norm):
    t, d = x.shape
    k = a.shape[1]
    return pl.pallas_call(
        functools.partial(_ffn_down_kernel, final_norm=final_norm),
        grid=(t // tm,),
        in_specs=[
            pl.BlockSpec((tm, d), lambda i: (i, 0)),
            pl.BlockSpec((tm, k), lambda i: (i, 0)),
            pl.BlockSpec((k, d), lambda i: (0, 0)),
            pl.BlockSpec((1, d), lambda i: (0, 0)),
        ],
        out_specs=pl.BlockSpec((tm, d), lambda i: (i, 0)),
        out_shape=jax.ShapeDtypeStruct((t, d), F32),
        compiler_params=_cparams(("parallel",)),
        name="ffn_down",
    )(x, a, w, fw.reshape(1, d))


def _gdn_kernel(q_ref, k_ref, v_ref, z_ref, gt_ref, cw_ref, arow_ref, dtrow_ref, nw_ref, o_ref,
                qs_ref, ks_ref, vs_ref, state_ref, *, n_chunks):
    hp = GDN_HEADS_PER_STEP
    dh = HEAD_DIM
    c = GDN_CHUNK
    lblk = q_ref.shape[0]
    sb = pl.program_id(2)

    @pl.when(sb == 0)
    def _():
        zero8 = jnp.zeros((8, hp * dh), F32)
        qs_ref[0:8, :] = zero8
        ks_ref[0:8, :] = zero8
        vs_ref[0:8, :] = zero8
        state_ref[...] = jnp.zeros(state_ref.shape, F32)

    for idx, (src, dst) in enumerate(((q_ref, qs_ref), (k_ref, ks_ref), (v_ref, vs_ref))):
        dst[8:, :] = src[...].astype(F32)
        w = cw_ref[idx]
        y = (dst[5:5 + lblk, :] * w[0:1, :] + dst[6:6 + lblk, :] * w[1:2, :]
             + dst[7:7 + lblk, :] * w[2:3, :] + dst[8:8 + lblk, :] * w[3:4, :])
        tail = dst[lblk:lblk + 8, :]
        dst[8:, :] = _silu(y)
        dst[0:8, :] = tail

    sl = GDN_SUPER
    cps = sl // c
    row = lax.broadcasted_iota(jnp.int32, (sl, sl), 0)
    col = lax.broadcasted_iota(jnp.int32, (sl, sl), 1)
    same = (row // c) == (col // c)
    incl = same & (row >= col)
    strict = same & (row > col)
    tril = jnp.where(incl, 1.0, 0.0).astype(BF16)
    ones_blk = jnp.where(same, 1.0, 0.0).astype(BF16)
    arow = arow_ref[0]
    dtrow = dtrow_ref[0]
    nw = nw_ref[...]
    states = [state_ref[r] for r in range(hp)]
    zeros_c = jnp.zeros((c, dh), BF16)

    for sci in range(lblk // sl):
        r0 = sci * sl
        gt = gt_ref[r0:r0 + sl, :]
        beta_all = _sigmoid(gt)
        g_all = -arow * _softplus(gt + dtrow)
        gh, gm, gl = _split3(g_all)
        gc_all = _dot(tril, gh) + _dot(tril, gm) + _dot(tril, gl)
        gend_all = _dot(ones_blk, gh) + _dot(ones_blk, gm) + _dot(ones_blk, gl)
        gc_t = gc_all.T
        pre = []
        for r in range(hp):
            lo = r * dh
            qh = qs_ref[r0 + 8:r0 + 8 + sl, lo:lo + dh]
            kh = ks_ref[r0 + 8:r0 + 8 + sl, lo:lo + dh]
            vh = vs_ref[r0 + 8:r0 + 8 + sl, lo:lo + dh]
            qn = qh * lax.rsqrt(jnp.sum(qh * qh, axis=-1, keepdims=True) + EPS) * (dh ** -0.5)
            kn = kh * lax.rsqrt(jnp.sum(kh * kh, axis=-1, keepdims=True) + EPS)
            beta = beta_all[:, r:r + 1]
            gc_col = gc_all[:, hp + r:hp + r + 1]
            gc_row = gc_t[hp + r:hp + r + 1, :]
            gend = gend_all[:, hp + r:hp + r + 1]
            decay = jnp.where(incl, jnp.exp(jnp.where(incl, gc_col - gc_row, 0.0)), 0.0)
            eg = jnp.exp(gc_col)
            kb = kn * beta
            kn_b = kn.astype(BF16)
            lmat = jnp.where(strict, _dot_nt(kb.astype(BF16), kn_b) * decay, 0.0)
            a_intra = jnp.where(incl, _dot_nt(qn.astype(BF16), kn_b) * decay, 0.0).astype(BF16)
            x = jnp.concatenate([vh * beta, kb * eg], axis=1)
            mp = -lmat
            for it in range(6):
                mp_b = mp.astype(BF16)
                x = x + _dot(mp_b, x.astype(BF16))
                if it < 5:
                    mp = _dot(mp_b, mp_b)
            pre.append((x[:, :dh], x[:, dh:], qn * eg, kn * jnp.exp(gend - gc_col), a_intra,
                        jnp.exp(gend)))

        for ci in range(cps):
            rows = slice(ci * c, (ci + 1) * c)
            for r in range(hp):
                lo = r * dh
                u, w, qd, kd, a_intra, egend = pre[r]
                state = states[r]
                wq = jnp.concatenate([w[rows], qd[rows]], axis=0).astype(BF16)
                wqs = _dot(wq, state.astype(BF16))
                v_new = u[rows] - wqs[:c]
                v_new_b = v_new.astype(BF16)
                vcat = jnp.concatenate([zeros_c] * ci + [v_new_b] + [zeros_c] * (cps - 1 - ci), axis=0)
                o = wqs[c:] + _dot(a_intra[rows], vcat)
                states[r] = (state * egend[ci * c:ci * c + 1, :]
                             + _dot(kd[rows].T.astype(BF16), v_new_b))
                ms = jnp.mean(o * o, axis=-1, keepdims=True)
                zz = z_ref[r0 + ci * c:r0 + (ci + 1) * c, lo:lo + dh].astype(F32)
                o = o * lax.rsqrt(ms + EPS) * nw * _silu(zz)
                o_ref[r0 + ci * c:r0 + (ci + 1) * c, lo:lo + dh] = o.astype(o_ref.dtype)

    for r in range(hp):
        state_ref[r] = states[r]


def gdn_mixer(proj, gates, conv_w, a_log, dt_bias, norm_w, *, batch, seq, lblk):
    hp = GDN_HEADS_PER_STEP
    dh = HEAD_DIM
    n_heads = a_log.shape[0]
    ng = n_heads // hp
    wd = n_heads * dh
    bw = hp * dh
    nsb = seq // lblk
    arow = jnp.zeros((ng, 1, 128), F32).at[:, 0, hp:2 * hp].set(jnp.exp(a_log.astype(F32)).reshape(ng, hp))
    dtrow = jnp.zeros((ng, 1, 128), F32).at[:, 0, hp:2 * hp].set(dt_bias.astype(F32).reshape(ng, hp))
    cw = conv_w.reshape(GDN_CONV, 3, ng, bw).transpose(1, 2, 0, 3)
    cw = jnp.pad(cw, ((0, 0), (0, 0), (0, 8 - GDN_CONV), (0, 0)))
    kern = functools.partial(_gdn_kernel, n_chunks=lblk // GDN_CHUNK)
    row_map = lambda off: (lambda b, g, s: (b * nsb + s, off + g))
    return pl.pallas_call(
        kern,
        grid=(batch, ng, nsb),
        in_specs=[
            pl.BlockSpec((lblk, bw), row_map(0)),
            pl.BlockSpec((lblk, bw), row_map(ng)),
            pl.BlockSpec((lblk, bw), row_map(2 * ng)),
            pl.BlockSpec((lblk, bw), row_map(3 * ng)),
            pl.BlockSpec((lblk, 128), row_map(0)),
            pl.BlockSpec((3, None, 8, bw), lambda b, g, s: (0, g, 0, 0)),
            pl.BlockSpec((None, 1, 128), lambda b, g, s: (g, 0, 0)),
            pl.BlockSpec((None, 1, 128), lambda b, g, s: (g, 0, 0)),
            pl.BlockSpec((1, dh), lambda b, g, s: (0, 0)),
        ],
        out_specs=pl.BlockSpec((lblk, bw), row_map(0)),
        out_shape=jax.ShapeDtypeStruct((batch * seq, wd), BF16),
        scratch_shapes=[
            pltpu.VMEM((lblk + 8, bw), F32),
            pltpu.VMEM((lblk + 8, bw), F32),
            pltpu.VMEM((lblk + 8, bw), F32),
            pltpu.VMEM((hp, dh, dh), F32),
        ],
        compiler_params=_cparams(("parallel", "parallel", "arbitrary")),
        name="gdn_mixer",
    )(proj, proj, proj, proj, gates, cw, arow, dtrow, norm_w.reshape(1, dh).astype(F32))


def _nsa_compress_kernel(t_ref, pos_ref, w1_ref, w2_ref, o_ref):
    nc = t_ref.shape[0] // CMP_STRIDE
    dh = HEAD_DIM
    acc_a = jnp.zeros((nc, dh), F32)
    acc_b = jnp.zeros((nc, dh), F32)
    for i in range(CMP_STRIDE):
        xi = t_ref[pl.ds(i, nc, stride=CMP_STRIDE), :]
        xa = (xi + pos_ref[i:i + 1, :]).astype(BF16)
        xb = (xi + pos_ref[CMP_STRIDE + i:CMP_STRIDE + i + 1, :]).astype(BF16)
        acc_a = acc_a + _dot(xa, w1_ref[i * dh:(i + 1) * dh, :])
        acc_b = acc_b + _dot(xb, w1_ref[(CMP_STRIDE + i) * dh:(CMP_STRIDE + i + 1) * dh, :])
    h = acc_a + jnp.concatenate([acc_b[1:, :], acc_b[:1, :]], axis=0)
    o_ref[...] = _dot(_silu(h).astype(BF16), w2_ref[...]).astype(o_ref.dtype)


def nsa_compress(aux, pos, w1, w2, *, batch, seq, n_groups):
    dh = HEAD_DIM
    nc = seq // CMP_STRIDE
    return pl.pallas_call(
        _nsa_compress_kernel,
        grid=(batch, 2, n_groups),
        in_specs=[
            pl.BlockSpec((seq, dh), lambda b, j, g: (b, j * n_groups + g)),
            pl.BlockSpec((None, CMP_LEN, dh), lambda b, j, g: (j, 0, 0)),
            pl.BlockSpec((None, CMP_LEN * dh, dh), lambda b, j, g: (j, 0, 0)),
            pl.BlockSpec((None, dh, dh), lambda b, j, g: (j, 0, 0)),
        ],
        out_specs=pl.BlockSpec((None, None, None, nc, dh), lambda b, j, g: (j, b, g, 0, 0)),
        out_shape=jax.ShapeDtypeStruct((2, batch, n_groups, nc, dh), BF16),
        compiler_params=_cparams(("parallel", "parallel", "parallel")),
        name="nsa_compress",
    )(aux, pos, w1, w2)


def _t5_bucket_np(n):
    max_exact = NUM_BUCKETS // 2
    nf = np.maximum(n, 1).astype(np.float32)
    logv = np.log(nf / np.float32(max_exact)) / np.float32(math.log(MAX_DISTANCE / max_exact))
    large = max_exact + (logv * np.float32(NUM_BUCKETS - max_exact)).astype(np.int32)
    large = np.minimum(large, NUM_BUCKETS - 1)
    return np.where(n < max_exact, n, large).astype(np.int32)


def _t5_thresholds():
    n = np.arange(0, MAX_DISTANCE + 1, dtype=np.int32)
    b = _t5_bucket_np(n)
    half = NUM_BUCKETS // 2
    return tuple(int(np.min(n[b >= half + k])) for k in range(1, NUM_BUCKETS - half))


def _nsa_bias_kernel(tab_ref, dtab_ref, tt_ref, *, thresholds, nc):
    h = pl.program_id(0)
    qb = NSA_QB
    half = NUM_BUCKETS // 2
    c31 = tab_ref[NUM_BUCKETS - 1, h]

    def lookup(dist):
        n = jnp.maximum(dist, 0)
        big = jnp.full(n.shape, half, jnp.int32)
        for t in thresholds:
            big = big + jnp.where(n >= t, 1, 0)
        bucket = jnp.where(n < half, n, big)
        val = jnp.zeros(n.shape, F32)
        for b in range(NUM_BUCKETS):
            val = jnp.where(bucket == b, tab_ref[b, h], val)
        return val

    q = lax.broadcasted_iota(jnp.int32, (qb, qb), 0)
    kk = lax.broadcasted_iota(jnp.int32, (qb, qb), 1)
    dtab_ref[0] = jnp.where(q >= kk, (lookup(q - kk) - c31) * LOG2E, NEG)
    dtab_ref[1] = (lookup(q - kk + qb) - c31) * LOG2E
    dtab_ref[2] = jnp.where(kk > q, 0.0, NEG)
    dtab_ref[3] = jnp.full((qb, qb), NEG, F32)
    nw = 2 * qb // CMP_STRIDE
    x = lax.broadcasted_iota(jnp.int32, (nw, qb), 0)
    ql = lax.broadcasted_iota(jnp.int32, (nw, qb), 1)
    dist = ql - CMP_STRIDE * x + (qb - CMP_LEN + 1)
    tt_ref[0:nc, :] = jnp.zeros((nc, qb), F32) + c31 * LOG2E
    tt_ref[nc:nc + nw, :] = jnp.where(dist >= 0, lookup(dist) * LOG2E, NEG)
    tt_ref[nc + nw:, :] = jnp.full((nc, qb), NEG, F32)


def nsa_bias_tables(rel_bias, seq):
    qb = NSA_QB
    nc = seq // CMP_STRIDE
    nw = 2 * qb // CMP_STRIDE
    n_heads = rel_bias.shape[1]
    kern = functools.partial(_nsa_bias_kernel, thresholds=_t5_thresholds(), nc=nc)
    return pl.pallas_call(
        kern,
        grid=(n_heads,),
        in_specs=[pl.BlockSpec(memory_space=pltpu.SMEM)],
        out_specs=[
            pl.BlockSpec((None, 4, qb, qb), lambda h: (h, 0, 0, 0)),
            pl.BlockSpec((None, 2 * nc + nw, qb), lambda h: (h, 0, 0)),
        ],
        out_shape=[jax.ShapeDtypeStruct((n_heads, 4, qb, qb), F32),
                   jax.ShapeDtypeStruct((n_heads, 2 * nc + nw, qb), F32)],
        compiler_params=_cparams(("parallel",)),
        name="nsa_bias_tables",
    )(rel_bias.astype(F32))


def _sel_map_t(seq):
    nc = seq // CMP_STRIDE
    n_slc = seq // SLC_BLOCK
    c_start = np.arange(nc, dtype=np.int32) * CMP_STRIDE
    c_end = c_start + CMP_LEN - 1
    s_start = np.arange(n_slc, dtype=np.int32) * SLC_BLOCK
    m = (c_start[None, :] < s_start[:, None] + SLC_BLOCK) & (s_start[:, None] <= c_end[None, :])
    return m.astype(np.float32)


def _nsa_attn_kernel(cst_ref, q_ref, ksl_ref, vsl_ref, kw_ref, vw_ref, kc_ref, vc_ref, gt_ref,
                     dtab_ref, tt_ref, smt_ref, o_ref,
                     kaug_ref, vaug_ref, qaug_ref, vct_ref, p_ref, m_ref, alpha_ref,
                     acc_ref, *, n_sel):
    qb = NSA_QB
    dh = HEAD_DIM
    grp = NSA_GROUP
    nrows = grp * qb
    rblk = NSA_ROWBLK
    g = pl.program_id(1)
    qi = pl.program_id(2)
    seq = ksl_ref.shape[0]
    nc = kc_ref.shape[0]
    n_slc = seq // SLC_BLOCK
    blocks_per_chunk = qb // SLC_BLOCK

    @pl.when(qi == 0)
    def _():
        lane = lax.broadcasted_iota(jnp.int32, (seq, dh), 1)
        rb = lax.broadcasted_iota(jnp.int32, (seq, dh), 0) // SLC_BLOCK
        ones_tail = jnp.where(lane >= dh - 2, 1.0, 0.0)
        kaug_ref[0, :, 0:dh] = ksl_ref[...]
        kaug_ref[0, :, dh:] = (jnp.where(rb == lane, -SEL_PENALTY, 0.0) + ones_tail).astype(BF16)
        kaug_ref[1, :, 0:dh] = kw_ref[...]
        kaug_ref[1, :, dh:] = ones_tail.astype(BF16)
        ones = jnp.ones((seq, dh), BF16)
        vaug_ref[0, :, 0:dh] = vsl_ref[...]
        vaug_ref[0, :, dh:] = ones
        vaug_ref[1, :, 0:dh] = vw_ref[...]
        vaug_ref[1, :, dh:] = ones
        vct_ref[...] = vc_ref[...].astype(F32).T.astype(BF16)

    for r in range(grp):
        qaug_ref[r * qb:(r + 1) * qb, 0:dh] = q_ref[:, r * dh:(r + 1) * dh]
    q4 = qaug_ref[:, 0:dh]

    st = _dot_nt(kc_ref[...], q4)
    start = pl.multiple_of(nc + CMP_STRIDE - (qb // CMP_STRIDE) * qi, CMP_STRIDE)
    psum = jnp.zeros((nc, qb), F32)
    o_cmp = []
    for r in range(grp):
        bias = tt_ref[r, pl.ds(start, nc), :]
        s = st[:, r * qb:(r + 1) * qb] + bias
        valid = bias > 0.5 * NEG
        m = jnp.max(s, axis=0, keepdims=True)
        p = jnp.where(valid, jnp.exp2(s - m), 0.0)
        p = p / jnp.maximum(jnp.sum(p, axis=0, keepdims=True), 1e-30)
        psum = psum + p
        o_cmp.append(_dot(vct_ref[...], p.astype(BF16)).T)

    ph = psum.astype(BF16)
    pl_ = (psum - ph.astype(F32)).astype(BF16)
    smt = smt_ref[...]
    imp = _dot(smt, ph) + _dot(smt, pl_)
    jb = lax.broadcasted_iota(jnp.int32, (n_slc, qb), 0)
    tb = qi * blocks_per_chunk + lax.broadcasted_iota(jnp.int32, (n_slc, qb), 1) // SLC_BLOCK
    forced = (jb == 0) | (jb == tb) | (jb == tb - 1)
    score = jnp.where(jb <= tb, jnp.where(forced, jnp.inf, imp), -jnp.inf)
    jbf = jb.astype(F32)
    notsel = jnp.ones((n_slc, qb), F32)
    for _ in range(n_sel):
        top = jnp.max(score, axis=0, keepdims=True)
        first = jnp.min(jnp.where(score == top, jbf, float(n_slc)), axis=0, keepdims=True)
        pick = (jbf == first) & (top > -jnp.inf)
        notsel = jnp.where(pick, 0.0, notsel)
        score = jnp.where(pick, -jnp.inf, score)
    notsel = jnp.concatenate([notsel, jnp.zeros((dh - n_slc, qb), F32)], axis=0).T
    lane1 = lax.broadcasted_iota(jnp.int32, (1, dh), 1)
    for r in range(grp):
        h = g * grp + r
        cvec = jnp.where(lane1 == dh - 2, cst_ref[0, h], jnp.where(lane1 == dh - 1, cst_ref[1, h], 0.0))
        qaug_ref[r * qb:(r + 1) * qb, dh:] = (notsel + cvec).astype(BF16)

    m_ref[...] = jnp.full(m_ref.shape, NEG, F32)
    acc_ref[...] = jnp.zeros(acc_ref.shape, F32)

    def flash_steps(steps):
        svals = []
        for n, (_, branch, kchunk, _) in enumerate(steps):
            k0 = pl.multiple_of(kchunk * qb, qb)
            svals.append(_dot_nt(qaug_ref[...], kaug_ref[branch, pl.ds(k0, qb), :]))
        for n, (a, branch, kchunk, tile_idx) in enumerate(steps):
            for rbi in range(nrows // rblk):
                rows = slice(rbi * rblk, (rbi + 1) * rblk)
                s0 = svals[n][rows, 0:dh]
                s1 = svals[n][rows, dh:]
                if tile_idx is not None:
                    head = (rbi * rblk) // qb
                    t0 = (rbi * rblk) % qb
                    s0 = s0 + dtab_ref[head, tile_idx, t0:t0 + rblk, 0:dh]
                    s1 = s1 + dtab_ref[head, tile_idx, t0:t0 + rblk, dh:]
                m_old = m_ref[a, rows, :]
                m_new = jnp.maximum(m_old, jnp.max(jnp.maximum(s0, s1), axis=-1, keepdims=True))
                alpha_ref[n, rows, :] = jnp.exp2(m_old - m_new)
                p_ref[n, rows, 0:dh] = jnp.exp2(s0 - m_new).astype(BF16)
                p_ref[n, rows, dh:] = jnp.exp2(s1 - m_new).astype(BF16)
                m_ref[a, rows, :] = m_new
            k0 = pl.multiple_of(kchunk * qb, qb)
            pv = _dot(p_ref[n], vaug_ref[branch, pl.ds(k0, qb), :])
            alpha = alpha_ref[n]
            acc_ref[a, :, 0:dh] = acc_ref[a, :, 0:dh] * alpha + pv[:, 0:dh]
            acc_ref[a, :, dh:] = acc_ref[a, :, dh:] * alpha + pv[:, dh:]

    n_far = jnp.maximum(qi - 1, 0)

    def pair_body(i, carry):
        flash_steps([(0, 0, 2 * i, None), (0, 0, 2 * i + 1, None)])
        return carry

    lax.fori_loop(0, n_far // 2, pair_body, 0)

    @pl.when(n_far % 2 == 1)
    def _():
        flash_steps([(0, 0, n_far - 1, None)])

    def near(kchunk, tile):
        return jnp.maximum(kchunk, 0), jnp.where(kchunk >= 0, tile, 3)

    k_s1, t_s1 = near(qi - 1, 1)
    k_w2, t_w2 = near(qi - 2, 2)
    flash_steps([(0, 0, k_s1, t_s1), (0, 0, qi, 0),
                 (1, 1, k_w2, t_w2), (1, 1, k_s1, t_s1), (1, 1, qi, 0)])

    gates = _sigmoid(gt_ref[...])
    for r in range(grp):
        rows = slice(r * qb, (r + 1) * qb)
        o_sel = acc_ref[0, rows, 0:dh] / acc_ref[0, rows, dh:]
        o_win = acc_ref[1, rows, 0:dh] / acc_ref[1, rows, dh:]
        o = (gates[:, 3 * r:3 * r + 1] * o_cmp[r] + gates[:, 3 * r + 1:3 * r + 2] * o_sel
             + gates[:, 3 * r + 2:3 * r + 3] * o_win)
        o_ref[:, r * dh:(r + 1) * dh] = o.astype(o_ref.dtype)


def nsa_attention(proj, aux, kvc, rel_bias, *, batch, seq, n_groups):
    qb = NSA_QB
    dh = HEAD_DIM
    grp = NSA_GROUP
    nq = seq // qb
    nc = seq // CMP_STRIDE
    n_slc = seq // SLC_BLOCK
    n_sel = min(SLC_TOPK, n_slc)
    assert n_slc <= dh - 2
    nrows = grp * qb
    dtab, tt = nsa_bias_tables(rel_bias, seq)
    smt = jnp.asarray(_sel_map_t(seq), BF16)
    c31 = rel_bias[NUM_BUCKETS - 1].astype(F32) * LOG2E
    c31_hi = c31.astype(BF16).astype(F32)
    cst = jnp.stack([c31_hi, c31 - c31_hi])
    kv_base = n_groups * grp
    kv_map = lambda j: (lambda b, g, i, c: (b, kv_base + j * n_groups + g))
    grid_spec = pltpu.PrefetchScalarGridSpec(
        num_scalar_prefetch=1,
        grid=(batch, n_groups, nq),
        in_specs=[
            pl.BlockSpec((qb, grp * dh), lambda b, g, i, c: (b * nq + i, g)),
            pl.BlockSpec((seq, dh), kv_map(0)),
            pl.BlockSpec((seq, dh), kv_map(1)),
            pl.BlockSpec((seq, dh), kv_map(2)),
            pl.BlockSpec((seq, dh), kv_map(3)),
            pl.BlockSpec((None, None, None, nc, dh), lambda b, g, i, c: (0, b, g, 0, 0)),
            pl.BlockSpec((None, None, None, nc, dh), lambda b, g, i, c: (1, b, g, 0, 0)),
            pl.BlockSpec((qb, 128), lambda b, g, i, c: (b * nq + i, 2 * n_groups + g)),
            pl.BlockSpec((grp, 4, qb, qb), lambda b, g, i, c: (g, 0, 0, 0)),
            pl.BlockSpec((grp, tt.shape[1], qb), lambda b, g, i, c: (g, 0, 0)),
            pl.BlockSpec((n_slc, nc), lambda b, g, i, c: (0, 0)),
        ],
        out_specs=pl.BlockSpec((qb, grp * dh), lambda b, g, i, c: (b * nq + i, g)),
        scratch_shapes=[
            pltpu.VMEM((2, seq, 2 * dh), BF16),
            pltpu.VMEM((2, seq, 2 * dh), BF16),
            pltpu.VMEM((nrows, 2 * dh), BF16),
            pltpu.VMEM((dh, nc), BF16),
            pltpu.VMEM((NSA_STEPS, nrows, qb), BF16),
            pltpu.VMEM((2, nrows, dh), F32),
            pltpu.VMEM((NSA_STEPS, nrows, dh), F32),
            pltpu.VMEM((2, nrows, 2 * dh), F32),
        ],
    )
    return pl.pallas_call(
        functools.partial(_nsa_attn_kernel, n_sel=n_sel),
        grid_spec=grid_spec,
        out_shape=jax.ShapeDtypeStruct((batch * seq, n_groups * grp * dh), BF16),
        compiler_params=_cparams(("parallel", "parallel", "arbitrary")),
        name="nsa_attention",
    )(cst, proj, proj, proj, proj, proj, kvc, kvc, aux, dtab, tt, smt)


def _gdn_in_weights(w_in, n_heads):
    hp = GDN_HEADS_PER_STEP
    wd = n_heads * HEAD_DIM
    ng = n_heads // hp
    main = jnp.concatenate([w_in[:, :4 * wd], w_in[:, 4 * wd + 2 * n_heads:]], axis=1)
    wb = w_in[:, 4 * wd:4 * wd + n_heads].reshape(-1, ng, hp)
    wa = w_in[:, 4 * wd + n_heads:4 * wd + 2 * n_heads].reshape(-1, ng, hp)
    gate = jnp.concatenate([wb, wa, jnp.zeros((w_in.shape[0], ng, 128 - 2 * hp), w_in.dtype)], axis=2)
    return main.astype(BF16), gate.reshape(w_in.shape[0], ng * 128).astype(BF16)


def _nsa_in_weights(w_in, n_heads):
    grp = NSA_GROUP
    ng = n_heads // grp
    qw = n_heads * HEAD_DIM
    kvw = ng * HEAD_DIM
    main = jnp.concatenate([w_in[:, :qw] * (HEAD_DIM ** -0.5 * LOG2E), w_in[:, qw + 2 * kvw:qw + 6 * kvw],
                            w_in[:, qw + 6 * kvw + 3 * n_heads:]], axis=1)
    cmp_w = w_in[:, qw:qw + 2 * kvw]
    wg = w_in[:, qw + 6 * kvw:qw + 6 * kvw + 3 * n_heads].reshape(-1, ng, 3 * grp)
    gate = jnp.concatenate([wg, jnp.zeros((w_in.shape[0], ng, 128 - 3 * grp), w_in.dtype)], axis=2)
    aux = jnp.concatenate([cmp_w, gate.reshape(w_in.shape[0], ng * 128)], axis=1)
    return main.astype(BF16), aux.astype(BF16)


def _pick(n, candidates):
    for c in candidates:
        if n % c == 0:
            return c
    return n


def kernel(x, mem, rel_bias, norm_mix_w, norm_ffn_w, final_norm_w, mem_norm_w, mem_w_kv, w_out, gdn_w_in, gdn_conv_w, gdn_a_log, gdn_dt_bias, gdn_norm_w, nsa_w_in, nsa_cmp_pos_k, nsa_cmp_w1_k, nsa_cmp_w2_k, nsa_cmp_pos_v, nsa_cmp_w1_v, nsa_cmp_w2_v, ffn_w_up, ffn_conv_w, ffn_conv_b, ffn_w_down):
    batch, seq, d_model = x.shape
    depth = norm_mix_w.shape[0]
    n_heads = d_model // HEAD_DIM
    m_tok = mem.shape[1]
    mw = MEM_HEADS * HEAD_DIM
    d_ff = ffn_w_down.shape[1]
    t = batch * seq
    xf = x.reshape(t, d_model)
    memf = mem.reshape(batch * m_tok, d_model)
    tm = _pick(seq, (1024, 512, 256, 128))
    tm_small = _pick(seq, (512, 256, 128))

    for i in range(depth):
        j = i // 2
        kvw = mem_w_kv[i].astype(BF16)
        kv, _ = norm_matmul(memf, mem_norm_w[i], kvw, kvw[:, :128],
                            tm=_pick(batch * m_tok, (512, 256)), tn=_pick(kvw.shape[1], (512, 256)))
        if i % 2 == 0:
            w_main, w_aux = _gdn_in_weights(gdn_w_in[j], n_heads)
            proj, aux = norm_matmul(xf, norm_mix_w[i], w_main, w_aux, tm=tm_small,
                                    tn=_pick(w_main.shape[1], (512, 256)))
            mix = gdn_mixer(proj, aux, gdn_conv_w[j], gdn_a_log[j], gdn_dt_bias[j], gdn_norm_w[j],
                            batch=batch, seq=seq, lblk=_pick(seq, (512, 256, 128, 64)))
            qm_block = 4 * n_heads * HEAD_DIM // mw
        else:
            ng = n_heads // NSA_GROUP
            w_main, w_aux = _nsa_in_weights(nsa_w_in[j], n_heads)
            proj, aux = norm_matmul(xf, norm_mix_w[i], w_main, w_aux, tm=tm_small,
                                    tn=_pick(w_main.shape[1], (512, 256)))
            pos = jnp.stack([nsa_cmp_pos_k[j], nsa_cmp_pos_v[j]]).astype(F32)
            w1 = jnp.stack([nsa_cmp_w1_k[j], nsa_cmp_w1_v[j]]).astype(BF16)
            w2 = jnp.stack([nsa_cmp_w2_k[j], nsa_cmp_w2_v[j]]).astype(BF16)
            kvc = nsa_compress(aux, pos, w1, w2, batch=batch, seq=seq, n_groups=ng)
            mix = nsa_attention(proj, aux, kvc, rel_bias, batch=batch, seq=seq, n_groups=ng)
            qm_block = (n_heads * HEAD_DIM + 4 * ng * HEAD_DIM) // mw
        mo = mem_attention(proj, kv, batch=batch, seq=seq, q_col_block=qm_block, ts=tm)
        wo = w_out[i].astype(BF16)
        xf = out_proj(xf, mix, mo, wo[:n_heads * HEAD_DIM], wo[n_heads * HEAD_DIM:], tm=tm_small)
        wu = ffn_w_up[i].astype(BF16)
        act = ffn_up(xf, norm_ffn_w[i], wu[:, :d_ff], wu[:, d_ff:], ffn_conv_w[i], ffn_conv_b[i],
                     seq=seq, tm=tm_small, tn=_pick(d_ff, (256, 128)))
        xf = ffn_down(xf, act, ffn_w_down[i].astype(BF16), final_norm_w, tm=tm_small,
                      final_norm=(i == depth - 1))
    return xf.reshape(batch, seq, d_model)
```

```python
import functools
import math

import jax
import jax.numpy as jnp
import numpy as np
from jax import lax
from jax.experimental import pallas as pl
from jax.experimental.pallas import tpu as pltpu

F32 = jnp.float32
BF16 = jnp.bfloat16

HEAD_DIM = 128
GDN_CONV = 4
GDN_CHUNK = 64
GDN_HEADS_PER_STEP = 4
GDN_SUPER = 256
NSA_GROUP = 4
CMP_LEN = 32
CMP_STRIDE = 16
SLC_BLOCK = 64
SLC_TOPK = 16
WINDOW = 512
NSA_QB = 256
NSA_ROWBLK = 128
NSA_STEPS = 5
MEM_HEADS = 4
NUM_BUCKETS = 32
MAX_DISTANCE = 128
FFN_CONV = 3
EPS = 1e-6
LOG2E = math.log2(math.e)
NEG = -1e30
SEL_PENALTY = 32768.0
VMEM_LIMIT = 56 * 1024 * 1024


def _cparams(sem):
    return pltpu.CompilerParams(dimension_semantics=sem, vmem_limit_bytes=VMEM_LIMIT)


def _dot(a, b):
    return jnp.dot(a, b, preferred_element_type=F32)


def _dot_nt(a, b):
    return lax.dot_general(a, b, (((1,), (1,)), ((), ())), preferred_element_type=F32)


def _silu(x):
    return x * (1.0 / (1.0 + jnp.exp(-x)))


def _sigmoid(x):
    return 1.0 / (1.0 + jnp.exp(-x))


def _softplus(x):
    return jnp.maximum(x, 0.0) + jnp.log(1.0 + jnp.exp(-jnp.abs(x)))


def _shift_rows(x, prev, k):
    sub = lax.broadcasted_iota(jnp.int32, (8, x.shape[1]), 0)
    rolled = pltpu.roll(x, k, 0)
    head = jnp.where(sub < k, pltpu.roll(prev, k, 0), rolled[0:8, :])
    return jnp.concatenate([head, rolled[8:, :]], axis=0)


def _split3(x):
    h = x.astype(BF16)
    r = x - h.astype(F32)
    m = r.astype(BF16)
    l = (r - m.astype(F32)).astype(BF16)
    return h, m, l


def _norm_mm_kernel(x_ref, nw_ref, w_ref, waux_ref, o_ref, oaux_ref, *, rs, tn):
    for r0 in range(0, x_ref.shape[0], rs):
        rows = slice(r0, r0 + rs)
        x = x_ref[rows, :]
        ms = jnp.mean(x * x, axis=-1, keepdims=True)
        xn = (x * lax.rsqrt(ms + EPS) * nw_ref[...]).astype(BF16)
        oaux_ref[rows, :] = _dot(xn, waux_ref[...])
        for c0 in range(0, o_ref.shape[1], tn):
            o_ref[rows, c0:c0 + tn] = _dot(xn, w_ref[:, c0:c0 + tn]).astype(o_ref.dtype)


def norm_matmul(x, nw, w, waux, *, tm, tn):
    t, d = x.shape
    n = w.shape[1]
    na = waux.shape[1]
    whole = lambda i: (0, 0)
    return pl.pallas_call(
        functools.partial(_norm_mm_kernel, rs=min(tm, 256), tn=tn),
        grid=(t // tm,),
        in_specs=[
            pl.BlockSpec((tm, d), lambda i: (i, 0)),
            pl.BlockSpec((1, d), whole),
            pl.BlockSpec((d, n), whole),
            pl.BlockSpec((d, na), whole),
        ],
        out_specs=[
            pl.BlockSpec((tm, n), lambda i: (i, 0)),
            pl.BlockSpec((tm, na), lambda i: (i, 0)),
        ],
        out_shape=[jax.ShapeDtypeStruct((t, n), BF16), jax.ShapeDtypeStruct((t, na), F32)],
        compiler_params=_cparams(("parallel",)),
        name="norm_matmul",
    )(x, nw.reshape(1, d), w, waux)


def _mem_attn_kernel(q_ref, kv_ref, o_ref):
    scale = HEAD_DIM ** -0.5
    mw = MEM_HEADS * HEAD_DIM
    for h in range(MEM_HEADS):
        lo = h * HEAD_DIM
        q = q_ref[:, lo:lo + HEAD_DIM]
        k = kv_ref[:, lo:lo + HEAD_DIM]
        v = kv_ref[:, mw + lo:mw + lo + HEAD_DIM]
        s = _dot_nt(q, k) * scale
        m = jnp.max(s, axis=-1, keepdims=True)
        p = jnp.exp(s - m)
        l = jnp.sum(p, axis=-1, keepdims=True)
        o = _dot(p.astype(BF16), v) / l
        o_ref[:, lo:lo + HEAD_DIM] = o.astype(o_ref.dtype)


def mem_attention(proj, kv, *, batch, seq, q_col_block, ts):
    mw = MEM_HEADS * HEAD_DIM
    m_tok = kv.shape[0] // batch
    nt = seq // ts
    return pl.pallas_call(
        _mem_attn_kernel,
        grid=(batch, nt),
        in_specs=[
            pl.BlockSpec((ts, mw), lambda b, i: (b * nt + i, q_col_block)),
            pl.BlockSpec((m_tok, 2 * mw), lambda b, i: (b, 0)),
        ],
        out_specs=pl.BlockSpec((ts, mw), lambda b, i: (b * nt + i, 0)),
        out_shape=jax.ShapeDtypeStruct((batch * seq, mw), BF16),
        compiler_params=_cparams(("parallel", "parallel")),
        name="mem_attention",
    )(proj, kv)


def _out_proj_kernel(x_ref, a_ref, b_ref, wa_ref, wb_ref, o_ref):
    o_ref[...] = x_ref[...] + _dot(a_ref[...], wa_ref[...]) + _dot(b_ref[...], wb_ref[...])


def out_proj(x, a, b, wa, wb, *, tm):
    t, d = x.shape
    ka, kb = a.shape[1], b.shape[1]
    return pl.pallas_call(
        _out_proj_kernel,
        grid=(t // tm,),
        in_specs=[
            pl.BlockSpec((tm, d), lambda i: (i, 0)),
            pl.BlockSpec((tm, ka), lambda i: (i, 0)),
            pl.BlockSpec((tm, kb), lambda i: (i, 0)),
            pl.BlockSpec((ka, d), lambda i: (0, 0)),
            pl.BlockSpec((kb, d), lambda i: (0, 0)),
        ],
        out_specs=pl.BlockSpec((tm, d), lambda i: (i, 0)),
        out_shape=jax.ShapeDtypeStruct((t, d), F32),
        compiler_params=_cparams(("parallel",)),
        name="out_proj",
    )(x, a, b, wa, wb)


def _ffn_up_kernel(x_ref, nw_ref, wg_ref, wv_ref, cw_ref, cb_ref, o_ref, gs_ref,
                   *, tiles_per_seq, tn, rs):
    tm = x_ref.shape[0]
    dff = o_ref.shape[1]
    first = pl.program_id(0) % tiles_per_seq == 0

    for r0 in range(0, tm, rs):
        rows = slice(r0, r0 + rs)
        x = x_ref[rows, :]
        ms = jnp.mean(x * x, axis=-1, keepdims=True)
        xn = (x * lax.rsqrt(ms + EPS) * nw_ref[...]).astype(BF16)
        for j in range(dff // tn):
            cols = slice(j * tn, (j + 1) * tn)
            g = _dot(xn, wg_ref[:, cols])
            v = _dot(xn, wv_ref[:, cols])
            prev = gs_ref[:, cols]
            if r0 == 0:
                prev = jnp.where(first, 0.0, prev)
            gs_ref[:, cols] = g[rs - 8:, :]
            conv = (_shift_rows(g, prev, 2) * cw_ref[0:1, cols] + _shift_rows(g, prev, 1) * cw_ref[1:2, cols]
                    + g * cw_ref[2:3, cols] + cb_ref[:, cols])
            o_ref[rows, cols] = (_silu(conv) * v).astype(o_ref.dtype)


def ffn_up(x, nw, wg, wv, cw, cb, *, seq, tm, tn):
    t, d = x.shape
    dff = wg.shape[1]
    kern = functools.partial(_ffn_up_kernel, tiles_per_seq=seq // tm, tn=tn, rs=min(tm, 256))
    whole = lambda i: (0, 0)
    return pl.pallas_call(
        kern,
        grid=(t // tm,),
        in_specs=[
            pl.BlockSpec((tm, d), lambda i: (i, 0)),
            pl.BlockSpec((1, d), whole),
            pl.BlockSpec((d, dff), whole),
            pl.BlockSpec((d, dff), whole),
            pl.BlockSpec((8, dff), whole),
            pl.BlockSpec((1, dff), whole),
        ],
        out_specs=pl.BlockSpec((tm, dff), lambda i: (i, 0)),
        out_shape=jax.ShapeDtypeStruct((t, dff), BF16),
        scratch_shapes=[
            pltpu.VMEM((8, dff), F32),
        ],
        compiler_params=_cparams(("arbitrary",)),
        name="ffn_up",
    )(x, nw.reshape(1, d), wg, wv, jnp.pad(cw, ((0, 8 - cw.shape[0]), (0, 0))), cb.reshape(1, dff))


def _ffn_down_kernel(x_ref, a_ref, w_ref, fw_ref, o_ref, *, final_norm):
    y = x_ref[...] + _dot(a_ref[...], w_ref[...])
    if final_norm:
        ms = jnp.mean(y * y, axis=-1, keepdims=True)
        y = y * lax.rsqrt(ms + EPS) * fw_ref[...]
    o_ref[...] = y


def ffn_down(x, a, w, fw, *, tm, final_norm):
    t, d = x.shape
    k = a.shape[1]
    return pl.pallas_call(
        functools.partial(_ffn_down_kernel, final_norm=final_norm),
        grid=(t // tm,),
        in_specs=[
            pl.BlockSpec((tm, d), lambda i: (i, 0)),
            pl.BlockSpec((tm, k), lambda i: (i, 0)),
            pl.BlockSpec((k, d), lambda i: (0, 0)),
            pl.BlockSpec((1, d), lambda i: (0, 0)),
        ],
        out_specs=pl.BlockSpec((tm, d), lambda i: (i, 0)),
        out_shape=jax.ShapeDtypeStruct((t, d), F32),
        compiler_params=_cparams(("parallel",)),
        name="ffn_down",
    )(x, a, w, fw.reshape(1, d))


def _gdn_kernel(q_ref, k_ref, v_ref, z_ref, gt_ref, cw_ref, arow_ref, dtrow_ref, nw_ref, o_ref,
                qs_ref, ks_ref, vs_ref, state_ref, *, n_chunks):
    hp = GDN_HEADS_PER_STEP
    dh = HEAD_DIM
    c = GDN_CHUNK
    lblk = q_ref.shape[0]
    sb = pl.program_id(2)

    @pl.when(sb == 0)
    def _():
        zero8 = jnp.zeros((8, hp * dh), F32)
        qs_ref[0:8, :] = zero8
        ks_ref[0:8, :] = zero8
        vs_ref[0:8, :] = zero8
        state_ref[...] = jnp.zeros(state_ref.shape, F32)

    for idx, (src, dst) in enumerate(((q_ref, qs_ref), (k_ref, ks_ref), (v_ref, vs_ref))):
        x = src[...].astype(F32)
        prev = dst[0:8, :]
        w = cw_ref[idx]
        y = (_shift_rows(x, prev, 3) * w[0:1, :] + _shift_rows(x, prev, 2) * w[1:2, :]
             + _shift_rows(x, prev, 1) * w[2:3, :] + x * w[3:4, :])
        dst[8:, :] = _silu(y)
        dst[0:8, :] = x[lblk - 8:, :]

    sl = GDN_SUPER
    cps = sl // c
    row = lax.broadcasted_iota(jnp.int32, (sl, sl), 0)
    col = lax.broadcasted_iota(jnp.int32, (sl, sl), 1)
    same = (row // c) == (col // c)
    incl = same & (row >= col)
    strict = same & (row > col)
    tril = jnp.where(incl, 1.0, 0.0).astype(BF16)
    ones_blk = jnp.where(same, 1.0, 0.0).astype(BF16)
    arow = arow_ref[0]
    dtrow = dtrow_ref[0]
    nw = nw_ref[...]
    states = [state_ref[r] for r in range(hp)]
    zeros_c = jnp.zeros((c, dh), BF16)

    for sci in range(lblk // sl):
        r0 = sci * sl
        gt = gt_ref[r0:r0 + sl, :]
        beta_all = _sigmoid(gt)
        g_all = -arow * _softplus(gt + dtrow)
        gh, gm, gl = _split3(g_all)
        gc_all = _dot(tril, gh) + _dot(tril, gm) + _dot(tril, gl)
        gend_all = _dot(ones_blk, gh) + _dot(ones_blk, gm) + _dot(ones_blk, gl)
        gc_t = gc_all.T
        pre = []
        for r in range(hp):
            lo = r * dh
            qh = qs_ref[r0 + 8:r0 + 8 + sl, lo:lo + dh]
            kh = ks_ref[r0 + 8:r0 + 8 + sl, lo:lo + dh]
            vh = vs_ref[r0 + 8:r0 + 8 + sl, lo:lo + dh]
            qn = qh * lax.rsqrt(jnp.sum(qh * qh, axis=-1, keepdims=True) + EPS) * (dh ** -0.5)
            kn = kh * lax.rsqrt(jnp.sum(kh * kh, axis=-1, keepdims=True) + EPS)
            beta = beta_all[:, r:r + 1]
            gc_col = gc_all[:, hp + r:hp + r + 1]
            gc_row = gc_t[hp + r:hp + r + 1, :]
            gend = gend_all[:, hp + r:hp + r + 1]
            decay = jnp.where(incl, jnp.exp(jnp.where(incl, gc_col - gc_row, 0.0)), 0.0)
            eg = jnp.exp(gc_col)
            kb = kn * beta
            kn_b = kn.astype(BF16)
            lmat = jnp.where(strict, _dot_nt(kb.astype(BF16), kn_b) * decay, 0.0)
            a_intra = jnp.where(incl, _dot_nt(qn.astype(BF16), kn_b) * decay, 0.0).astype(BF16)
            x = jnp.concatenate([vh * beta, kb * eg], axis=1)
            mp = -lmat
            for it in range(6):
                mp_b = mp.astype(BF16)
                x = x + _dot(mp_b, x.astype(BF16))
                if it < 5:
                    mp = _dot(mp_b, mp_b)
            pre.append((x[:, :dh], x[:, dh:], qn * eg, kn * jnp.exp(gend - gc_col), a_intra,
                        jnp.exp(gend)))

        for ci in range(cps):
            rows = slice(ci * c, (ci + 1) * c)
            for r in range(hp):
                lo = r * dh
                u, w, qd, kd, a_intra, egend = pre[r]
                state = states[r]
                wq = jnp.concatenate([w[rows], qd[rows]], axis=0).astype(BF16)
                wqs = _dot(wq, state.astype(BF16))
                v_new = u[rows] - wqs[:c]
                v_new_b = v_new.astype(BF16)
                vcat = jnp.concatenate([zeros_c] * ci + [v_new_b] + [zeros_c] * (cps - 1 - ci), axis=0)
                o = wqs[c:] + _dot(a_intra[rows], vcat)
                states[r] = (state * egend[ci * c:ci * c + 1, :]
                             + _dot(kd[rows].T.astype(BF16), v_new_b))
                ms = jnp.mean(o * o, axis=-1, keepdims=True)
                zz = z_ref[r0 + ci * c:r0 + (ci + 1) * c, lo:lo + dh].astype(F32)
                o = o * lax.rsqrt(ms + EPS) * nw * _silu(zz)
                o_ref[r0 + ci * c:r0 + (ci + 1) * c, lo:lo + dh] = o.astype(o_ref.dtype)

    for r in range(hp):
        state_ref[r] = states[r]


def gdn_mixer(proj, gates, conv_w, a_log, dt_bias, norm_w, *, batch, seq, lblk):
    hp = GDN_HEADS_PER_STEP
    dh = HEAD_DIM
    n_heads = a_log.shape[0]
    ng = n_heads // hp
    wd = n_heads * dh
    bw = hp * dh
    nsb = seq // lblk
    arow = jnp.zeros((ng, 1, 128), F32).at[:, 0, hp:2 * hp].set(jnp.exp(a_log.astype(F32)).reshape(ng, hp))
    dtrow = jnp.zeros((ng, 1, 128), F32).at[:, 0, hp:2 * hp].set(dt_bias.astype(F32).reshape(ng, hp))
    cw = conv_w.reshape(GDN_CONV, 3, ng, bw).transpose(1, 2, 0, 3)
    cw = jnp.pad(cw, ((0, 0), (0, 0), (0, 8 - GDN_CONV), (0, 0)))
    kern = functools.partial(_gdn_kernel, n_chunks=lblk // GDN_CHUNK)
    row_map = lambda off: (lambda b, g, s: (b * nsb + s, off + g))
    return pl.pallas_call(
        kern,
        grid=(batch, ng, nsb),
        in_specs=[
            pl.BlockSpec((lblk, bw), row_map(0)),
            pl.BlockSpec((lblk, bw), row_map(ng)),
            pl.BlockSpec((lblk, bw), row_map(2 * ng)),
            pl.BlockSpec((lblk, bw), row_map(3 * ng)),
            pl.BlockSpec((lblk, 128), row_map(0)),
            pl.BlockSpec((3, None, 8, bw), lambda b, g, s: (0, g, 0, 0)),
            pl.BlockSpec((None, 1, 128), lambda b, g, s: (g, 0, 0)),
            pl.BlockSpec((None, 1, 128), lambda b, g, s: (g, 0, 0)),
            pl.BlockSpec((1, dh), lambda b, g, s: (0, 0)),
        ],
        out_specs=pl.BlockSpec((lblk, bw), row_map(0)),
        out_shape=jax.ShapeDtypeStruct((batch * seq, wd), BF16),
        scratch_shapes=[
            pltpu.VMEM((lblk + 8, bw), F32),
            pltpu.VMEM((lblk + 8, bw), F32),
            pltpu.VMEM((lblk + 8, bw), F32),
            pltpu.VMEM((hp, dh, dh), F32),
        ],
        compiler_params=_cparams(("parallel", "parallel", "arbitrary")),
        name="gdn_mixer",
    )(proj, proj, proj, proj, gates, cw, arow, dtrow, norm_w.reshape(1, dh).astype(F32))


def _nsa_compress_kernel(t_ref, pos_ref, w1_ref, w2_ref, o_ref):
    nc = t_ref.shape[0] // CMP_STRIDE
    dh = HEAD_DIM
    acc_a = jnp.zeros((nc, dh), F32)
    acc_b = jnp.zeros((nc, dh), F32)
    for i in range(CMP_STRIDE):
        xi = t_ref[pl.ds(i, nc, stride=CMP_STRIDE), :]
        xa = (xi + pos_ref[i:i + 1, :]).astype(BF16)
        xb = (xi + pos_ref[CMP_STRIDE + i:CMP_STRIDE + i + 1, :]).astype(BF16)
        acc_a = acc_a + _dot(xa, w1_ref[i * dh:(i + 1) * dh, :])
        acc_b = acc_b + _dot(xb, w1_ref[(CMP_STRIDE + i) * dh:(CMP_STRIDE + i + 1) * dh, :])
    h = acc_a + jnp.concatenate([acc_b[1:, :], acc_b[:1, :]], axis=0)
    o_ref[...] = _dot(_silu(h).astype(BF16), w2_ref[...]).astype(o_ref.dtype)


def nsa_compress(aux, pos, w1, w2, *, batch, seq, n_groups):
    dh = HEAD_DIM
    nc = seq // CMP_STRIDE
    return pl.pallas_call(
        _nsa_compress_kernel,
        grid=(batch, 2, n_groups),
        in_specs=[
            pl.BlockSpec((seq, dh), lambda b, j, g: (b, j * n_groups + g)),
            pl.BlockSpec((None, CMP_LEN, dh), lambda b, j, g: (j, 0, 0)),
            pl.BlockSpec((None, CMP_LEN * dh, dh), lambda b, j, g: (j, 0, 0)),
            pl.BlockSpec((None, dh, dh), lambda b, j, g: (j, 0, 0)),
        ],
        out_specs=pl.BlockSpec((None, None, None, nc, dh), lambda b, j, g: (j, b, g, 0, 0)),
        out_shape=jax.ShapeDtypeStruct((2, batch, n_groups, nc, dh), BF16),
        compiler_params=_cparams(("parallel", "parallel", "parallel")),
        name="nsa_compress",
    )(aux, pos, w1, w2)


def _t5_bucket_np(n):
    max_exact = NUM_BUCKETS // 2
    nf = np.maximum(n, 1).astype(np.float32)
    logv = np.log(nf / np.float32(max_exact)) / np.float32(math.log(MAX_DISTANCE / max_exact))
    large = max_exact + (logv * np.float32(NUM_BUCKETS - max_exact)).astype(np.int32)
    large = np.minimum(large, NUM_BUCKETS - 1)
    return np.where(n < max_exact, n, large).astype(np.int32)


def _t5_thresholds():
    n = np.arange(0, MAX_DISTANCE + 1, dtype=np.int32)
    b = _t5_bucket_np(n)
    half = NUM_BUCKETS // 2
    return tuple(int(np.min(n[b >= half + k])) for k in range(1, NUM_BUCKETS - half))


def _nsa_bias_kernel(tab_ref, dtab_ref, tt_ref, *, thresholds, nc):
    h = pl.program_id(0)
    qb = NSA_QB
    half = NUM_BUCKETS // 2
    c31 = tab_ref[NUM_BUCKETS - 1, h]

    def lookup(dist):
        n = jnp.maximum(dist, 0)
        big = jnp.full(n.shape, half, jnp.int32)
        for t in thresholds:
            big = big + jnp.where(n >= t, 1, 0)
        bucket = jnp.where(n < half, n, big)
        val = jnp.zeros(n.shape, F32)
        for b in range(NUM_BUCKETS):
            val = jnp.where(bucket == b, tab_ref[b, h], val)
        return val

    q = lax.broadcasted_iota(jnp.int32, (qb, qb), 0)
    kk = lax.broadcasted_iota(jnp.int32, (qb, qb), 1)
    dtab_ref[0] = jnp.where(q >= kk, (lookup(q - kk) - c31) * LOG2E, NEG)
    dtab_ref[1] = (lookup(q - kk + qb) - c31) * LOG2E
    dtab_ref[2] = jnp.where(kk > q, 0.0, NEG)
    dtab_ref[3] = jnp.full((qb, qb), NEG, F32)
    nw = 2 * qb // CMP_STRIDE
    x = lax.broadcasted_iota(jnp.int32, (nw, qb), 0)
    ql = lax.broadcasted_iota(jnp.int32, (nw, qb), 1)
    dist = ql - CMP_STRIDE * x + (qb - CMP_LEN + 1)
    tt_ref[0:nc, :] = jnp.zeros((nc, qb), F32) + c31 * LOG2E
    tt_ref[nc:nc + nw, :] = jnp.where(dist >= 0, lookup(dist) * LOG2E, NEG)
    tt_ref[nc + nw:, :] = jnp.full((nc, qb), NEG, F32)


def nsa_bias_tables(rel_bias, seq):
    qb = NSA_QB
    nc = seq // CMP_STRIDE
    nw = 2 * qb // CMP_STRIDE
    n_heads = rel_bias.shape[1]
    kern = functools.partial(_nsa_bias_kernel, thresholds=_t5_thresholds(), nc=nc)
    return pl.pallas_call(
        kern,
        grid=(n_heads,),
        in_specs=[pl.BlockSpec(memory_space=pltpu.SMEM)],
        out_specs=[
            pl.BlockSpec((None, 4, qb, qb), lambda h: (h, 0, 0, 0)),
            pl.BlockSpec((None, 2 * nc + nw, qb), lambda h: (h, 0, 0)),
        ],
        out_shape=[jax.ShapeDtypeStruct((n_heads, 4, qb, qb), F32),
                   jax.ShapeDtypeStruct((n_heads, 2 * nc + nw, qb), F32)],
        compiler_params=_cparams(("parallel",)),
        name="nsa_bias_tables",
    )(rel_bias.astype(F32))


def _sel_map_t(seq):
    nc = seq // CMP_STRIDE
    n_slc = seq // SLC_BLOCK
    c_start = np.arange(nc, dtype=np.int32) * CMP_STRIDE
    c_end = c_start + CMP_LEN - 1
    s_start = np.arange(n_slc, dtype=np.int32) * SLC_BLOCK
    m = (c_start[None, :] < s_start[:, None] + SLC_BLOCK) & (s_start[:, None] <= c_end[None, :])
    return m.astype(np.float32)


def _nsa_attn_kernel(cst_ref, q_ref, ksl_ref, vsl_ref, kw_ref, vw_ref, kc_ref, vc_ref, gt_ref,
                     dtab_ref, tt_ref, smt_ref, o_ref,
                     kaug_ref, vaug_ref, qaug_ref, vct_ref, p_ref, m_ref, alpha_ref,
                     acc_ref, *, n_sel):
    qb = NSA_QB
    dh = HEAD_DIM
    grp = NSA_GROUP
    nrows = grp * qb
    rblk = NSA_ROWBLK
    g = pl.program_id(1)
    qi = pl.program_id(2)
    seq = ksl_ref.shape[0]
    nc = kc_ref.shape[0]
    n_slc = seq // SLC_BLOCK
    blocks_per_chunk = qb // SLC_BLOCK

    @pl.when(qi == 0)
    def _():
        lane = lax.broadcasted_iota(jnp.int32, (seq, dh), 1)
        rb = lax.broadcasted_iota(jnp.int32, (seq, dh), 0) // SLC_BLOCK
        ones_tail = jnp.where(lane >= dh - 2, 1.0, 0.0)
        kaug_ref[0, :, 0:dh] = ksl_ref[...]
        kaug_ref[0, :, dh:] = (jnp.where(rb == lane, -SEL_PENALTY, 0.0) + ones_tail).astype(BF16)
        kaug_ref[1, :, 0:dh] = kw_ref[...]
        kaug_ref[1, :, dh:] = ones_tail.astype(BF16)
        ones = jnp.ones((seq, dh), BF16)
        vaug_ref[0, :, 0:dh] = vsl_ref[...]
        vaug_ref[0, :, dh:] = ones
        vaug_ref[1, :, 0:dh] = vw_ref[...]
        vaug_ref[1, :, dh:] = ones
        vct_ref[...] = vc_ref[...].astype(F32).T.astype(BF16)

    for r in range(grp):
        qaug_ref[r * qb:(r + 1) * qb, 0:dh] = q_ref[:, r * dh:(r + 1) * dh]
    q4 = qaug_ref[:, 0:dh]

    st = _dot_nt(kc_ref[...], q4)
    start = pl.multiple_of(nc + CMP_STRIDE - (qb // CMP_STRIDE) * qi, CMP_STRIDE)
    psum = jnp.zeros((nc, qb), F32)
    o_cmp = []
    for r in range(grp):
        bias = tt_ref[r, pl.ds(start, nc), :]
        s = st[:, r * qb:(r + 1) * qb] + bias
        valid = bias > 0.5 * NEG
        m = jnp.max(s, axis=0, keepdims=True)
        p = jnp.where(valid, jnp.exp2(s - m), 0.0)
        p = p / jnp.maximum(jnp.sum(p, axis=0, keepdims=True), 1e-30)
        psum = psum + p
        o_cmp.append(_dot(vct_ref[...], p.astype(BF16)).T)

    ph = psum.astype(BF16)
    pl_ = (psum - ph.astype(F32)).astype(BF16)
    smt = smt_ref[...]
    imp = _dot(smt, ph) + _dot(smt, pl_)
    jb = lax.broadcasted_iota(jnp.int32, (n_slc, qb), 0)
    tb = qi * blocks_per_chunk + lax.broadcasted_iota(jnp.int32, (n_slc, qb), 1) // SLC_BLOCK
    forced = (jb == 0) | (jb == tb) | (jb == tb - 1)
    score = jnp.where(jb <= tb, jnp.where(forced, jnp.inf, imp), -jnp.inf)
    jbf = jb.astype(F32)
    notsel = jnp.ones((n_slc, qb), F32)
    for _ in range(n_sel):
        top = jnp.max(score, axis=0, keepdims=True)
        first = jnp.min(jnp.where(score == top, jbf, float(n_slc)), axis=0, keepdims=True)
        pick = (jbf == first) & (top > -jnp.inf)
        notsel = jnp.where(pick, 0.0, notsel)
        score = jnp.where(pick, -jnp.inf, score)
    notsel = jnp.concatenate([notsel, jnp.zeros((dh - n_slc, qb), F32)], axis=0).T
    lane1 = lax.broadcasted_iota(jnp.int32, (1, dh), 1)
    for r in range(grp):
        h = g * grp + r
        cvec = jnp.where(lane1 == dh - 2, cst_ref[0, h], jnp.where(lane1 == dh - 1, cst_ref[1, h], 0.0))
        qaug_ref[r * qb:(r + 1) * qb, dh:] = (notsel + cvec).astype(BF16)

    m_ref[...] = jnp.full(m_ref.shape, NEG, F32)
    acc_ref[...] = jnp.zeros(acc_ref.shape, F32)

    def flash_steps(steps):
        svals = []
        for n, (_, branch, kchunk, _) in enumerate(steps):
            k0 = pl.multiple_of(kchunk * qb, qb)
            svals.append(_dot_nt(qaug_ref[...], kaug_ref[branch, pl.ds(k0, qb), :]))
        for n, (a, branch, kchunk, tile_idx) in enumerate(steps):
            for rbi in range(nrows // rblk):
                rows = slice(rbi * rblk, (rbi + 1) * rblk)
                s0 = svals[n][rows, 0:dh]
                s1 = svals[n][rows, dh:]
                if tile_idx is not None:
                    head = (rbi * rblk) // qb
                    t0 = (rbi * rblk) % qb
                    s0 = s0 + dtab_ref[head, tile_idx, t0:t0 + rblk, 0:dh]
                    s1 = s1 + dtab_ref[head, tile_idx, t0:t0 + rblk, dh:]
                m_old = m_ref[a, rows, :]
                m_new = jnp.maximum(m_old, jnp.max(jnp.maximum(s0, s1), axis=-1, keepdims=True))
                alpha_ref[n, rows, :] = jnp.exp2(m_old - m_new)
                p_ref[n, rows, 0:dh] = jnp.exp2(s0 - m_new).astype(BF16)
                p_ref[n, rows, dh:] = jnp.exp2(s1 - m_new).astype(BF16)
                m_ref[a, rows, :] = m_new
            k0 = pl.multiple_of(kchunk * qb, qb)
            pv = _dot(p_ref[n], vaug_ref[branch, pl.ds(k0, qb), :])
            alpha = alpha_ref[n]
            acc_ref[a, :, 0:dh] = acc_ref[a, :, 0:dh] * alpha + pv[:, 0:dh]
            acc_ref[a, :, dh:] = acc_ref[a, :, dh:] * alpha + pv[:, dh:]

    n_far = jnp.maximum(qi - 1, 0)

    def pair_body(i, carry):
        flash_steps([(0, 0, 2 * i, None), (0, 0, 2 * i + 1, None)])
        return carry

    lax.fori_loop(0, n_far // 2, pair_body, 0)

    @pl.when(n_far % 2 == 1)
    def _():
        flash_steps([(0, 0, n_far - 1, None)])

    def near(kchunk, tile):
        return jnp.maximum(kchunk, 0), jnp.where(kchunk >= 0, tile, 3)

    k_s1, t_s1 = near(qi - 1, 1)
    k_w2, t_w2 = near(qi - 2, 2)
    flash_steps([(0, 0, k_s1, t_s1), (0, 0, qi, 0),
                 (1, 1, k_w2, t_w2), (1, 1, k_s1, t_s1), (1, 1, qi, 0)])

    gates = _sigmoid(gt_ref[...])
    for r in range(grp):
        rows = slice(r * qb, (r + 1) * qb)
        o_sel = acc_ref[0, rows, 0:dh] / acc_ref[0, rows, dh:]
        o_win = acc_ref[1, rows, 0:dh] / acc_ref[1, rows, dh:]
        o = (gates[:, 3 * r:3 * r + 1] * o_cmp[r] + gates[:, 3 * r + 1:3 * r + 2] * o_sel
             + gates[:, 3 * r + 2:3 * r + 3] * o_win)
        o_ref[:, r * dh:(r + 1) * dh] = o.astype(o_ref.dtype)


def nsa_attention(proj, aux, kvc, rel_bias, *, batch, seq, n_groups):
    qb = NSA_QB
    dh = HEAD_DIM
    grp = NSA_GROUP
    nq = seq // qb
    nc = seq // CMP_STRIDE
    n_slc = seq // SLC_BLOCK
    n_sel = min(SLC_TOPK, n_slc)
    assert n_slc <= dh - 2
    nrows = grp * qb
    dtab, tt = nsa_bias_tables(rel_bias, seq)
    smt = jnp.asarray(_sel_map_t(seq), BF16)
    c31 = rel_bias[NUM_BUCKETS - 1].astype(F32) * LOG2E
    c31_hi = c31.astype(BF16).astype(F32)
    cst = jnp.stack([c31_hi, c31 - c31_hi])
    kv_base = n_groups * grp
    kv_map = lambda j: (lambda b, g, i, c: (b, kv_base + j * n_groups + g))
    grid_spec = pltpu.PrefetchScalarGridSpec(
        num_scalar_prefetch=1,
        grid=(batch, n_groups, nq),
        in_specs=[
            pl.BlockSpec((qb, grp * dh), lambda b, g, i, c: (b * nq + i, g)),
            pl.BlockSpec((seq, dh), kv_map(0)),
            pl.BlockSpec((seq, dh), kv_map(1)),
            pl.BlockSpec((seq, dh), kv_map(2)),
            pl.BlockSpec((seq, dh), kv_map(3)),
            pl.BlockSpec((None, None, None, nc, dh), lambda b, g, i, c: (0, b, g, 0, 0)),
            pl.BlockSpec((None, None, None, nc, dh), lambda b, g, i, c: (1, b, g, 0, 0)),
            pl.BlockSpec((qb, 128), lambda b, g, i, c: (b * nq + i, 2 * n_groups + g)),
            pl.BlockSpec((grp, 4, qb, qb), lambda b, g, i, c: (g, 0, 0, 0)),
            pl.BlockSpec((grp, tt.shape[1], qb), lambda b, g, i, c: (g, 0, 0)),
            pl.BlockSpec((n_slc, nc), lambda b, g, i, c: (0, 0)),
        ],
        out_specs=pl.BlockSpec((qb, grp * dh), lambda b, g, i, c: (b * nq + i, g)),
        scratch_shapes=[
            pltpu.VMEM((2, seq, 2 * dh), BF16),
            pltpu.VMEM((2, seq, 2 * dh), BF16),
            pltpu.VMEM((nrows, 2 * dh), BF16),
            pltpu.VMEM((dh, nc), BF16),
            pltpu.VMEM((NSA_STEPS, nrows, qb), BF16),
            pltpu.VMEM((2, nrows, dh), F32),
            pltpu.VMEM((NSA_STEPS, nrows, dh), F32),
            pltpu.VMEM((2, nrows, 2 * dh), F32),
        ],
    )
    return pl.pallas_call(
        functools.partial(_nsa_attn_kernel, n_sel=n_sel),
        grid_spec=grid_spec,
        out_shape=jax.ShapeDtypeStruct((batch * seq, n_groups * grp * dh), BF16),
        compiler_params=_cparams(("parallel", "parallel", "arbitrary")),
        name="nsa_attention",
    )(cst, proj, proj, proj, proj, proj, kvc, kvc, aux, dtab, tt, smt)


def _gdn_in_weights(w_in, n_heads):
    hp = GDN_HEADS_PER_STEP
    wd = n_heads * HEAD_DIM
    ng = n_heads // hp
    main = jnp.concatenate([w_in[:, :4 * wd], w_in[:, 4 * wd + 2 * n_heads:]], axis=1)
    wb = w_in[:, 4 * wd:4 * wd + n_heads].reshape(-1, ng, hp)
    wa = w_in[:, 4 * wd + n_heads:4 * wd + 2 * n_heads].reshape(-1, ng, hp)
    gate = jnp.concatenate([wb, wa, jnp.zeros((w_in.shape[0], ng, 128 - 2 * hp), w_in.dtype)], axis=2)
    return main.astype(BF16), gate.reshape(w_in.shape[0], ng * 128).astype(BF16)


def _nsa_in_weights(w_in, n_heads):
    grp = NSA_GROUP
    ng = n_heads // grp
    qw = n_heads * HEAD_DIM
    kvw = ng * HEAD_DIM
    main = jnp.concatenate([w_in[:, :qw] * (HEAD_DIM ** -0.5 * LOG2E), w_in[:, qw + 2 * kvw:qw + 6 * kvw],
                            w_in[:, qw + 6 * kvw + 3 * n_heads:]], axis=1)
    cmp_w = w_in[:, qw:qw + 2 * kvw]
    wg = w_in[:, qw + 6 * kvw:qw + 6 * kvw + 3 * n_heads].reshape(-1, ng, 3 * grp)
    gate = jnp.concatenate([wg, jnp.zeros((w_in.shape[0], ng, 128 - 3 * grp), w_in.dtype)], axis=2)
    aux = jnp.concatenate([cmp_w, gate.reshape(w_in.shape[0], ng * 128)], axis=1)
    return main.astype(BF16), aux.astype(BF16)


def _pick(n, candidates):
    for c in candidates:
        if n % c == 0:
            return c
    return n


def kernel(x, mem, rel_bias, norm_mix_w, norm_ffn_w, final_norm_w, mem_norm_w, mem_w_kv, w_out, gdn_w_in, gdn_conv_w, gdn_a_log, gdn_dt_bias, gdn_norm_w, nsa_w_in, nsa_cmp_pos_k, nsa_cmp_w1_k, nsa_cmp_w2_k, nsa_cmp_pos_v, nsa_cmp_w1_v, nsa_cmp_w2_v, ffn_w_up, ffn_conv_w, ffn_conv_b, ffn_w_down):
    batch, seq, d_model = x.shape
    depth = norm_mix_w.shape[0]
    n_heads = d_model // HEAD_DIM
    m_tok = mem.shape[1]
    mw = MEM_HEADS * HEAD_DIM
    d_ff = ffn_w_down.shape[1]
    t = batch * seq
    xf = x.reshape(t, d_model)
    memf = mem.reshape(batch * m_tok, d_model)
    tm = _pick(seq, (1024, 512, 256, 128))
    tm_small = _pick(seq, (512, 256, 128))

    for i in range(depth):
        j = i // 2
        kvw = mem_w_kv[i].astype(BF16)
        kv, _ = norm_matmul(memf, mem_norm_w[i], kvw, kvw[:, :128],
                            tm=_pick(batch * m_tok, (512, 256)), tn=_pick(kvw.shape[1], (512, 256)))
        if i % 2 == 0:
            w_main, w_aux = _gdn_in_weights(gdn_w_in[j], n_heads)
            proj, aux = norm_matmul(xf, norm_mix_w[i], w_main, w_aux, tm=tm_small,
                                    tn=_pick(w_main.shape[1], (512, 256)))
            mix = gdn_mixer(proj, aux, gdn_conv_w[j], gdn_a_log[j], gdn_dt_bias[j], gdn_norm_w[j],
                            batch=batch, seq=seq, lblk=_pick(seq, (512, 256, 128, 64)))
            qm_block = 4 * n_heads * HEAD_DIM // mw
        else:
            ng = n_heads // NSA_GROUP
            w_main, w_aux = _nsa_in_weights(nsa_w_in[j], n_heads)
            proj, aux = norm_matmul(xf, norm_mix_w[i], w_main, w_aux, tm=tm_small,
                                    tn=_pick(w_main.shape[1], (512, 256)))
            pos = jnp.stack([nsa_cmp_pos_k[j], nsa_cmp_pos_v[j]]).astype(F32)
            w1 = jnp.stack([nsa_cmp_w1_k[j], nsa_cmp_w1_v[j]]).astype(BF16)
            w2 = jnp.stack([nsa_cmp_w2_k[j], nsa_cmp_w2_v[j]]).astype(BF16)
            kvc = nsa_compress(aux, pos, w1, w2, batch=batch, seq=seq, n_groups=ng)
            mix = nsa_attention(proj, aux, kvc, rel_bias, batch=batch, seq=seq, n_groups=ng)
            qm_block = (n_heads * HEAD_DIM + 4 * ng * HEAD_DIM) // mw
        mo = mem_attention(proj, kv, batch=batch, seq=seq, q_col_block=qm_block, ts=tm)
        wo = w_out[i].astype(BF16)
        xf = out_proj(xf, mix, mo, wo[:n_heads * HEAD_DIM], wo[n_heads * HEAD_DIM:], tm=tm_small)
        wu = ffn_w_up[i].astype(BF16)
        act = ffn_up(xf, norm_ffn_w[i], wu[:, :d_ff], wu[:, d_ff:], ffn_conv_w[i], ffn_conv_b[i],
                     seq=seq, tm=tm_small, tn=_pick(d_ff, (256, 128)))
        xf = ffn_down(xf, act, ffn_w_down[i].astype(BF16), final_norm_w, tm=tm_small,
                      final_norm=(i == depth - 1))
    return xf.reshape(batch, seq, d_model)
```

```python
import functools
import math

import jax
import jax.numpy as jnp
import numpy as np
from jax import lax
from jax.experimental import pallas as pl
from jax.experimental.pallas import tpu as pltpu

F32 = jnp.float32
BF16 = jnp.bfloat16

HEAD_DIM = 128
GDN_CONV = 4
GDN_CHUNK = 64
GDN_HEADS_PER_STEP = 4
GDN_SUPER = 256
NSA_GROUP = 4
CMP_LEN = 32
CMP_STRIDE = 16
SLC_BLOCK = 64
SLC_TOPK = 16
WINDOW = 512
NSA_QB = 256
NSA_ROWBLK = 128
NSA_STEPS = 5
MEM_HEADS = 4
NUM_BUCKETS = 32
MAX_DISTANCE = 128
FFN_CONV = 3
EPS = 1e-6
LOG2E = math.log2(math.e)
NEG = -1e30
SEL_PENALTY = 32768.0
VMEM_LIMIT = 56 * 1024 * 1024


def _cparams(sem):
    return pltpu.CompilerParams(dimension_semantics=sem, vmem_limit_bytes=VMEM_LIMIT)


def _dot(a, b):
    return jnp.dot(a, b, preferred_element_type=F32)


def _dot_nt(a, b):
    return lax.dot_general(a, b, (((1,), (1,)), ((), ())), preferred_element_type=F32)


def _silu(x):
    return x * (1.0 / (1.0 + jnp.exp(-x)))


def _sigmoid(x):
    return 1.0 / (1.0 + jnp.exp(-x))


def _softplus(x):
    return jnp.maximum(x, 0.0) + jnp.log(1.0 + jnp.exp(-jnp.abs(x)))


def _shift_rows(x, prev, k):
    sub = lax.broadcasted_iota(jnp.int32, (8, x.shape[1]), 0)
    rolled = pltpu.roll(x, k, 0)
    head = jnp.where(sub < k, pltpu.roll(prev, k, 0), rolled[0:8, :])
    return jnp.concatenate([head, rolled[8:, :]], axis=0)


def _split3(x):
    h = x.astype(BF16)
    r = x - h.astype(F32)
    m = r.astype(BF16)
    l = (r - m.astype(F32)).astype(BF16)
    return h, m, l


def _norm_mm_kernel(x_ref, nw_ref, w_ref, waux_ref, o_ref, oaux_ref, *, rs, tn):
    for r0 in range(0, x_ref.shape[0], rs):
        rows = slice(r0, r0 + rs)
        x = x_ref[rows, :]
        ms = jnp.mean(x * x, axis=-1, keepdims=True)
        xn = (x * lax.rsqrt(ms + EPS) * nw_ref[...]).astype(BF16)
        oaux_ref[rows, :] = _dot(xn, waux_ref[...])
        for c0 in range(0, o_ref.shape[1], tn):
            o_ref[rows, c0:c0 + tn] = _dot(xn, w_ref[:, c0:c0 + tn]).astype(o_ref.dtype)


def norm_matmul(x, nw, w, waux, *, tm, tn):
    t, d = x.shape
    n = w.shape[1]
    na = waux.shape[1]
    whole = lambda i: (0, 0)
    return pl.pallas_call(
        functools.partial(_norm_mm_kernel, rs=min(tm, 256), tn=tn),
        grid=(t // tm,),
        in_specs=[
            pl.BlockSpec((tm, d), lambda i: (i, 0)),
            pl.BlockSpec((1, d), whole),
            pl.BlockSpec((d, n), whole),
            pl.BlockSpec((d, na), whole),
        ],
        out_specs=[
            pl.BlockSpec((tm, n), lambda i: (i, 0)),
            pl.BlockSpec((tm, na), lambda i: (i, 0)),
        ],
        out_shape=[jax.ShapeDtypeStruct((t, n), BF16), jax.ShapeDtypeStruct((t, na), F32)],
        compiler_params=_cparams(("parallel",)),
        name="norm_matmul",
    )(x, nw.reshape(1, d), w, waux)


def _mem_attn_kernel(q_ref, kv_ref, o_ref):
    scale = HEAD_DIM ** -0.5
    mw = MEM_HEADS * HEAD_DIM
    for h in range(MEM_HEADS):
        lo = h * HEAD_DIM
        q = q_ref[:, lo:lo + HEAD_DIM]
        k = kv_ref[:, lo:lo + HEAD_DIM]
        v = kv_ref[:, mw + lo:mw + lo + HEAD_DIM]
        s = _dot_nt(q, k) * scale
        m = jnp.max(s, axis=-1, keepdims=True)
        p = jnp.exp(s - m)
        l = jnp.sum(p, axis=-1, keepdims=True)
        o = _dot(p.astype(BF16), v) / l
        o_ref[:, lo:lo + HEAD_DIM] = o.astype(o_ref.dtype)


def mem_attention(proj, kv, *, batch, seq, q_col_block, ts):
    mw = MEM_HEADS * HEAD_DIM
    m_tok = kv.shape[0] // batch
    nt = seq // ts
    return pl.pallas_call(
        _mem_attn_kernel,
        grid=(batch, nt),
        in_specs=[
            pl.BlockSpec((ts, mw), lambda b, i: (b * nt + i, q_col_block)),
            pl.BlockSpec((m_tok, 2 * mw), lambda b, i: (b, 0)),
        ],
        out_specs=pl.BlockSpec((ts, mw), lambda b, i: (b * nt + i, 0)),
        out_shape=jax.ShapeDtypeStruct((batch * seq, mw), BF16),
        compiler_params=_cparams(("parallel", "parallel")),
        name="mem_attention",
    )(proj, kv)


def _mix_ffn_up_kernel(x_ref, a_ref, b_ref, wo_ref, nw_ref, wu_ref, cw_ref, cb_ref, x1_ref, o_ref, gs_ref,
                       *, tiles_per_seq, tn, rs):
    tm = x_ref.shape[0]
    dff = o_ref.shape[1]
    ka = a_ref.shape[1]
    first = pl.program_id(0) % tiles_per_seq == 0

    for r0 in range(0, tm, rs):
        rows = slice(r0, r0 + rs)
        x = (x_ref[rows, :] + _dot(a_ref[rows, :], wo_ref[0:ka, :])
             + _dot(b_ref[rows, :], wo_ref[ka:, :]))
        x1_ref[rows, :] = x
        ms = jnp.mean(x * x, axis=-1, keepdims=True)
        xn = (x * lax.rsqrt(ms + EPS) * nw_ref[...]).astype(BF16)
        for j in range(dff // tn):
            cols = slice(j * tn, (j + 1) * tn)
            g = _dot(xn, wu_ref[:, cols])
            v = _dot(xn, wu_ref[:, dff + j * tn:dff + (j + 1) * tn])
            prev = gs_ref[:, cols]
            if r0 == 0:
                prev = jnp.where(first, 0.0, prev)
            gs_ref[:, cols] = g[rs - 8:, :]
            conv = (_shift_rows(g, prev, 2) * cw_ref[0:1, cols] + _shift_rows(g, prev, 1) * cw_ref[1:2, cols]
                    + g * cw_ref[2:3, cols] + cb_ref[:, cols])
            o_ref[rows, cols] = (_silu(conv) * v).astype(o_ref.dtype)


def mix_ffn_up(x, a, b, wo, nw, wu, cw, cb, *, seq, tm, tn):
    t, d = x.shape
    ka, kb = a.shape[1], b.shape[1]
    dff = wu.shape[1] // 2
    kern = functools.partial(_mix_ffn_up_kernel, tiles_per_seq=seq // tm, tn=tn, rs=min(tm, 256))
    whole = lambda i: (0, 0)
    row = lambda i: (i, 0)
    return pl.pallas_call(
        kern,
        grid=(t // tm,),
        in_specs=[
            pl.BlockSpec((tm, d), row),
            pl.BlockSpec((tm, ka), row),
            pl.BlockSpec((tm, kb), row),
            pl.BlockSpec((ka + kb, d), whole),
            pl.BlockSpec((1, d), whole),
            pl.BlockSpec((d, 2 * dff), whole),
            pl.BlockSpec((8, dff), whole),
            pl.BlockSpec((1, dff), whole),
        ],
        out_specs=[pl.BlockSpec((tm, d), row), pl.BlockSpec((tm, dff), row)],
        out_shape=[jax.ShapeDtypeStruct((t, d), F32), jax.ShapeDtypeStruct((t, dff), BF16)],
        scratch_shapes=[pltpu.VMEM((8, dff), F32)],
        compiler_params=_cparams(("arbitrary",)),
        name="mix_ffn_up",
    )(x, a, b, wo, nw.reshape(1, d), wu, jnp.pad(cw, ((0, 8 - cw.shape[0]), (0, 0))), cb.reshape(1, dff))


def _ffn_down_kernel(x_ref, a_ref, w_ref, fw_ref, o_ref, *, final_norm):
    y = x_ref[...] + _dot(a_ref[...], w_ref[...])
    if final_norm:
        ms = jnp.mean(y * y, axis=-1, keepdims=True)
        y = y * lax.rsqrt(ms + EPS) * fw_ref[...]
    o_ref[...] = y


def ffn_down(x, a, w, fw, *, tm, final_norm):
    t, d = x.shape
    k = a.shape[1]
    return pl.pallas_call(
        functools.partial(_ffn_down_kernel, final_norm=final_norm),
        grid=(t // tm,),
        in_specs=[
            pl.BlockSpec((tm, d), lambda i: (i, 0)),
            pl.BlockSpec((tm, k), lambda i: (i, 0)),
            pl.BlockSpec((k, d), lambda i: (0, 0)),
            pl.BlockSpec((1, d), lambda i: (0, 0)),
        ],
        out_specs=pl.BlockSpec((tm, d), lambda i: (i, 0)),
        out_shape=jax.ShapeDtypeStruct((t, d), F32),
        compiler_params=_cparams(("parallel",)),
        name="ffn_down",
    )(x, a, w, fw.reshape(1, d))


def _gdn_kernel(q_ref, k_ref, v_ref, z_ref, gt_ref, cw_ref, arow_ref, dtrow_ref, nw_ref, o_ref,
                qs_ref, ks_ref, vs_ref, state_ref, *, n_chunks):
    hp = GDN_HEADS_PER_STEP
    dh = HEAD_DIM
    c = GDN_CHUNK
    lblk = q_ref.shape[0]
    sb = pl.program_id(2)

    @pl.when(sb == 0)
    def _():
        zero8 = jnp.zeros((8, hp * dh), F32)
        qs_ref[0:8, :] = zero8
        ks_ref[0:8, :] = zero8
        vs_ref[0:8, :] = zero8
        state_ref[...] = jnp.zeros(state_ref.shape, F32)

    for idx, (src, dst) in enumerate(((q_ref, qs_ref), (k_ref, ks_ref), (v_ref, vs_ref))):
        x = src[...].astype(F32)
        prev = dst[0:8, :]
        w = cw_ref[idx]
        y = (_shift_rows(x, prev, 3) * w[0:1, :] + _shift_rows(x, prev, 2) * w[1:2, :]
             + _shift_rows(x, prev, 1) * w[2:3, :] + x * w[3:4, :])
        dst[8:, :] = _silu(y)
        dst[0:8, :] = x[lblk - 8:, :]

    sl = GDN_SUPER
    cps = sl // c
    row = lax.broadcasted_iota(jnp.int32, (sl, sl), 0)
    col = lax.broadcasted_iota(jnp.int32, (sl, sl), 1)
    same = (row // c) == (col // c)
    incl = same & (row >= col)
    strict = same & (row > col)
    tril = jnp.where(incl, 1.0, 0.0).astype(BF16)
    ones_blk = jnp.where(same, 1.0, 0.0).astype(BF16)
    arow = arow_ref[0]
    dtrow = dtrow_ref[0]
    nw = nw_ref[...]
    states = [state_ref[r] for r in range(hp)]
    zeros_c = jnp.zeros((c, dh), BF16)

    for sci in range(lblk // sl):
        r0 = sci * sl
        gt = gt_ref[r0:r0 + sl, :]
        beta_all = _sigmoid(gt)
        g_all = -arow * _softplus(gt + dtrow)
        gh, gm, gl = _split3(g_all)
        gc_all = _dot(tril, gh) + _dot(tril, gm) + _dot(tril, gl)
        gend_all = _dot(ones_blk, gh) + _dot(ones_blk, gm) + _dot(ones_blk, gl)
        gc_t = gc_all.T
        pre = []
        for r in range(hp):
            lo = r * dh
            qh = qs_ref[r0 + 8:r0 + 8 + sl, lo:lo + dh]
            kh = ks_ref[r0 + 8:r0 + 8 + sl, lo:lo + dh]
            vh = vs_ref[r0 + 8:r0 + 8 + sl, lo:lo + dh]
            qn = qh * lax.rsqrt(jnp.sum(qh * qh, axis=-1, keepdims=True) + EPS) * (dh ** -0.5)
            kn = kh * lax.rsqrt(jnp.sum(kh * kh, axis=-1, keepdims=True) + EPS)
            beta = beta_all[:, r:r + 1]
            gc_col = gc_all[:, hp + r:hp + r + 1]
            gc_row = gc_t[hp + r:hp + r + 1, :]
            gend = gend_all[:, hp + r:hp + r + 1]
            decay = jnp.where(incl, jnp.exp(jnp.where(incl, gc_col - gc_row, 0.0)), 0.0)
            eg = jnp.exp(gc_col)
            kb = kn * beta
            kn_b = kn.astype(BF16)
            lmat = jnp.where(strict, _dot_nt(kb.astype(BF16), kn_b) * decay, 0.0)
            a_intra = jnp.where(incl, _dot_nt(qn.astype(BF16), kn_b) * decay, 0.0).astype(BF16)
            x = jnp.concatenate([vh * beta, kb * eg], axis=1)
            mp = -lmat
            for it in range(6):
                mp_b = mp.astype(BF16)
                x = x + _dot(mp_b, x.astype(BF16))
                if it < 5:
                    mp = _dot(mp_b, mp_b)
            w_b = x[:, dh:].astype(BF16)
            qd_b = (qn * eg).astype(BF16)
            kd = kn * jnp.exp(gend - gc_col)
            pre.append((x[:, :dh],
                        [jnp.concatenate([w_b[ci * c:(ci + 1) * c], qd_b[ci * c:(ci + 1) * c]], axis=0)
                         for ci in range(cps)],
                        [kd[ci * c:(ci + 1) * c].T.astype(BF16) for ci in range(cps)],
                        a_intra, jnp.exp(gend)))

        for ci in range(cps):
            rows = slice(ci * c, (ci + 1) * c)
            for r in range(hp):
                lo = r * dh
                u, wq, kd_t, a_intra, egend = pre[r]
                state = states[r]
                wqs = _dot(wq[ci], state.astype(BF16))
                v_new = u[rows] - wqs[:c]
                v_new_b = v_new.astype(BF16)
                vcat = jnp.concatenate([zeros_c] * ci + [v_new_b] + [zeros_c] * (cps - 1 - ci), axis=0)
                o = wqs[c:] + _dot(a_intra[rows], vcat)
                states[r] = state * egend[ci * c:ci * c + 1, :] + _dot(kd_t[ci], v_new_b)
                ms = jnp.mean(o * o, axis=-1, keepdims=True)
                zz = z_ref[r0 + ci * c:r0 + (ci + 1) * c, lo:lo + dh].astype(F32)
                o = o * lax.rsqrt(ms + EPS) * nw * _silu(zz)
                o_ref[r0 + ci * c:r0 + (ci + 1) * c, lo:lo + dh] = o.astype(o_ref.dtype)

    for r in range(hp):
        state_ref[r] = states[r]


def gdn_mixer(proj, gates, conv_w, a_log, dt_bias, norm_w, *, batch, seq, lblk):
    hp = GDN_HEADS_PER_STEP
    dh = HEAD_DIM
    n_heads = a_log.shape[0]
    ng = n_heads // hp
    wd = n_heads * dh
    bw = hp * dh
    nsb = seq // lblk
    arow = jnp.zeros((ng, 1, 128), F32).at[:, 0, hp:2 * hp].set(jnp.exp(a_log.astype(F32)).reshape(ng, hp))
    dtrow = jnp.zeros((ng, 1, 128), F32).at[:, 0, hp:2 * hp].set(dt_bias.astype(F32).reshape(ng, hp))
    cw = conv_w.reshape(GDN_CONV, 3, ng, bw).transpose(1, 2, 0, 3)
    cw = jnp.pad(cw, ((0, 0), (0, 0), (0, 8 - GDN_CONV), (0, 0)))
    kern = functools.partial(_gdn_kernel, n_chunks=lblk // GDN_CHUNK)
    row_map = lambda off: (lambda b, g, s: (b * nsb + s, off + g))
    return pl.pallas_call(
        kern,
        grid=(batch, ng, nsb),
        in_specs=[
            pl.BlockSpec((lblk, bw), row_map(0)),
            pl.BlockSpec((lblk, bw), row_map(ng)),
            pl.BlockSpec((lblk, bw), row_map(2 * ng)),
            pl.BlockSpec((lblk, bw), row_map(3 * ng)),
            pl.BlockSpec((lblk, 128), row_map(0)),
            pl.BlockSpec((3, None, 8, bw), lambda b, g, s: (0, g, 0, 0)),
            pl.BlockSpec((None, 1, 128), lambda b, g, s: (g, 0, 0)),
            pl.BlockSpec((None, 1, 128), lambda b, g, s: (g, 0, 0)),
            pl.BlockSpec((1, dh), lambda b, g, s: (0, 0)),
        ],
        out_specs=pl.BlockSpec((lblk, bw), row_map(0)),
        out_shape=jax.ShapeDtypeStruct((batch * seq, wd), BF16),
        scratch_shapes=[
            pltpu.VMEM((lblk + 8, bw), F32),
            pltpu.VMEM((lblk + 8, bw), F32),
            pltpu.VMEM((lblk + 8, bw), F32),
            pltpu.VMEM((hp, dh, dh), F32),
        ],
        compiler_params=_cparams(("parallel", "parallel", "arbitrary")),
        name="gdn_mixer",
    )(proj, proj, proj, proj, gates, cw, arow, dtrow, norm_w.reshape(1, dh).astype(F32))


def _nsa_compress_kernel(t_ref, pos_ref, w1_ref, w2_ref, o_ref):
    nc = t_ref.shape[0] // CMP_STRIDE
    dh = HEAD_DIM
    acc_a = jnp.zeros((nc, dh), F32)
    acc_b = jnp.zeros((nc, dh), F32)
    for i in range(CMP_STRIDE):
        xi = t_ref[pl.ds(i, nc, stride=CMP_STRIDE), :]
        xa = (xi + pos_ref[i:i + 1, :]).astype(BF16)
        xb = (xi + pos_ref[CMP_STRIDE + i:CMP_STRIDE + i + 1, :]).astype(BF16)
        acc_a = acc_a + _dot(xa, w1_ref[i * dh:(i + 1) * dh, :])
        acc_b = acc_b + _dot(xb, w1_ref[(CMP_STRIDE + i) * dh:(CMP_STRIDE + i + 1) * dh, :])
    h = acc_a + pltpu.roll(acc_b, nc - 1, 0)
    o_ref[...] = _dot(_silu(h).astype(BF16), w2_ref[...]).astype(o_ref.dtype)


def nsa_compress(aux, pos, w1, w2, *, batch, seq, n_groups):
    dh = HEAD_DIM
    nc = seq // CMP_STRIDE
    return pl.pallas_call(
        _nsa_compress_kernel,
        grid=(batch, 2, n_groups),
        in_specs=[
            pl.BlockSpec((seq, dh), lambda b, j, g: (b, j * n_groups + g)),
            pl.BlockSpec((None, CMP_LEN, dh), lambda b, j, g: (j, 0, 0)),
            pl.BlockSpec((None, CMP_LEN * dh, dh), lambda b, j, g: (j, 0, 0)),
            pl.BlockSpec((None, dh, dh), lambda b, j, g: (j, 0, 0)),
        ],
        out_specs=pl.BlockSpec((None, None, None, nc, dh), lambda b, j, g: (j, b, g, 0, 0)),
        out_shape=jax.ShapeDtypeStruct((2, batch, n_groups, nc, dh), BF16),
        compiler_params=_cparams(("parallel", "parallel", "parallel")),
        name="nsa_compress",
    )(aux, pos, w1, w2)


def _t5_bucket_np(n):
    max_exact = NUM_BUCKETS // 2
    nf = np.maximum(n, 1).astype(np.float32)
    logv = np.log(nf / np.float32(max_exact)) / np.float32(math.log(MAX_DISTANCE / max_exact))
    large = max_exact + (logv * np.float32(NUM_BUCKETS - max_exact)).astype(np.int32)
    large = np.minimum(large, NUM_BUCKETS - 1)
    return np.where(n < max_exact, n, large).astype(np.int32)


def _t5_thresholds():
    n = np.arange(0, MAX_DISTANCE + 1, dtype=np.int32)
    b = _t5_bucket_np(n)
    half = NUM_BUCKETS // 2
    return tuple(int(np.min(n[b >= half + k])) for k in range(1, NUM_BUCKETS - half))


def _nsa_bias_kernel(tab_ref, dtab_ref, tt_ref, *, thresholds, nc):
    h = pl.program_id(0)
    qb = NSA_QB
    half = NUM_BUCKETS // 2
    c31 = tab_ref[NUM_BUCKETS - 1, h]

    def lookup(dist):
        n = jnp.maximum(dist, 0)
        big = jnp.full(n.shape, half, jnp.int32)
        for t in thresholds:
            big = big + jnp.where(n >= t, 1, 0)
        bucket = jnp.where(n < half, n, big)
        val = jnp.zeros(n.shape, F32)
        for b in range(NUM_BUCKETS):
            val = jnp.where(bucket == b, tab_ref[b, h], val)
        return val

    q = lax.broadcasted_iota(jnp.int32, (qb, qb), 0)
    kk = lax.broadcasted_iota(jnp.int32, (qb, qb), 1)
    dtab_ref[0] = jnp.where(q >= kk, (lookup(q - kk) - c31) * LOG2E, NEG)
    dtab_ref[1] = (lookup(q - kk + qb) - c31) * LOG2E
    dtab_ref[2] = jnp.where(kk > q, 0.0, NEG)
    dtab_ref[3] = jnp.full((qb, qb), NEG, F32)
    nw = 2 * qb // CMP_STRIDE
    x = lax.broadcasted_iota(jnp.int32, (nw, qb), 0)
    ql = lax.broadcasted_iota(jnp.int32, (nw, qb), 1)
    dist = ql - CMP_STRIDE * x + (qb - CMP_LEN + 1)
    tt_ref[0:nc, :] = jnp.zeros((nc, qb), F32) + c31 * LOG2E
    tt_ref[nc:nc + nw, :] = jnp.where(dist >= 0, lookup(dist) * LOG2E, NEG)
    tt_ref[nc + nw:, :] = jnp.full((nc, qb), NEG, F32)


def nsa_bias_tables(rel_bias, seq):
    qb = NSA_QB
    nc = seq // CMP_STRIDE
    nw = 2 * qb // CMP_STRIDE
    n_heads = rel_bias.shape[1]
    kern = functools.partial(_nsa_bias_kernel, thresholds=_t5_thresholds(), nc=nc)
    return pl.pallas_call(
        kern,
        grid=(n_heads,),
        in_specs=[pl.BlockSpec(memory_space=pltpu.SMEM)],
        out_specs=[
            pl.BlockSpec((None, 4, qb, qb), lambda h: (h, 0, 0, 0)),
            pl.BlockSpec((None, 2 * nc + nw, qb), lambda h: (h, 0, 0)),
        ],
        out_shape=[jax.ShapeDtypeStruct((n_heads, 4, qb, qb), F32),
                   jax.ShapeDtypeStruct((n_heads, 2 * nc + nw, qb), F32)],
        compiler_params=_cparams(("parallel",)),
        name="nsa_bias_tables",
    )(rel_bias.astype(F32))


def _sel_map_t(seq):
    nc = seq // CMP_STRIDE
    n_slc = seq // SLC_BLOCK
    c_start = np.arange(nc, dtype=np.int32) * CMP_STRIDE
    c_end = c_start + CMP_LEN - 1
    s_start = np.arange(n_slc, dtype=np.int32) * SLC_BLOCK
    m = (c_start[None, :] < s_start[:, None] + SLC_BLOCK) & (s_start[:, None] <= c_end[None, :])
    return m.astype(np.float32)


def _nsa_attn_kernel(cst_ref, q_ref, ksl_ref, vsl_ref, kw_ref, vw_ref, kc_ref, vc_ref, gt_ref,
                     dtab_ref, tt_ref, smt_ref, o_ref,
                     kaug_ref, vaug_ref, qaug_ref, vct_ref, p_ref, m_ref, alpha_ref,
                     acc_ref, *, n_sel):
    qb = NSA_QB
    dh = HEAD_DIM
    grp = NSA_GROUP
    nrows = grp * qb
    rblk = NSA_ROWBLK
    g = pl.program_id(1)
    qi = pl.program_id(2)
    seq = ksl_ref.shape[0]
    nc = kc_ref.shape[0]
    n_slc = seq // SLC_BLOCK
    blocks_per_chunk = qb // SLC_BLOCK

    @pl.when(qi == 0)
    def _():
        lane = lax.broadcasted_iota(jnp.int32, (seq, dh), 1)
        rb = lax.broadcasted_iota(jnp.int32, (seq, dh), 0) // SLC_BLOCK
        ones_tail = jnp.where(lane >= dh - 2, 1.0, 0.0)
        kaug_ref[0, :, 0:dh] = ksl_ref[...]
        kaug_ref[0, :, dh:] = (jnp.where(rb == lane, -SEL_PENALTY, 0.0) + ones_tail).astype(BF16)
        kaug_ref[1, :, 0:dh] = kw_ref[...]
        kaug_ref[1, :, dh:] = ones_tail.astype(BF16)
        ones = jnp.ones((seq, dh), BF16)
        vaug_ref[0, :, 0:dh] = vsl_ref[...]
        vaug_ref[0, :, dh:] = ones
        vaug_ref[1, :, 0:dh] = vw_ref[...]
        vaug_ref[1, :, dh:] = ones
        vct_ref[...] = vc_ref[...].astype(F32).T.astype(BF16)

    for r in range(grp):
        qaug_ref[r * qb:(r + 1) * qb, 0:dh] = q_ref[:, r * dh:(r + 1) * dh]
    q4 = qaug_ref[:, 0:dh]

    st = _dot_nt(kc_ref[...], q4)
    start = pl.multiple_of(nc + CMP_STRIDE - (qb // CMP_STRIDE) * qi, CMP_STRIDE)
    psum = jnp.zeros((nc, qb), F32)
    o_cmp = []
    for r in range(grp):
        bias = tt_ref[r, pl.ds(start, nc), :]
        s = st[:, r * qb:(r + 1) * qb] + bias
        valid = bias > 0.5 * NEG
        m = jnp.max(s, axis=0, keepdims=True)
        p = jnp.where(valid, jnp.exp2(s - m), 0.0)
        p = p / jnp.maximum(jnp.sum(p, axis=0, keepdims=True), 1e-30)
        psum = psum + p
        o_cmp.append(_dot(vct_ref[...], p.astype(BF16)).T)

    ph = psum.astype(BF16)
    pl_ = (psum - ph.astype(F32)).astype(BF16)
    smt = smt_ref[...]
    imp = _dot(smt, ph) + _dot(smt, pl_)
    jb = lax.broadcasted_iota(jnp.int32, (n_slc, qb), 0)
    tb = qi * blocks_per_chunk + lax.broadcasted_iota(jnp.int32, (n_slc, qb), 1) // SLC_BLOCK
    forced = (jb == 0) | (jb == tb) | (jb == tb - 1)
    score = jnp.where(jb <= tb, jnp.where(forced, jnp.inf, imp), -jnp.inf)
    jbf = jb.astype(F32)
    taken = forced & (jb <= tb)
    notsel = jnp.where(taken, 0.0, 1.0)
    score = jnp.where(taken, -jnp.inf, score)
    for _ in range(max(n_sel - 3, 0)):
        top = jnp.max(score, axis=0, keepdims=True)
        first = jnp.min(jnp.where(score == top, jbf, float(n_slc)), axis=0, keepdims=True)
        pick = (jbf == first) & (top > -jnp.inf)
        notsel = jnp.where(pick, 0.0, notsel)
        score = jnp.where(pick, -jnp.inf, score)
    notsel = jnp.concatenate([notsel, jnp.zeros((dh - n_slc, qb), F32)], axis=0).T
    lane1 = lax.broadcasted_iota(jnp.int32, (1, dh), 1)
    for r in range(grp):
        h = g * grp + r
        cvec = jnp.where(lane1 == dh - 2, cst_ref[0, h], jnp.where(lane1 == dh - 1, cst_ref[1, h], 0.0))
        qaug_ref[r * qb:(r + 1) * qb, dh:] = (notsel + cvec).astype(BF16)

    m_ref[...] = jnp.full(m_ref.shape, NEG, F32)
    acc_ref[...] = jnp.zeros(acc_ref.shape, F32)

    def flash_steps(steps):
        svals = []
        for n, (_, branch, kchunk, _) in enumerate(steps):
            k0 = pl.multiple_of(kchunk * qb, qb)
            svals.append(_dot_nt(qaug_ref[...], kaug_ref[branch, pl.ds(k0, qb), :]))
        for n, (a, branch, kchunk, tile_idx) in enumerate(steps):
            for rbi in range(nrows // rblk):
                rows = slice(rbi * rblk, (rbi + 1) * rblk)
                s0 = svals[n][rows, 0:dh]
                s1 = svals[n][rows, dh:]
                if tile_idx is not None:
                    head = (rbi * rblk) // qb
                    t0 = (rbi * rblk) % qb
                    s0 = s0 + dtab_ref[head, tile_idx, t0:t0 + rblk, 0:dh]
                    s1 = s1 + dtab_ref[head, tile_idx, t0:t0 + rblk, dh:]
                m_old = m_ref[a, rows, :]
                m_new = jnp.maximum(m_old, jnp.max(jnp.maximum(s0, s1), axis=-1, keepdims=True))
                alpha_ref[n, rows, :] = jnp.exp2(m_old - m_new)
                p_ref[n, rows, 0:dh] = jnp.exp2(s0 - m_new).astype(BF16)
                p_ref[n, rows, dh:] = jnp.exp2(s1 - m_new).astype(BF16)
                m_ref[a, rows, :] = m_new
            k0 = pl.multiple_of(kchunk * qb, qb)
            pv = _dot(p_ref[n], vaug_ref[branch, pl.ds(k0, qb), :])
            alpha = alpha_ref[n]
            acc_ref[a, :, 0:dh] = acc_ref[a, :, 0:dh] * alpha + pv[:, 0:dh]
            acc_ref[a, :, dh:] = acc_ref[a, :, dh:] * alpha + pv[:, dh:]

    n_far = jnp.maximum(qi - 1, 0)

    def pair_body(i, carry):
        flash_steps([(0, 0, 2 * i, None), (0, 0, 2 * i + 1, None)])
        return carry

    lax.fori_loop(0, n_far // 2, pair_body, 0)

    @pl.when(n_far % 2 == 1)
    def _():
        flash_steps([(0, 0, n_far - 1, None)])

    def near(kchunk, tile):
        return jnp.maximum(kchunk, 0), jnp.where(kchunk >= 0, tile, 3)

    k_s1, t_s1 = near(qi - 1, 1)
    k_w2, t_w2 = near(qi - 2, 2)
    flash_steps([(0, 0, k_s1, t_s1), (0, 0, qi, 0),
                 (1, 1, k_w2, t_w2), (1, 1, k_s1, t_s1), (1, 1, qi, 0)])

    gates = _sigmoid(gt_ref[...])
    for r in range(grp):
        rows = slice(r * qb, (r + 1) * qb)
        o_sel = acc_ref[0, rows, 0:dh] / acc_ref[0, rows, dh:]
        o_win = acc_ref[1, rows, 0:dh] / acc_ref[1, rows, dh:]
        o = (gates[:, 3 * r:3 * r + 1] * o_cmp[r] + gates[:, 3 * r + 1:3 * r + 2] * o_sel
             + gates[:, 3 * r + 2:3 * r + 3] * o_win)
        o_ref[:, r * dh:(r + 1) * dh] = o.astype(o_ref.dtype)


def nsa_attention(proj, aux, kvc, rel_bias, *, batch, seq, n_groups):
    qb = NSA_QB
    dh = HEAD_DIM
    grp = NSA_GROUP
    nq = seq // qb
    nc = seq // CMP_STRIDE
    n_slc = seq // SLC_BLOCK
    n_sel = min(SLC_TOPK, n_slc)
    assert n_slc <= dh - 2 and n_sel >= 3
    nrows = grp * qb
    dtab, tt = nsa_bias_tables(rel_bias, seq)
    smt = jnp.asarray(_sel_map_t(seq), BF16)
    c31 = rel_bias[NUM_BUCKETS - 1].astype(F32) * LOG2E
    c31_hi = c31.astype(BF16).astype(F32)
    cst = jnp.stack([c31_hi, c31 - c31_hi])
    kv_base = n_groups * grp
    kv_map = lambda j: (lambda b, g, i, c: (b, kv_base + j * n_groups + g))
    grid_spec = pltpu.PrefetchScalarGridSpec(
        num_scalar_prefetch=1,
        grid=(batch, n_groups, nq),
        in_specs=[
            pl.BlockSpec((qb, grp * dh), lambda b, g, i, c: (b * nq + i, g)),
            pl.BlockSpec((seq, dh), kv_map(0)),
            pl.BlockSpec((seq, dh), kv_map(1)),
            pl.BlockSpec((seq, dh), kv_map(2)),
            pl.BlockSpec((seq, dh), kv_map(3)),
            pl.BlockSpec((None, None, None, nc, dh), lambda b, g, i, c: (0, b, g, 0, 0)),
            pl.BlockSpec((None, None, None, nc, dh), lambda b, g, i, c: (1, b, g, 0, 0)),
            pl.BlockSpec((qb, 128), lambda b, g, i, c: (b * nq + i, 2 * n_groups + g)),
            pl.BlockSpec((grp, 4, qb, qb), lambda b, g, i, c: (g, 0, 0, 0)),
            pl.BlockSpec((grp, tt.shape[1], qb), lambda b, g, i, c: (g, 0, 0)),
            pl.BlockSpec((n_slc, nc), lambda b, g, i, c: (0, 0)),
        ],
        out_specs=pl.BlockSpec((qb, grp * dh), lambda b, g, i, c: (b * nq + i, g)),
        scratch_shapes=[
            pltpu.VMEM((2, seq, 2 * dh), BF16),
            pltpu.VMEM((2, seq, 2 * dh), BF16),
            pltpu.VMEM((nrows, 2 * dh), BF16),
            pltpu.VMEM((dh, nc), BF16),
            pltpu.VMEM((NSA_STEPS, nrows, qb), BF16),
            pltpu.VMEM((2, nrows, dh), F32),
            pltpu.VMEM((NSA_STEPS, nrows, dh), F32),
            pltpu.VMEM((2, nrows, 2 * dh), F32),
        ],
    )
    return pl.pallas_call(
        functools.partial(_nsa_attn_kernel, n_sel=n_sel),
        grid_spec=grid_spec,
        out_shape=jax.ShapeDtypeStruct((batch * seq, n_groups * grp * dh), BF16),
        compiler_params=_cparams(("parallel", "parallel", "arbitrary")),
        name="nsa_attention",
    )(cst, proj, proj, proj, proj, proj, kvc, kvc, aux, dtab, tt, smt)


def _gdn_in_weights(w_in, n_heads):
    hp = GDN_HEADS_PER_STEP
    wd = n_heads * HEAD_DIM
    ng = n_heads // hp
    main = jnp.concatenate([w_in[:, :4 * wd], w_in[:, 4 * wd + 2 * n_heads:]], axis=1)
    wb = w_in[:, 4 * wd:4 * wd + n_heads].reshape(-1, ng, hp)
    wa = w_in[:, 4 * wd + n_heads:4 * wd + 2 * n_heads].reshape(-1, ng, hp)
    gate = jnp.concatenate([wb, wa, jnp.zeros((w_in.shape[0], ng, 128 - 2 * hp), w_in.dtype)], axis=2)
    return main.astype(BF16), gate.reshape(w_in.shape[0], ng * 128).astype(BF16)


def _nsa_in_weights(w_in, n_heads):
    grp = NSA_GROUP
    ng = n_heads // grp
    qw = n_heads * HEAD_DIM
    kvw = ng * HEAD_DIM
    main = jnp.concatenate([w_in[:, :qw] * (HEAD_DIM ** -0.5 * LOG2E), w_in[:, qw + 2 * kvw:qw + 6 * kvw],
                            w_in[:, qw + 6 * kvw + 3 * n_heads:]], axis=1)
    cmp_w = w_in[:, qw:qw + 2 * kvw]
    wg = w_in[:, qw + 6 * kvw:qw + 6 * kvw + 3 * n_heads].reshape(-1, ng, 3 * grp)
    gate = jnp.concatenate([wg, jnp.zeros((w_in.shape[0], ng, 128 - 3 * grp), w_in.dtype)], axis=2)
    aux = jnp.concatenate([cmp_w, gate.reshape(w_in.shape[0], ng * 128)], axis=1)
    return main.astype(BF16), aux.astype(BF16)


def _pick(n, candidates):
    for c in candidates:
        if n % c == 0:
            return c
    return n


def kernel(x, mem, rel_bias, norm_mix_w, norm_ffn_w, final_norm_w, mem_norm_w, mem_w_kv, w_out, gdn_w_in, gdn_conv_w, gdn_a_log, gdn_dt_bias, gdn_norm_w, nsa_w_in, nsa_cmp_pos_k, nsa_cmp_w1_k, nsa_cmp_w2_k, nsa_cmp_pos_v, nsa_cmp_w1_v, nsa_cmp_w2_v, ffn_w_up, ffn_conv_w, ffn_conv_b, ffn_w_down):
    batch, seq, d_model = x.shape
    depth = norm_mix_w.shape[0]
    n_heads = d_model // HEAD_DIM
    m_tok = mem.shape[1]
    mw = MEM_HEADS * HEAD_DIM
    d_ff = ffn_w_down.shape[1]
    t = batch * seq
    xf = x.reshape(t, d_model)
    memf = mem.reshape(batch * m_tok, d_model)
    tm = _pick(seq, (1024, 512, 256, 128))
    tm_small = _pick(seq, (512, 256, 128))

    for i in range(depth):
        j = i // 2
        kvw = mem_w_kv[i].astype(BF16)
        kv, _ = norm_matmul(memf, mem_norm_w[i], kvw, kvw[:, :128],
                            tm=_pick(batch * m_tok, (512, 256)), tn=_pick(kvw.shape[1], (512, 256)))
        if i % 2 == 0:
            w_main, w_aux = _gdn_in_weights(gdn_w_in[j], n_heads)
            proj, aux = norm_matmul(xf, norm_mix_w[i], w_main, w_aux, tm=tm_small,
                                    tn=_pick(w_main.shape[1], (512, 256)))
            mix = gdn_mixer(proj, aux, gdn_conv_w[j], gdn_a_log[j], gdn_dt_bias[j], gdn_norm_w[j],
                            batch=batch, seq=seq, lblk=_pick(seq, (512, 256, 128, 64)))
            qm_block = 4 * n_heads * HEAD_DIM // mw
        else:
            ng = n_heads // NSA_GROUP
            w_main, w_aux = _nsa_in_weights(nsa_w_in[j], n_heads)
            proj, aux = norm_matmul(xf, norm_mix_w[i], w_main, w_aux, tm=tm_small,
                                    tn=_pick(w_main.shape[1], (512, 256)))
            pos = jnp.stack([nsa_cmp_pos_k[j], nsa_cmp_pos_v[j]]).astype(F32)
            w1 = jnp.stack([nsa_cmp_w1_k[j], nsa_cmp_w1_v[j]]).astype(BF16)
            w2 = jnp.stack([nsa_cmp_w2_k[j], nsa_cmp_w2_v[j]]).astype(BF16)
            kvc = nsa_compress(aux, pos, w1, w2, batch=batch, seq=seq, n_groups=ng)
            mix = nsa_attention(proj, aux, kvc, rel_bias, batch=batch, seq=seq, n_groups=ng)
            qm_block = (n_heads * HEAD_DIM + 4 * ng * HEAD_DIM) // mw
        mo = mem_attention(proj, kv, batch=batch, seq=seq, q_col_block=qm_block, ts=tm)
        xf, act = mix_ffn_up(xf, mix, mo, w_out[i].astype(BF16), norm_ffn_w[i], ffn_w_up[i].astype(BF16),
                             ffn_conv_w[i], ffn_conv_b[i], seq=seq, tm=tm_small,
                             tn=_pick(d_ff, (256, 128)))
        xf = ffn_down(xf, act, ffn_w_down[i].astype(BF16), final_norm_w, tm=tm_small,
                      final_norm=(i == depth - 1))
    return xf.reshape(batch, seq, d_model)
```

```python
import functools
import math

import jax
import jax.numpy as jnp
import numpy as np
from jax import lax
from jax.experimental import pallas as pl
from jax.experimental.pallas import tpu as pltpu

F32 = jnp.float32
BF16 = jnp.bfloat16

HEAD_DIM = 128
GDN_CONV = 4
GDN_CHUNK = 64
GDN_HEADS_PER_STEP = 4
GDN_SUPER = 256
NSA_GROUP = 4
CMP_LEN = 32
CMP_STRIDE = 16
SLC_BLOCK = 64
SLC_TOPK = 16
WINDOW = 512
NSA_QB = 256
NSA_ROWBLK = 128
NSA_STEPS = 5
MEM_HEADS = 4
NUM_BUCKETS = 32
MAX_DISTANCE = 128
FFN_CONV = 3
EPS = 1e-6
LOG2E = math.log2(math.e)
NEG = -1e30
SEL_PENALTY = 32768.0
VMEM_LIMIT = 56 * 1024 * 1024


def _cparams(sem):
    return pltpu.CompilerParams(dimension_semantics=sem, vmem_limit_bytes=VMEM_LIMIT)


def _dot(a, b):
    return jnp.dot(a, b, preferred_element_type=F32)


def _dot_nt(a, b):
    return lax.dot_general(a, b, (((1,), (1,)), ((), ())), preferred_element_type=F32)


def _silu(x):
    return x * (1.0 / (1.0 + jnp.exp(-x)))


def _sigmoid(x):
    return 1.0 / (1.0 + jnp.exp(-x))


def _softplus(x):
    return jnp.maximum(x, 0.0) + jnp.log(1.0 + jnp.exp(-jnp.abs(x)))


def _shift_rows(x, prev, k):
    sub = lax.broadcasted_iota(jnp.int32, (8, x.shape[1]), 0)
    rolled = pltpu.roll(x, k, 0)
    head = jnp.where(sub < k, pltpu.roll(prev, k, 0), rolled[0:8, :])
    return jnp.concatenate([head, rolled[8:, :]], axis=0)


def _split3(x):
    h = x.astype(BF16)
    r = x - h.astype(F32)
    m = r.astype(BF16)
    l = (r - m.astype(F32)).astype(BF16)
    return h, m, l


def _norm_mm_kernel(x_ref, nw_ref, w_ref, waux_ref, o_ref, oaux_ref, *, rs, tn):
    for r0 in range(0, x_ref.shape[0], rs):
        rows = slice(r0, r0 + rs)
        x = x_ref[rows, :]
        ms = jnp.mean(x * x, axis=-1, keepdims=True)
        xn = (x * lax.rsqrt(ms + EPS) * nw_ref[...]).astype(BF16)
        oaux_ref[rows, :] = _dot(xn, waux_ref[...])
        for c0 in range(0, o_ref.shape[1], tn):
            o_ref[rows, c0:c0 + tn] = _dot(xn, w_ref[:, c0:c0 + tn]).astype(o_ref.dtype)


def norm_matmul(x, nw, w, waux, *, tm, tn):
    t, d = x.shape
    n = w.shape[1]
    na = waux.shape[1]
    whole = lambda i: (0, 0)
    return pl.pallas_call(
        functools.partial(_norm_mm_kernel, rs=min(tm, 256), tn=tn),
        grid=(t // tm,),
        in_specs=[
            pl.BlockSpec((tm, d), lambda i: (i, 0)),
            pl.BlockSpec((1, d), whole),
            pl.BlockSpec((d, n), whole),
            pl.BlockSpec((d, na), whole),
        ],
        out_specs=[
            pl.BlockSpec((tm, n), lambda i: (i, 0)),
            pl.BlockSpec((tm, na), lambda i: (i, 0)),
        ],
        out_shape=[jax.ShapeDtypeStruct((t, n), BF16), jax.ShapeDtypeStruct((t, na), F32)],
        compiler_params=_cparams(("parallel",)),
        name="norm_matmul",
    )(x, nw.reshape(1, d), w, waux)


def _mem_attn_kernel(q_ref, kv_ref, o_ref):
    scale = HEAD_DIM ** -0.5
    mw = MEM_HEADS * HEAD_DIM
    for h in range(MEM_HEADS):
        lo = h * HEAD_DIM
        q = q_ref[:, lo:lo + HEAD_DIM]
        k = kv_ref[:, lo:lo + HEAD_DIM]
        v = kv_ref[:, mw + lo:mw + lo + HEAD_DIM]
        s = _dot_nt(q, k) * scale
        m = jnp.max(s, axis=-1, keepdims=True)
        p = jnp.exp(s - m)
        l = jnp.sum(p, axis=-1, keepdims=True)
        o = _dot(p.astype(BF16), v) / l
        o_ref[:, lo:lo + HEAD_DIM] = o.astype(o_ref.dtype)


def mem_attention(proj, kv, *, batch, seq, q_col_block, ts):
    mw = MEM_HEADS * HEAD_DIM
    m_tok = kv.shape[0] // batch
    nt = seq // ts
    return pl.pallas_call(
        _mem_attn_kernel,
        grid=(batch, nt),
        in_specs=[
            pl.BlockSpec((ts, mw), lambda b, i: (b * nt + i, q_col_block)),
            pl.BlockSpec((m_tok, 2 * mw), lambda b, i: (b, 0)),
        ],
        out_specs=pl.BlockSpec((ts, mw), lambda b, i: (b * nt + i, 0)),
        out_shape=jax.ShapeDtypeStruct((batch * seq, mw), BF16),
        compiler_params=_cparams(("parallel", "parallel")),
        name="mem_attention",
    )(proj, kv)


def _mix_ffn_up_kernel(x_ref, a_ref, b_ref, wo_ref, nw_ref, wu_ref, cw_ref, cb_ref, x1_ref, o_ref, gs_ref,
                       *, tiles_per_seq, tn, rs):
    tm = x_ref.shape[0]
    dff = o_ref.shape[1]
    ka = a_ref.shape[1]
    first = pl.program_id(0) % tiles_per_seq == 0

    for r0 in range(0, tm, rs):
        rows = slice(r0, r0 + rs)
        x = (x_ref[rows, :] + _dot(a_ref[rows, :], wo_ref[0:ka, :])
             + _dot(b_ref[rows, :], wo_ref[ka:, :]))
        x1_ref[rows, :] = x
        ms = jnp.mean(x * x, axis=-1, keepdims=True)
        xn = (x * lax.rsqrt(ms + EPS) * nw_ref[...]).astype(BF16)
        for j in range(dff // tn):
            cols = slice(j * tn, (j + 1) * tn)
            g = _dot(xn, wu_ref[:, cols])
            v = _dot(xn, wu_ref[:, dff + j * tn:dff + (j + 1) * tn])
            prev = gs_ref[:, cols]
            if r0 == 0:
                prev = jnp.where(first, 0.0, prev)
            gs_ref[:, cols] = g[rs - 8:, :]
            conv = (_shift_rows(g, prev, 2) * cw_ref[0:1, cols] + _shift_rows(g, prev, 1) * cw_ref[1:2, cols]
                    + g * cw_ref[2:3, cols] + cb_ref[:, cols])
            o_ref[rows, cols] = (_silu(conv) * v).astype(o_ref.dtype)


def mix_ffn_up(x, a, b, wo, nw, wu, cw, cb, *, seq, tm, tn):
    t, d = x.shape
    ka, kb = a.shape[1], b.shape[1]
    dff = wu.shape[1] // 2
    kern = functools.partial(_mix_ffn_up_kernel, tiles_per_seq=seq // tm, tn=tn, rs=min(tm, 256))
    whole = lambda i: (0, 0)
    row = lambda i: (i, 0)
    return pl.pallas_call(
        kern,
        grid=(t // tm,),
        in_specs=[
            pl.BlockSpec((tm, d), row),
            pl.BlockSpec((tm, ka), row),
            pl.BlockSpec((tm, kb), row),
            pl.BlockSpec((ka + kb, d), whole),
            pl.BlockSpec((1, d), whole),
            pl.BlockSpec((d, 2 * dff), whole),
            pl.BlockSpec((8, dff), whole),
            pl.BlockSpec((1, dff), whole),
        ],
        out_specs=[pl.BlockSpec((tm, d), row), pl.BlockSpec((tm, dff), row)],
        out_shape=[jax.ShapeDtypeStruct((t, d), F32), jax.ShapeDtypeStruct((t, dff), BF16)],
        scratch_shapes=[pltpu.VMEM((8, dff), F32)],
        compiler_params=_cparams(("arbitrary",)),
        name="mix_ffn_up",
    )(x, a, b, wo, nw.reshape(1, d), wu, jnp.pad(cw, ((0, 8 - cw.shape[0]), (0, 0))), cb.reshape(1, dff))


def _ffn_down_kernel(x_ref, a_ref, w_ref, fw_ref, o_ref, *, final_norm):
    y = x_ref[...] + _dot(a_ref[...], w_ref[...])
    if final_norm:
        ms = jnp.mean(y * y, axis=-1, keepdims=True)
        y = y * lax.rsqrt(ms + EPS) * fw_ref[...]
    o_ref[...] = y


def ffn_down(x, a, w, fw, *, tm, final_norm):
    t, d = x.shape
    k = a.shape[1]
    return pl.pallas_call(
        functools.partial(_ffn_down_kernel, final_norm=final_norm),
        grid=(t // tm,),
        in_specs=[
            pl.BlockSpec((tm, d), lambda i: (i, 0)),
            pl.BlockSpec((tm, k), lambda i: (i, 0)),
            pl.BlockSpec((k, d), lambda i: (0, 0)),
            pl.BlockSpec((1, d), lambda i: (0, 0)),
        ],
        out_specs=pl.BlockSpec((tm, d), lambda i: (i, 0)),
        out_shape=jax.ShapeDtypeStruct((t, d), F32),
        compiler_params=_cparams(("parallel",)),
        name="ffn_down",
    )(x, a, w, fw.reshape(1, d))


def _gdn_kernel(q_ref, k_ref, v_ref, z_ref, gt_ref, cw_ref, arow_ref, dtrow_ref, nw_ref, o_ref,
                qs_ref, ks_ref, vs_ref, state_ref, *, n_chunks):
    hp = GDN_HEADS_PER_STEP
    dh = HEAD_DIM
    c = GDN_CHUNK
    lblk = q_ref.shape[0]
    sb = pl.program_id(2)

    @pl.when(sb == 0)
    def _():
        zero8 = jnp.zeros((8, hp * dh), F32)
        qs_ref[0:8, :] = zero8
        ks_ref[0:8, :] = zero8
        vs_ref[0:8, :] = zero8
        state_ref[...] = jnp.zeros(state_ref.shape, F32)

    for idx, (src, dst) in enumerate(((q_ref, qs_ref), (k_ref, ks_ref), (v_ref, vs_ref))):
        x = src[...].astype(F32)
        prev = dst[0:8, :]
        w = cw_ref[idx]
        y = (_shift_rows(x, prev, 3) * w[0:1, :] + _shift_rows(x, prev, 2) * w[1:2, :]
             + _shift_rows(x, prev, 1) * w[2:3, :] + x * w[3:4, :])
        dst[8:, :] = _silu(y)
        dst[0:8, :] = x[lblk - 8:, :]

    sl = GDN_SUPER
    cps = sl // c
    row = lax.broadcasted_iota(jnp.int32, (sl, sl), 0)
    col = lax.broadcasted_iota(jnp.int32, (sl, sl), 1)
    same = (row // c) == (col // c)
    incl = same & (row >= col)
    strict = same & (row > col)
    tril = jnp.where(incl, 1.0, 0.0).astype(BF16)
    ones_blk = jnp.where(same, 1.0, 0.0).astype(BF16)
    arow = arow_ref[0]
    dtrow = dtrow_ref[0]
    nw = nw_ref[...]
    states = [state_ref[r] for r in range(hp)]
    zeros_c = jnp.zeros((c, dh), BF16)

    for sci in range(lblk // sl):
        r0 = sci * sl
        gt = gt_ref[r0:r0 + sl, :]
        beta_all = _sigmoid(gt)
        g_all = -arow * _softplus(gt + dtrow)
        gh, gm, gl = _split3(g_all)
        gc_all = _dot(tril, gh) + _dot(tril, gm) + _dot(tril, gl)
        gend_all = _dot(ones_blk, gh) + _dot(ones_blk, gm) + _dot(ones_blk, gl)
        gc_t = gc_all.T
        hr = range(hp)
        hs = [slice(r * dh, (r + 1) * dh) for r in hr]
        qh = [qs_ref[r0 + 8:r0 + 8 + sl, hs[r]] for r in hr]
        kh = [ks_ref[r0 + 8:r0 + 8 + sl, hs[r]] for r in hr]
        vh = [vs_ref[r0 + 8:r0 + 8 + sl, hs[r]] for r in hr]
        qn = [qh[r] * lax.rsqrt(jnp.sum(qh[r] * qh[r], axis=-1, keepdims=True) + EPS) * (dh ** -0.5) for r in hr]
        kn = [kh[r] * lax.rsqrt(jnp.sum(kh[r] * kh[r], axis=-1, keepdims=True) + EPS) for r in hr]
        beta = [beta_all[:, r:r + 1] for r in hr]
        gc_col = [gc_all[:, hp + r:hp + r + 1] for r in hr]
        gc_row = [gc_t[hp + r:hp + r + 1, :] for r in hr]
        gend = [gend_all[:, hp + r:hp + r + 1] for r in hr]
        kb = [kn[r] * beta[r] for r in hr]
        kn_b = [kn[r].astype(BF16) for r in hr]
        kk = [_dot_nt(kb[r].astype(BF16), kn_b[r]) for r in hr]
        qk = [_dot_nt(qn[r].astype(BF16), kn_b[r]) for r in hr]
        decay = [jnp.where(incl, jnp.exp(jnp.where(incl, gc_col[r] - gc_row[r], 0.0)), 0.0) for r in hr]
        eg = [jnp.exp(gc_col[r]) for r in hr]
        lmat = [jnp.where(strict, kk[r] * decay[r], 0.0) for r in hr]
        a_in = [jnp.where(incl, qk[r] * decay[r], 0.0).astype(BF16) for r in hr]
        heads = [(jnp.concatenate([vh[r] * beta[r], kb[r] * eg[r]], axis=1), -lmat[r],
                  (qn[r] * eg[r]).astype(BF16), kn[r] * jnp.exp(gend[r] - gc_col[r]), a_in[r],
                  jnp.exp(gend[r])) for r in hr]
        xs = [h[0] for h in heads]
        mps = [h[1] for h in heads]
        for it in range(6):
            for r in range(hp):
                mp_b = mps[r].astype(BF16)
                xs[r] = xs[r] + _dot(mp_b, xs[r].astype(BF16))
                if it < 5:
                    mps[r] = _dot(mp_b, mp_b)
        pre = []
        for r in range(hp):
            x = xs[r]
            _, _, qd_b, kd, a_intra, egend_r = heads[r]
            w_b = x[:, dh:].astype(BF16)
            pre.append((x[:, :dh],
                        [jnp.concatenate([w_b[ci * c:(ci + 1) * c], qd_b[ci * c:(ci + 1) * c]], axis=0)
                         for ci in range(cps)],
                        [kd[ci * c:(ci + 1) * c].T.astype(BF16) for ci in range(cps)],
                        a_intra, egend_r))

        for ci in range(cps):
            rows = slice(ci * c, (ci + 1) * c)
            wqs = [_dot(pre[r][1][ci], states[r].astype(BF16)) for r in hr]
            v_new_b = [(pre[r][0][rows] - wqs[r][:c]).astype(BF16) for r in hr]
            vcat = [jnp.concatenate([zeros_c] * ci + [v_new_b[r]] + [zeros_c] * (cps - 1 - ci), axis=0)
                    for r in hr]
            o_new = [wqs[r][c:] + _dot(pre[r][3][rows], vcat[r]) for r in hr]
            states = [states[r] * pre[r][4][ci * c:ci * c + 1, :] + _dot(pre[r][2][ci], v_new_b[r])
                      for r in hr]
            for r in hr:
                o = o_new[r]
                ms = jnp.mean(o * o, axis=-1, keepdims=True)
                zz = z_ref[r0 + ci * c:r0 + (ci + 1) * c, hs[r]].astype(F32)
                o = o * lax.rsqrt(ms + EPS) * nw * _silu(zz)
                o_ref[r0 + ci * c:r0 + (ci + 1) * c, hs[r]] = o.astype(o_ref.dtype)

    for r in range(hp):
        state_ref[r] = states[r]


def gdn_mixer(proj, gates, conv_w, a_log, dt_bias, norm_w, *, batch, seq, lblk):
    hp = GDN_HEADS_PER_STEP
    dh = HEAD_DIM
    n_heads = a_log.shape[0]
    ng = n_heads // hp
    wd = n_heads * dh
    bw = hp * dh
    nsb = seq // lblk
    arow = jnp.zeros((ng, 1, 128), F32).at[:, 0, hp:2 * hp].set(jnp.exp(a_log.astype(F32)).reshape(ng, hp))
    dtrow = jnp.zeros((ng, 1, 128), F32).at[:, 0, hp:2 * hp].set(dt_bias.astype(F32).reshape(ng, hp))
    cw = conv_w.reshape(GDN_CONV, 3, ng, bw).transpose(1, 2, 0, 3)
    cw = jnp.pad(cw, ((0, 0), (0, 0), (0, 8 - GDN_CONV), (0, 0)))
    kern = functools.partial(_gdn_kernel, n_chunks=lblk // GDN_CHUNK)
    row_map = lambda off: (lambda b, g, s: (b * nsb + s, off + g))
    return pl.pallas_call(
        kern,
        grid=(batch, ng, nsb),
        in_specs=[
            pl.BlockSpec((lblk, bw), row_map(0)),
            pl.BlockSpec((lblk, bw), row_map(ng)),
            pl.BlockSpec((lblk, bw), row_map(2 * ng)),
            pl.BlockSpec((lblk, bw), row_map(3 * ng)),
            pl.BlockSpec((lblk, 128), row_map(0)),
            pl.BlockSpec((3, None, 8, bw), lambda b, g, s: (0, g, 0, 0)),
            pl.BlockSpec((None, 1, 128), lambda b, g, s: (g, 0, 0)),
            pl.BlockSpec((None, 1, 128), lambda b, g, s: (g, 0, 0)),
            pl.BlockSpec((1, dh), lambda b, g, s: (0, 0)),
        ],
        out_specs=pl.BlockSpec((lblk, bw), row_map(0)),
        out_shape=jax.ShapeDtypeStruct((batch * seq, wd), BF16),
        scratch_shapes=[
            pltpu.VMEM((lblk + 8, bw), F32),
            pltpu.VMEM((lblk + 8, bw), F32),
            pltpu.VMEM((lblk + 8, bw), F32),
            pltpu.VMEM((hp, dh, dh), F32),
        ],
        compiler_params=_cparams(("parallel", "parallel", "arbitrary")),
        name="gdn_mixer",
    )(proj, proj, proj, proj, gates, cw, arow, dtrow, norm_w.reshape(1, dh).astype(F32))


def _nsa_compress_kernel(t_ref, pos_ref, w1_ref, w2_ref, o_ref):
    nc = t_ref.shape[0] // CMP_STRIDE
    dh = HEAD_DIM
    acc_a = jnp.zeros((nc, dh), F32)
    acc_b = jnp.zeros((nc, dh), F32)
    for i in range(CMP_STRIDE):
        xi = t_ref[pl.ds(i, nc, stride=CMP_STRIDE), :]
        xa = (xi + pos_ref[i:i + 1, :]).astype(BF16)
        xb = (xi + pos_ref[CMP_STRIDE + i:CMP_STRIDE + i + 1, :]).astype(BF16)
        acc_a = acc_a + _dot(xa, w1_ref[i * dh:(i + 1) * dh, :])
        acc_b = acc_b + _dot(xb, w1_ref[(CMP_STRIDE + i) * dh:(CMP_STRIDE + i + 1) * dh, :])
    h = acc_a + pltpu.roll(acc_b, nc - 1, 0)
    o_ref[...] = _dot(_silu(h).astype(BF16), w2_ref[...]).astype(o_ref.dtype)


def nsa_compress(aux, pos, w1, w2, *, batch, seq, n_groups):
    dh = HEAD_DIM
    nc = seq // CMP_STRIDE
    return pl.pallas_call(
        _nsa_compress_kernel,
        grid=(batch, 2, n_groups),
        in_specs=[
            pl.BlockSpec((seq, dh), lambda b, j, g: (b, j * n_groups + g)),
            pl.BlockSpec((None, CMP_LEN, dh), lambda b, j, g: (j, 0, 0)),
            pl.BlockSpec((None, CMP_LEN * dh, dh), lambda b, j, g: (j, 0, 0)),
            pl.BlockSpec((None, dh, dh), lambda b, j, g: (j, 0, 0)),
        ],
        out_specs=pl.BlockSpec((None, None, None, nc, dh), lambda b, j, g: (j, b, g, 0, 0)),
        out_shape=jax.ShapeDtypeStruct((2, batch, n_groups, nc, dh), BF16),
        compiler_params=_cparams(("parallel", "parallel", "parallel")),
        name="nsa_compress",
    )(aux, pos, w1, w2)


def _t5_bucket_np(n):
    max_exact = NUM_BUCKETS // 2
    nf = np.maximum(n, 1).astype(np.float32)
    logv = np.log(nf / np.float32(max_exact)) / np.float32(math.log(MAX_DISTANCE / max_exact))
    large = max_exact + (logv * np.float32(NUM_BUCKETS - max_exact)).astype(np.int32)
    large = np.minimum(large, NUM_BUCKETS - 1)
    return np.where(n < max_exact, n, large).astype(np.int32)


def _t5_thresholds():
    n = np.arange(0, MAX_DISTANCE + 1, dtype=np.int32)
    b = _t5_bucket_np(n)
    half = NUM_BUCKETS // 2
    return tuple(int(np.min(n[b >= half + k])) for k in range(1, NUM_BUCKETS - half))


def _nsa_bias_kernel(tab_ref, dtab_ref, tt_ref, *, thresholds, nc):
    h = pl.program_id(0)
    qb = NSA_QB
    half = NUM_BUCKETS // 2
    c31 = tab_ref[NUM_BUCKETS - 1, h]

    def lookup(dist):
        n = jnp.maximum(dist, 0)
        big = jnp.full(n.shape, half, jnp.int32)
        for t in thresholds:
            big = big + jnp.where(n >= t, 1, 0)
        bucket = jnp.where(n < half, n, big)
        val = jnp.zeros(n.shape, F32)
        for b in range(NUM_BUCKETS):
            val = jnp.where(bucket == b, tab_ref[b, h], val)
        return val

    q = lax.broadcasted_iota(jnp.int32, (qb, qb), 0)
    kk = lax.broadcasted_iota(jnp.int32, (qb, qb), 1)
    dtab_ref[0] = jnp.where(q >= kk, (lookup(q - kk) - c31) * LOG2E, NEG)
    dtab_ref[1] = (lookup(q - kk + qb) - c31) * LOG2E
    dtab_ref[2] = jnp.where(kk > q, 0.0, NEG)
    dtab_ref[3] = jnp.full((qb, qb), NEG, F32)
    nw = 2 * qb // CMP_STRIDE
    x = lax.broadcasted_iota(jnp.int32, (nw, qb), 0)
    ql = lax.broadcasted_iota(jnp.int32, (nw, qb), 1)
    dist = ql - CMP_STRIDE * x + (qb - CMP_LEN + 1)
    tt_ref[0:nc, :] = jnp.zeros((nc, qb), F32) + c31 * LOG2E
    tt_ref[nc:nc + nw, :] = jnp.where(dist >= 0, lookup(dist) * LOG2E, NEG)
    tt_ref[nc + nw:, :] = jnp.full((nc, qb), NEG, F32)


def nsa_bias_tables(rel_bias, seq):
    qb = NSA_QB
    nc = seq // CMP_STRIDE
    nw = 2 * qb // CMP_STRIDE
    n_heads = rel_bias.shape[1]
    kern = functools.partial(_nsa_bias_kernel, thresholds=_t5_thresholds(), nc=nc)
    return pl.pallas_call(
        kern,
        grid=(n_heads,),
        in_specs=[pl.BlockSpec(memory_space=pltpu.SMEM)],
        out_specs=[
            pl.BlockSpec((None, 4, qb, qb), lambda h: (h, 0, 0, 0)),
            pl.BlockSpec((None, 2 * nc + nw, qb), lambda h: (h, 0, 0)),
        ],
        out_shape=[jax.ShapeDtypeStruct((n_heads, 4, qb, qb), F32),
                   jax.ShapeDtypeStruct((n_heads, 2 * nc + nw, qb), F32)],
        compiler_params=_cparams(("parallel",)),
        name="nsa_bias_tables",
    )(rel_bias.astype(F32))


def _sel_map_t(seq):
    nc = seq // CMP_STRIDE
    n_slc = seq // SLC_BLOCK
    c_start = np.arange(nc, dtype=np.int32) * CMP_STRIDE
    c_end = c_start + CMP_LEN - 1
    s_start = np.arange(n_slc, dtype=np.int32) * SLC_BLOCK
    m = (c_start[None, :] < s_start[:, None] + SLC_BLOCK) & (s_start[:, None] <= c_end[None, :])
    return m.astype(np.float32)


def _nsa_attn_kernel(cst_ref, q_ref, ksl_ref, vsl_ref, kw_ref, vw_ref, kc_ref, vc_ref, gt_ref,
                     dtab_ref, tt_ref, smt_ref, o_ref,
                     kaug_ref, vaug_ref, qaug_ref, vct_ref, p_ref, m_ref, alpha_ref,
                     acc_ref, *, n_sel):
    qb = NSA_QB
    dh = HEAD_DIM
    grp = NSA_GROUP
    nrows = grp * qb
    rblk = NSA_ROWBLK
    g = pl.program_id(1)
    qi = pl.program_id(2)
    seq = ksl_ref.shape[0]
    nc = kc_ref.shape[0]
    n_slc = seq // SLC_BLOCK
    blocks_per_chunk = qb // SLC_BLOCK

    @pl.when(qi == 0)
    def _():
        lane = lax.broadcasted_iota(jnp.int32, (seq, dh), 1)
        rb = lax.broadcasted_iota(jnp.int32, (seq, dh), 0) // SLC_BLOCK
        ones_tail = jnp.where(lane >= dh - 2, 1.0, 0.0)
        kaug_ref[0, :, 0:dh] = ksl_ref[...]
        kaug_ref[0, :, dh:] = (jnp.where(rb == lane, -SEL_PENALTY, 0.0) + ones_tail).astype(BF16)
        kaug_ref[1, :, 0:dh] = kw_ref[...]
        kaug_ref[1, :, dh:] = ones_tail.astype(BF16)
        ones = jnp.ones((seq, dh), BF16)
        vaug_ref[0, :, 0:dh] = vsl_ref[...]
        vaug_ref[0, :, dh:] = ones
        vaug_ref[1, :, 0:dh] = vw_ref[...]
        vaug_ref[1, :, dh:] = ones
        vct_ref[...] = vc_ref[...].astype(F32).T.astype(BF16)

    for r in range(grp):
        qaug_ref[r * qb:(r + 1) * qb, 0:dh] = q_ref[:, r * dh:(r + 1) * dh]
    q4 = qaug_ref[:, 0:dh]

    st = _dot_nt(kc_ref[...], q4)
    start = pl.multiple_of(nc + CMP_STRIDE - (qb // CMP_STRIDE) * qi, CMP_STRIDE)
    gr = range(grp)
    bias = [tt_ref[r, pl.ds(start, nc), :] for r in gr]
    sc = [st[:, r * qb:(r + 1) * qb] + bias[r] for r in gr]
    mx = [jnp.max(sc[r], axis=0, keepdims=True) for r in gr]
    pc = [jnp.where(bias[r] > 0.5 * NEG, jnp.exp2(sc[r] - mx[r]), 0.0) for r in gr]
    den = [jnp.maximum(jnp.sum(pc[r], axis=0, keepdims=True), 1e-30) for r in gr]
    pc = [pc[r] / den[r] for r in gr]
    psum = pc[0]
    for r in range(1, grp):
        psum = psum + pc[r]
    o_cmp = [_dot(vct_ref[...], pc[r].astype(BF16)) for r in gr]
    o_cmp = [o_cmp[r].T for r in gr]

    ph = psum.astype(BF16)
    pl_ = (psum - ph.astype(F32)).astype(BF16)
    smt = smt_ref[...]
    imp = _dot(smt, ph) + _dot(smt, pl_)
    jb = lax.broadcasted_iota(jnp.int32, (n_slc, qb), 0)
    tb = qi * blocks_per_chunk + lax.broadcasted_iota(jnp.int32, (n_slc, qb), 1) // SLC_BLOCK
    forced = (jb == 0) | (jb == tb) | (jb == tb - 1)
    score = jnp.where(jb <= tb, jnp.where(forced, jnp.inf, imp), -jnp.inf)
    jbf = jb.astype(F32)
    taken = forced & (jb <= tb)
    notsel = jnp.where(taken, 0.0, 1.0)
    score = jnp.where(taken, -jnp.inf, score)
    for _ in range(max(n_sel - 3, 0)):
        top = jnp.max(score, axis=0, keepdims=True)
        first = jnp.min(jnp.where(score == top, jbf, float(n_slc)), axis=0, keepdims=True)
        pick = (jbf == first) & (top > -jnp.inf)
        notsel = jnp.where(pick, 0.0, notsel)
        score = jnp.where(pick, -jnp.inf, score)
    notsel = jnp.concatenate([notsel, jnp.zeros((dh - n_slc, qb), F32)], axis=0).T
    lane1 = lax.broadcasted_iota(jnp.int32, (1, dh), 1)
    for r in range(grp):
        h = g * grp + r
        cvec = jnp.where(lane1 == dh - 2, cst_ref[0, h], jnp.where(lane1 == dh - 1, cst_ref[1, h], 0.0))
        qaug_ref[r * qb:(r + 1) * qb, dh:] = (notsel + cvec).astype(BF16)

    m_ref[...] = jnp.full(m_ref.shape, NEG, F32)
    acc_ref[...] = jnp.zeros(acc_ref.shape, F32)

    def flash_steps(steps):
        svals = []
        for n, (_, branch, kchunk, _) in enumerate(steps):
            k0 = pl.multiple_of(kchunk * qb, qb)
            svals.append(_dot_nt(qaug_ref[...], kaug_ref[branch, pl.ds(k0, qb), :]))
        for rbi in range(nrows // rblk):
            rows = slice(rbi * rblk, (rbi + 1) * rblk)
            for n, (a, branch, kchunk, tile_idx) in enumerate(steps):
                s0 = svals[n][rows, 0:dh]
                s1 = svals[n][rows, dh:]
                if tile_idx is not None:
                    head = (rbi * rblk) // qb
                    t0 = (rbi * rblk) % qb
                    s0 = s0 + dtab_ref[head, tile_idx, t0:t0 + rblk, 0:dh]
                    s1 = s1 + dtab_ref[head, tile_idx, t0:t0 + rblk, dh:]
                m_old = m_ref[a, rows, :]
                m_new = jnp.maximum(m_old, jnp.max(jnp.maximum(s0, s1), axis=-1, keepdims=True))
                alpha_ref[n, rows, :] = jnp.exp2(m_old - m_new)
                p_ref[n, rows, 0:dh] = jnp.exp2(s0 - m_new).astype(BF16)
                p_ref[n, rows, dh:] = jnp.exp2(s1 - m_new).astype(BF16)
                m_ref[a, rows, :] = m_new
        pvs = []
        for n, (_, branch, kchunk, _) in enumerate(steps):
            k0 = pl.multiple_of(kchunk * qb, qb)
            pvs.append(_dot(p_ref[n], vaug_ref[branch, pl.ds(k0, qb), :]))
        for n, (a, _, _, _) in enumerate(steps):
            alpha = alpha_ref[n]
            acc_ref[a, :, 0:dh] = acc_ref[a, :, 0:dh] * alpha + pvs[n][:, 0:dh]
            acc_ref[a, :, dh:] = acc_ref[a, :, dh:] * alpha + pvs[n][:, dh:]

    n_far = jnp.maximum(qi - 1, 0)

    def pair_body(i, carry):
        flash_steps([(0, 0, 2 * i, None), (0, 0, 2 * i + 1, None)])
        return carry

    lax.fori_loop(0, n_far // 2, pair_body, 0)

    @pl.when(n_far % 2 == 1)
    def _():
        flash_steps([(0, 0, n_far - 1, None)])

    def near(kchunk, tile):
        return jnp.maximum(kchunk, 0), jnp.where(kchunk >= 0, tile, 3)

    k_s1, t_s1 = near(qi - 1, 1)
    k_w2, t_w2 = near(qi - 2, 2)
    flash_steps([(0, 0, k_s1, t_s1), (0, 0, qi, 0),
                 (1, 1, k_w2, t_w2), (1, 1, k_s1, t_s1), (1, 1, qi, 0)])

    gates = _sigmoid(gt_ref[...])
    for r in range(grp):
        rows = slice(r * qb, (r + 1) * qb)
        o_sel = acc_ref[0, rows, 0:dh] / acc_ref[0, rows, dh:]
        o_win = acc_ref[1, rows, 0:dh] / acc_ref[1, rows, dh:]
        o = (gates[:, 3 * r:3 * r + 1] * o_cmp[r] + gates[:, 3 * r + 1:3 * r + 2] * o_sel
             + gates[:, 3 * r + 2:3 * r + 3] * o_win)
        o_ref[:, r * dh:(r + 1) * dh] = o.astype(o_ref.dtype)


def nsa_attention(proj, aux, kvc, rel_bias, *, batch, seq, n_groups):
    qb = NSA_QB
    dh = HEAD_DIM
    grp = NSA_GROUP
    nq = seq // qb
    nc = seq // CMP_STRIDE
    n_slc = seq // SLC_BLOCK
    n_sel = min(SLC_TOPK, n_slc)
    assert n_slc <= dh - 2 and n_sel >= 3
    nrows = grp * qb
    dtab, tt = nsa_bias_tables(rel_bias, seq)
    smt = jnp.asarray(_sel_map_t(seq), BF16)
    c31 = rel_bias[NUM_BUCKETS - 1].astype(F32) * LOG2E
    c31_hi = c31.astype(BF16).astype(F32)
    cst = jnp.stack([c31_hi, c31 - c31_hi])
    kv_base = n_groups * grp
    kv_map = lambda j: (lambda b, g, i, c: (b, kv_base + j * n_groups + g))
    grid_spec = pltpu.PrefetchScalarGridSpec(
        num_scalar_prefetch=1,
        grid=(batch, n_groups, nq),
        in_specs=[
            pl.BlockSpec((qb, grp * dh), lambda b, g, i, c: (b * nq + i, g)),
            pl.BlockSpec((seq, dh), kv_map(0)),
            pl.BlockSpec((seq, dh), kv_map(1)),
            pl.BlockSpec((seq, dh), kv_map(2)),
            pl.BlockSpec((seq, dh), kv_map(3)),
            pl.BlockSpec((None, None, None, nc, dh), lambda b, g, i, c: (0, b, g, 0, 0)),
            pl.BlockSpec((None, None, None, nc, dh), lambda b, g, i, c: (1, b, g, 0, 0)),
            pl.BlockSpec((qb, 128), lambda b, g, i, c: (b * nq + i, 2 * n_groups + g)),
            pl.BlockSpec((grp, 4, qb, qb), lambda b, g, i, c: (g, 0, 0, 0)),
            pl.BlockSpec((grp, tt.shape[1], qb), lambda b, g, i, c: (g, 0, 0)),
            pl.BlockSpec((n_slc, nc), lambda b, g, i, c: (0, 0)),
        ],
        out_specs=pl.BlockSpec((qb, grp * dh), lambda b, g, i, c: (b * nq + i, g)),
        scratch_shapes=[
            pltpu.VMEM((2, seq, 2 * dh), BF16),
            pltpu.VMEM((2, seq, 2 * dh), BF16),
            pltpu.VMEM((nrows, 2 * dh), BF16),
            pltpu.VMEM((dh, nc), BF16),
            pltpu.VMEM((NSA_STEPS, nrows, qb), BF16),
            pltpu.VMEM((2, nrows, dh), F32),
            pltpu.VMEM((NSA_STEPS, nrows, dh), F32),
            pltpu.VMEM((2, nrows, 2 * dh), F32),
        ],
    )
    return pl.pallas_call(
        functools.partial(_nsa_attn_kernel, n_sel=n_sel),
        grid_spec=grid_spec,
        out_shape=jax.ShapeDtypeStruct((batch * seq, n_groups * grp * dh), BF16),
        compiler_params=_cparams(("parallel", "parallel", "arbitrary")),
        name="nsa_attention",
    )(cst, proj, proj, proj, proj, proj, kvc, kvc, aux, dtab, tt, smt)


def _gdn_in_weights(w_in, n_heads):
    hp = GDN_HEADS_PER_STEP
    wd = n_heads * HEAD_DIM
    ng = n_heads // hp
    main = jnp.concatenate([w_in[:, :4 * wd], w_in[:, 4 * wd + 2 * n_heads:]], axis=1)
    wb = w_in[:, 4 * wd:4 * wd + n_heads].reshape(-1, ng, hp)
    wa = w_in[:, 4 * wd + n_heads:4 * wd + 2 * n_heads].reshape(-1, ng, hp)
    gate = jnp.concatenate([wb, wa, jnp.zeros((w_in.shape[0], ng, 128 - 2 * hp), w_in.dtype)], axis=2)
    return main.astype(BF16), gate.reshape(w_in.shape[0], ng * 128).astype(BF16)


def _nsa_in_weights(w_in, n_heads):
    grp = NSA_GROUP
    ng = n_heads // grp
    qw = n_heads * HEAD_DIM
    kvw = ng * HEAD_DIM
    main = jnp.concatenate([w_in[:, :qw] * (HEAD_DIM ** -0.5 * LOG2E), w_in[:, qw + 2 * kvw:qw + 6 * kvw],
                            w_in[:, qw + 6 * kvw + 3 * n_heads:]], axis=1)
    cmp_w = w_in[:, qw:qw + 2 * kvw]
    wg = w_in[:, qw + 6 * kvw:qw + 6 * kvw + 3 * n_heads].reshape(-1, ng, 3 * grp)
    gate = jnp.concatenate([wg, jnp.zeros((w_in.shape[0], ng, 128 - 3 * grp), w_in.dtype)], axis=2)
    aux = jnp.concatenate([cmp_w, gate.reshape(w_in.shape[0], ng * 128)], axis=1)
    return main.astype(BF16), aux.astype(BF16)


def _pick(n, candidates):
    for c in candidates:
        if n % c == 0:
            return c
    return n


def kernel(x, mem, rel_bias, norm_mix_w, norm_ffn_w, final_norm_w, mem_norm_w, mem_w_kv, w_out, gdn_w_in, gdn_conv_w, gdn_a_log, gdn_dt_bias, gdn_norm_w, nsa_w_in, nsa_cmp_pos_k, nsa_cmp_w1_k, nsa_cmp_w2_k, nsa_cmp_pos_v, nsa_cmp_w1_v, nsa_cmp_w2_v, ffn_w_up, ffn_conv_w, ffn_conv_b, ffn_w_down):
    batch, seq, d_model = x.shape
    depth = norm_mix_w.shape[0]
    n_heads = d_model // HEAD_DIM
    m_tok = mem.shape[1]
    mw = MEM_HEADS * HEAD_DIM
    d_ff = ffn_w_down.shape[1]
    t = batch * seq
    xf = x.reshape(t, d_model)
    memf = mem.reshape(batch * m_tok, d_model)
    tm = _pick(seq, (1024, 512, 256, 128))
    tm_small = _pick(seq, (512, 256, 128))

    for i in range(depth):
        j = i // 2
        kvw = mem_w_kv[i].astype(BF16)
        kv, _ = norm_matmul(memf, mem_norm_w[i], kvw, kvw[:, :128],
                            tm=_pick(batch * m_tok, (512, 256)), tn=_pick(kvw.shape[1], (512, 256)))
        if i % 2 == 0:
            w_main, w_aux = _gdn_in_weights(gdn_w_in[j], n_heads)
            proj, aux = norm_matmul(xf, norm_mix_w[i], w_main, w_aux, tm=tm_small,
                                    tn=_pick(w_main.shape[1], (512, 256)))
            mix = gdn_mixer(proj, aux, gdn_conv_w[j], gdn_a_log[j], gdn_dt_bias[j], gdn_norm_w[j],
                            batch=batch, seq=seq, lblk=_pick(seq, (512, 256, 128, 64)))
            qm_block = 4 * n_heads * HEAD_DIM // mw
        else:
            ng = n_heads // NSA_GROUP
            w_main, w_aux = _nsa_in_weights(nsa_w_in[j], n_heads)
            proj, aux = norm_matmul(xf, norm_mix_w[i], w_main, w_aux, tm=tm_small,
                                    tn=_pick(w_main.shape[1], (512, 256)))
            pos = jnp.stack([nsa_cmp_pos_k[j], nsa_cmp_pos_v[j]]).astype(F32)
            w1 = jnp.stack([nsa_cmp_w1_k[j], nsa_cmp_w1_v[j]]).astype(BF16)
            w2 = jnp.stack([nsa_cmp_w2_k[j], nsa_cmp_w2_v[j]]).astype(BF16)
            kvc = nsa_compress(aux, pos, w1, w2, batch=batch, seq=seq, n_groups=ng)
            mix = nsa_attention(proj, aux, kvc, rel_bias, batch=batch, seq=seq, n_groups=ng)
            qm_block = (n_heads * HEAD_DIM + 4 * ng * HEAD_DIM) // mw
        mo = mem_attention(proj, kv, batch=batch, seq=seq, q_col_block=qm_block, ts=tm)
        xf, act = mix_ffn_up(xf, mix, mo, w_out[i].astype(BF16), norm_ffn_w[i], ffn_w_up[i].astype(BF16),
                             ffn_conv_w[i], ffn_conv_b[i], seq=seq, tm=tm_small,
                             tn=_pick(d_ff, (256, 128)))
        xf = ffn_down(xf, act, ffn_w_down[i].astype(BF16), final_norm_w, tm=tm_small,
                      final_norm=(i == depth - 1))
    return xf.reshape(batch, seq, d_model)
```

```python
import functools
import math

import jax
import jax.numpy as jnp
import numpy as np
from jax import lax
from jax.experimental import pallas as pl
from jax.experimental.pallas import tpu as pltpu

F32 = jnp.float32
BF16 = jnp.bfloat16

HEAD_DIM = 128
GDN_CONV = 4
GDN_CHUNK = 64
GDN_HEADS_PER_STEP = 4
GDN_SUPER = 256
NSA_GROUP = 4
CMP_LEN = 32
CMP_STRIDE = 16
SLC_BLOCK = 64
SLC_TOPK = 16
WINDOW = 512
NSA_QB = 256
NSA_ROWBLK = 128
NSA_STEPS = 5
MEM_HEADS = 4
NUM_BUCKETS = 32
MAX_DISTANCE = 128
FFN_CONV = 3
EPS = 1e-6
LOG2E = math.log2(math.e)
NEG = -1e30
SEL_PENALTY = 32768.0
VMEM_LIMIT = 56 * 1024 * 1024


def _cparams(sem):
    return pltpu.CompilerParams(dimension_semantics=sem, vmem_limit_bytes=VMEM_LIMIT)


def _dot(a, b):
    return jnp.dot(a, b, preferred_element_type=F32)


def _dot_nt(a, b):
    return lax.dot_general(a, b, (((1,), (1,)), ((), ())), preferred_element_type=F32)


def _silu(x):
    return x * (1.0 / (1.0 + jnp.exp(-x)))


def _sigmoid(x):
    return 1.0 / (1.0 + jnp.exp(-x))


def _softplus(x):
    return jnp.maximum(x, 0.0) + jnp.log(1.0 + jnp.exp(-jnp.abs(x)))


def _shift_rows(x, prev, k):
    sub = lax.broadcasted_iota(jnp.int32, (8, x.shape[1]), 0)
    rolled = pltpu.roll(x, k, 0)
    head = jnp.where(sub < k, pltpu.roll(prev, k, 0), rolled[0:8, :])
    return jnp.concatenate([head, rolled[8:, :]], axis=0)


def _split3(x):
    h = x.astype(BF16)
    r = x - h.astype(F32)
    m = r.astype(BF16)
    l = (r - m.astype(F32)).astype(BF16)
    return h, m, l


def _norm_mm_kernel(x_ref, nw_ref, w_ref, waux_ref, o_ref, oaux_ref, *, rs, tn):
    for r0 in range(0, x_ref.shape[0], rs):
        rows = slice(r0, r0 + rs)
        x = x_ref[rows, :]
        ms = jnp.mean(x * x, axis=-1, keepdims=True)
        xn = (x * lax.rsqrt(ms + EPS) * nw_ref[...]).astype(BF16)
        oaux_ref[rows, :] = _dot(xn, waux_ref[...])
        for c0 in range(0, o_ref.shape[1], tn):
            o_ref[rows, c0:c0 + tn] = _dot(xn, w_ref[:, c0:c0 + tn]).astype(o_ref.dtype)


def norm_matmul(x, nw, w, waux, *, tm, tn):
    t, d = x.shape
    n = w.shape[1]
    na = waux.shape[1]
    whole = lambda i: (0, 0)
    return pl.pallas_call(
        functools.partial(_norm_mm_kernel, rs=min(tm, 256), tn=tn),
        grid=(t // tm,),
        in_specs=[
            pl.BlockSpec((tm, d), lambda i: (i, 0)),
            pl.BlockSpec((1, d), whole),
            pl.BlockSpec((d, n), whole),
            pl.BlockSpec((d, na), whole),
        ],
        out_specs=[
            pl.BlockSpec((tm, n), lambda i: (i, 0)),
            pl.BlockSpec((tm, na), lambda i: (i, 0)),
        ],
        out_shape=[jax.ShapeDtypeStruct((t, n), BF16), jax.ShapeDtypeStruct((t, na), F32)],
        compiler_params=_cparams(("parallel",)),
        name="norm_matmul",
    )(x, nw.reshape(1, d), w, waux)


def _mem_attn_kernel(q_ref, kv_ref, o_ref):
    scale = HEAD_DIM ** -0.5
    mw = MEM_HEADS * HEAD_DIM
    for h in range(MEM_HEADS):
        lo = h * HEAD_DIM
        q = q_ref[:, lo:lo + HEAD_DIM]
        k = kv_ref[:, lo:lo + HEAD_DIM]
        v = kv_ref[:, mw + lo:mw + lo + HEAD_DIM]
        s = _dot_nt(q, k) * scale
        m = jnp.max(s, axis=-1, keepdims=True)
        p = jnp.exp(s - m)
        l = jnp.sum(p, axis=-1, keepdims=True)
        o = _dot(p.astype(BF16), v) / l
        o_ref[:, lo:lo + HEAD_DIM] = o.astype(o_ref.dtype)


def mem_attention(proj, kv, *, batch, seq, q_col_block, ts):
    mw = MEM_HEADS * HEAD_DIM
    m_tok = kv.shape[0] // batch
    nt = seq // ts
    return pl.pallas_call(
        _mem_attn_kernel,
        grid=(batch, nt),
        in_specs=[
            pl.BlockSpec((ts, mw), lambda b, i: (b * nt + i, q_col_block)),
            pl.BlockSpec((m_tok, 2 * mw), lambda b, i: (b, 0)),
        ],
        out_specs=pl.BlockSpec((ts, mw), lambda b, i: (b * nt + i, 0)),
        out_shape=jax.ShapeDtypeStruct((batch * seq, mw), BF16),
        compiler_params=_cparams(("parallel", "parallel")),
        name="mem_attention",
    )(proj, kv)


def _mix_ffn_up_kernel(x_ref, a_ref, b_ref, wo_ref, nw_ref, wu_ref, cw_ref, cb_ref, x1_ref, o_ref, gs_ref,
                       *, tiles_per_seq, tn, rs):
    tm = x_ref.shape[0]
    dff = o_ref.shape[1]
    ka = a_ref.shape[1]
    first = pl.program_id(0) % tiles_per_seq == 0

    for r0 in range(0, tm, rs):
        rows = slice(r0, r0 + rs)
        x = (x_ref[rows, :] + _dot(a_ref[rows, :], wo_ref[0:ka, :])
             + _dot(b_ref[rows, :], wo_ref[ka:, :]))
        x1_ref[rows, :] = x
        ms = jnp.mean(x * x, axis=-1, keepdims=True)
        xn = (x * lax.rsqrt(ms + EPS) * nw_ref[...]).astype(BF16)
        for j in range(dff // tn):
            cols = slice(j * tn, (j + 1) * tn)
            g = _dot(xn, wu_ref[:, cols])
            v = _dot(xn, wu_ref[:, dff + j * tn:dff + (j + 1) * tn])
            prev = gs_ref[:, cols]
            if r0 == 0:
                prev = jnp.where(first, 0.0, prev)
            gs_ref[:, cols] = g[rs - 8:, :]
            conv = (_shift_rows(g, prev, 2) * cw_ref[0:1, cols] + _shift_rows(g, prev, 1) * cw_ref[1:2, cols]
                    + g * cw_ref[2:3, cols] + cb_ref[:, cols])
            o_ref[rows, cols] = (_silu(conv) * v).astype(o_ref.dtype)


def mix_ffn_up(x, a, b, wo, nw, wu, cw, cb, *, seq, tm, tn):
    t, d = x.shape
    ka, kb = a.shape[1], b.shape[1]
    dff = wu.shape[1] // 2
    kern = functools.partial(_mix_ffn_up_kernel, tiles_per_seq=seq // tm, tn=tn, rs=min(tm, 256))
    whole = lambda i: (0, 0)
    row = lambda i: (i, 0)
    return pl.pallas_call(
        kern,
        grid=(t // tm,),
        in_specs=[
            pl.BlockSpec((tm, d), row),
            pl.BlockSpec((tm, ka), row),
            pl.BlockSpec((tm, kb), row),
            pl.BlockSpec((ka + kb, d), whole),
            pl.BlockSpec((1, d), whole),
            pl.BlockSpec((d, 2 * dff), whole),
            pl.BlockSpec((8, dff), whole),
            pl.BlockSpec((1, dff), whole),
        ],
        out_specs=[pl.BlockSpec((tm, d), row), pl.BlockSpec((tm, dff), row)],
        out_shape=[jax.ShapeDtypeStruct((t, d), F32), jax.ShapeDtypeStruct((t, dff), BF16)],
        scratch_shapes=[pltpu.VMEM((8, dff), F32)],
        compiler_params=_cparams(("arbitrary",)),
        name="mix_ffn_up",
    )(x, a, b, wo, nw.reshape(1, d), wu, jnp.pad(cw, ((0, 8 - cw.shape[0]), (0, 0))), cb.reshape(1, dff))


def _ffn_down_kernel(x_ref, a_ref, w_ref, fw_ref, o_ref, *, final_norm):
    y = x_ref[...] + _dot(a_ref[...], w_ref[...])
    if final_norm:
        ms = jnp.mean(y * y, axis=-1, keepdims=True)
        y = y * lax.rsqrt(ms + EPS) * fw_ref[...]
    o_ref[...] = y


def ffn_down(x, a, w, fw, *, tm, final_norm):
    t, d = x.shape
    k = a.shape[1]
    return pl.pallas_call(
        functools.partial(_ffn_down_kernel, final_norm=final_norm),
        grid=(t // tm,),
        in_specs=[
            pl.BlockSpec((tm, d), lambda i: (i, 0)),
            pl.BlockSpec((tm, k), lambda i: (i, 0)),
            pl.BlockSpec((k, d), lambda i: (0, 0)),
            pl.BlockSpec((1, d), lambda i: (0, 0)),
        ],
        out_specs=pl.BlockSpec((tm, d), lambda i: (i, 0)),
        out_shape=jax.ShapeDtypeStruct((t, d), F32),
        compiler_params=_cparams(("parallel",)),
        name="ffn_down",
    )(x, a, w, fw.reshape(1, d))


def _gdn_kernel(q_ref, k_ref, v_ref, z_ref, gt_ref, cw_ref, arow_ref, dtrow_ref, nw_ref, o_ref,
                qs_ref, ks_ref, vs_ref, state_ref, *, n_chunks):
    hp = GDN_HEADS_PER_STEP
    dh = HEAD_DIM
    c = GDN_CHUNK
    lblk = q_ref.shape[0]
    sb = pl.program_id(2)

    @pl.when(sb == 0)
    def _():
        zero8 = jnp.zeros((8, hp * dh), F32)
        qs_ref[0:8, :] = zero8
        ks_ref[0:8, :] = zero8
        vs_ref[0:8, :] = zero8
        state_ref[...] = jnp.zeros(state_ref.shape, F32)

    srcs = (q_ref, k_ref, v_ref)
    dsts = (qs_ref, ks_ref, vs_ref)
    xs = [src[...].astype(F32) for src in srcs]
    prevs = [dst[0:8, :] for dst in dsts]
    ys = [xs[i] * cw_ref[i][3:4, :] for i in range(3)]
    for k in range(1, GDN_CONV):
        ys = [ys[i] + _shift_rows(xs[i], prevs[i], k) * cw_ref[i][3 - k:4 - k, :] for i in range(3)]
    for i in range(3):
        dsts[i][8:, :] = _silu(ys[i])
        dsts[i][0:8, :] = xs[i][lblk - 8:, :]

    sl = GDN_SUPER
    cps = sl // c
    row = lax.broadcasted_iota(jnp.int32, (sl, sl), 0)
    col = lax.broadcasted_iota(jnp.int32, (sl, sl), 1)
    same = (row // c) == (col // c)
    incl = same & (row >= col)
    strict = same & (row > col)
    tril = jnp.where(incl, 1.0, 0.0).astype(BF16)
    ones_blk = jnp.where(same, 1.0, 0.0).astype(BF16)
    arow = arow_ref[0]
    dtrow = dtrow_ref[0]
    nw = nw_ref[...]
    states = [state_ref[r] for r in range(hp)]
    zeros_c = jnp.zeros((c, dh), BF16)

    for sci in range(lblk // sl):
        r0 = sci * sl
        gt = gt_ref[r0:r0 + sl, :]
        beta_all = _sigmoid(gt)
        g_all = -arow * _softplus(gt + dtrow)
        gh, gm, gl = _split3(g_all)
        gc_all = _dot(tril, gh) + _dot(tril, gm) + _dot(tril, gl)
        gend_all = _dot(ones_blk, gh) + _dot(ones_blk, gm) + _dot(ones_blk, gl)
        gc_t = gc_all.T
        hr = range(hp)
        hs = [slice(r * dh, (r + 1) * dh) for r in hr]
        qh = [qs_ref[r0 + 8:r0 + 8 + sl, hs[r]] for r in hr]
        kh = [ks_ref[r0 + 8:r0 + 8 + sl, hs[r]] for r in hr]
        vh = [vs_ref[r0 + 8:r0 + 8 + sl, hs[r]] for r in hr]
        qn = [qh[r] * lax.rsqrt(jnp.sum(qh[r] * qh[r], axis=-1, keepdims=True) + EPS) * (dh ** -0.5) for r in hr]
        kn = [kh[r] * lax.rsqrt(jnp.sum(kh[r] * kh[r], axis=-1, keepdims=True) + EPS) for r in hr]
        beta = [beta_all[:, r:r + 1] for r in hr]
        gc_col = [gc_all[:, hp + r:hp + r + 1] for r in hr]
        gc_row = [gc_t[hp + r:hp + r + 1, :] for r in hr]
        gend = [gend_all[:, hp + r:hp + r + 1] for r in hr]
        kb = [kn[r] * beta[r] for r in hr]
        kn_b = [kn[r].astype(BF16) for r in hr]
        kk = [_dot_nt(kb[r].astype(BF16), kn_b[r]) for r in hr]
        qk = [_dot_nt(qn[r].astype(BF16), kn_b[r]) for r in hr]
        decay = [jnp.where(incl, jnp.exp(jnp.where(incl, gc_col[r] - gc_row[r], 0.0)), 0.0) for r in hr]
        eg = [jnp.exp(gc_col[r]) for r in hr]
        lmat = [jnp.where(strict, kk[r] * decay[r], 0.0) for r in hr]
        a_in = [jnp.where(incl, qk[r] * decay[r], 0.0).astype(BF16) for r in hr]
        heads = [(jnp.concatenate([vh[r] * beta[r], kb[r] * eg[r]], axis=1), -lmat[r],
                  (qn[r] * eg[r]).astype(BF16), kn[r] * jnp.exp(gend[r] - gc_col[r]), a_in[r],
                  jnp.exp(gend[r])) for r in hr]
        xs = [h[0] for h in heads]
        mps = [h[1] for h in heads]
        for it in range(6):
            for r in range(hp):
                mp_b = mps[r].astype(BF16)
                xs[r] = xs[r] + _dot(mp_b, xs[r].astype(BF16))
                if it < 5:
                    mps[r] = _dot(mp_b, mp_b)
        pre = []
        for r in range(hp):
            x = xs[r]
            _, _, qd_b, kd, a_intra, egend_r = heads[r]
            w_b = x[:, dh:].astype(BF16)
            pre.append((x[:, :dh],
                        [jnp.concatenate([w_b[ci * c:(ci + 1) * c], qd_b[ci * c:(ci + 1) * c]], axis=0)
                         for ci in range(cps)],
                        [kd[ci * c:(ci + 1) * c].T.astype(BF16) for ci in range(cps)],
                        a_intra, egend_r))

        for ci in range(cps):
            rows = slice(ci * c, (ci + 1) * c)
            wqs = [_dot(pre[r][1][ci], states[r].astype(BF16)) for r in hr]
            v_new_b = [(pre[r][0][rows] - wqs[r][:c]).astype(BF16) for r in hr]
            vcat = [jnp.concatenate([zeros_c] * ci + [v_new_b[r]] + [zeros_c] * (cps - 1 - ci), axis=0)
                    for r in hr]
            o_new = [wqs[r][c:] + _dot(pre[r][3][rows], vcat[r]) for r in hr]
            states = [states[r] * pre[r][4][ci * c:ci * c + 1, :] + _dot(pre[r][2][ci], v_new_b[r])
                      for r in hr]
            for r in hr:
                o = o_new[r]
                ms = jnp.mean(o * o, axis=-1, keepdims=True)
                zz = z_ref[r0 + ci * c:r0 + (ci + 1) * c, hs[r]].astype(F32)
                o = o * lax.rsqrt(ms + EPS) * nw * _silu(zz)
                o_ref[r0 + ci * c:r0 + (ci + 1) * c, hs[r]] = o.astype(o_ref.dtype)

    for r in range(hp):
        state_ref[r] = states[r]


def gdn_mixer(proj, gates, conv_w, a_log, dt_bias, norm_w, *, batch, seq, lblk):
    hp = GDN_HEADS_PER_STEP
    dh = HEAD_DIM
    n_heads = a_log.shape[0]
    ng = n_heads // hp
    wd = n_heads * dh
    bw = hp * dh
    nsb = seq // lblk
    arow = jnp.zeros((ng, 1, 128), F32).at[:, 0, hp:2 * hp].set(jnp.exp(a_log.astype(F32)).reshape(ng, hp))
    dtrow = jnp.zeros((ng, 1, 128), F32).at[:, 0, hp:2 * hp].set(dt_bias.astype(F32).reshape(ng, hp))
    cw = conv_w.reshape(GDN_CONV, 3, ng, bw).transpose(1, 2, 0, 3)
    cw = jnp.pad(cw, ((0, 0), (0, 0), (0, 8 - GDN_CONV), (0, 0)))
    kern = functools.partial(_gdn_kernel, n_chunks=lblk // GDN_CHUNK)
    row_map = lambda off: (lambda b, g, s: (b * nsb + s, off + g))
    return pl.pallas_call(
        kern,
        grid=(batch, ng, nsb),
        in_specs=[
            pl.BlockSpec((lblk, bw), row_map(0)),
            pl.BlockSpec((lblk, bw), row_map(ng)),
            pl.BlockSpec((lblk, bw), row_map(2 * ng)),
            pl.BlockSpec((lblk, bw), row_map(3 * ng)),
            pl.BlockSpec((lblk, 128), row_map(0)),
            pl.BlockSpec((3, None, 8, bw), lambda b, g, s: (0, g, 0, 0)),
            pl.BlockSpec((None, 1, 128), lambda b, g, s: (g, 0, 0)),
            pl.BlockSpec((None, 1, 128), lambda b, g, s: (g, 0, 0)),
            pl.BlockSpec((1, dh), lambda b, g, s: (0, 0)),
        ],
        out_specs=pl.BlockSpec((lblk, bw), row_map(0)),
        out_shape=jax.ShapeDtypeStruct((batch * seq, wd), BF16),
        scratch_shapes=[
            pltpu.VMEM((lblk + 8, bw), F32),
            pltpu.VMEM((lblk + 8, bw), F32),
            pltpu.VMEM((lblk + 8, bw), F32),
            pltpu.VMEM((hp, dh, dh), F32),
        ],
        compiler_params=_cparams(("parallel", "parallel", "arbitrary")),
        name="gdn_mixer",
    )(proj, proj, proj, proj, gates, cw, arow, dtrow, norm_w.reshape(1, dh).astype(F32))


def _nsa_compress_kernel(t_ref, pos_ref, w1_ref, w2_ref, o_ref):
    nc = t_ref.shape[0] // CMP_STRIDE
    dh = HEAD_DIM
    acc_a = jnp.zeros((nc, dh), F32)
    acc_b = jnp.zeros((nc, dh), F32)
    for i in range(CMP_STRIDE):
        xi = t_ref[pl.ds(i, nc, stride=CMP_STRIDE), :]
        xa = (xi + pos_ref[i:i + 1, :]).astype(BF16)
        xb = (xi + pos_ref[CMP_STRIDE + i:CMP_STRIDE + i + 1, :]).astype(BF16)
        acc_a = acc_a + _dot(xa, w1_ref[i * dh:(i + 1) * dh, :])
        acc_b = acc_b + _dot(xb, w1_ref[(CMP_STRIDE + i) * dh:(CMP_STRIDE + i + 1) * dh, :])
    h = acc_a + pltpu.roll(acc_b, nc - 1, 0)
    o_ref[...] = _dot(_silu(h).astype(BF16), w2_ref[...]).astype(o_ref.dtype)


def nsa_compress(aux, pos, w1, w2, *, batch, seq, n_groups):
    dh = HEAD_DIM
    nc = seq // CMP_STRIDE
    return pl.pallas_call(
        _nsa_compress_kernel,
        grid=(batch, 2, n_groups),
        in_specs=[
            pl.BlockSpec((seq, dh), lambda b, j, g: (b, j * n_groups + g)),
            pl.BlockSpec((None, CMP_LEN, dh), lambda b, j, g: (j, 0, 0)),
            pl.BlockSpec((None, CMP_LEN * dh, dh), lambda b, j, g: (j, 0, 0)),
            pl.BlockSpec((None, dh, dh), lambda b, j, g: (j, 0, 0)),
        ],
        out_specs=pl.BlockSpec((None, None, None, nc, dh), lambda b, j, g: (j, b, g, 0, 0)),
        out_shape=jax.ShapeDtypeStruct((2, batch, n_groups, nc, dh), BF16),
        compiler_params=_cparams(("parallel", "parallel", "parallel")),
        name="nsa_compress",
    )(aux, pos, w1, w2)


def _t5_bucket_np(n):
    max_exact = NUM_BUCKETS // 2
    nf = np.maximum(n, 1).astype(np.float32)
    logv = np.log(nf / np.float32(max_exact)) / np.float32(math.log(MAX_DISTANCE / max_exact))
    large = max_exact + (logv * np.float32(NUM_BUCKETS - max_exact)).astype(np.int32)
    large = np.minimum(large, NUM_BUCKETS - 1)
    return np.where(n < max_exact, n, large).astype(np.int32)


def _t5_thresholds():
    n = np.arange(0, MAX_DISTANCE + 1, dtype=np.int32)
    b = _t5_bucket_np(n)
    half = NUM_BUCKETS // 2
    return tuple(int(np.min(n[b >= half + k])) for k in range(1, NUM_BUCKETS - half))


def _nsa_bias_kernel(tab_ref, dtab_ref, tt_ref, *, thresholds, nc):
    h = pl.program_id(0)
    qb = NSA_QB
    half = NUM_BUCKETS // 2
    c31 = tab_ref[NUM_BUCKETS - 1, h]

    def lookup(dist):
        n = jnp.maximum(dist, 0)
        big = jnp.full(n.shape, half, jnp.int32)
        for t in thresholds:
            big = big + jnp.where(n >= t, 1, 0)
        bucket = jnp.where(n < half, n, big)
        val = jnp.zeros(n.shape, F32)
        for b in range(NUM_BUCKETS):
            val = jnp.where(bucket == b, tab_ref[b, h], val)
        return val

    q = lax.broadcasted_iota(jnp.int32, (qb, qb), 0)
    kk = lax.broadcasted_iota(jnp.int32, (qb, qb), 1)
    dtab_ref[0] = jnp.where(q >= kk, (lookup(q - kk) - c31) * LOG2E, NEG)
    dtab_ref[1] = (lookup(q - kk + qb) - c31) * LOG2E
    dtab_ref[2] = jnp.where(kk > q, 0.0, NEG)
    dtab_ref[3] = jnp.full((qb, qb), NEG, F32)
    nw = 2 * qb // CMP_STRIDE
    x = lax.broadcasted_iota(jnp.int32, (nw, qb), 0)
    ql = lax.broadcasted_iota(jnp.int32, (nw, qb), 1)
    dist = ql - CMP_STRIDE * x + (qb - CMP_LEN + 1)
    tt_ref[0:nc, :] = jnp.zeros((nc, qb), F32) + c31 * LOG2E
    tt_ref[nc:nc + nw, :] = jnp.where(dist >= 0, lookup(dist) * LOG2E, NEG)
    tt_ref[nc + nw:, :] = jnp.full((nc, qb), NEG, F32)


def nsa_bias_tables(rel_bias, seq):
    qb = NSA_QB
    nc = seq // CMP_STRIDE
    nw = 2 * qb // CMP_STRIDE
    n_heads = rel_bias.shape[1]
    kern = functools.partial(_nsa_bias_kernel, thresholds=_t5_thresholds(), nc=nc)
    return pl.pallas_call(
        kern,
        grid=(n_heads,),
        in_specs=[pl.BlockSpec(memory_space=pltpu.SMEM)],
        out_specs=[
            pl.BlockSpec((None, 4, qb, qb), lambda h: (h, 0, 0, 0)),
            pl.BlockSpec((None, 2 * nc + nw, qb), lambda h: (h, 0, 0)),
        ],
        out_shape=[jax.ShapeDtypeStruct((n_heads, 4, qb, qb), F32),
                   jax.ShapeDtypeStruct((n_heads, 2 * nc + nw, qb), F32)],
        compiler_params=_cparams(("parallel",)),
        name="nsa_bias_tables",
    )(rel_bias.astype(F32))


def _sel_map_t(seq):
    nc = seq // CMP_STRIDE
    n_slc = seq // SLC_BLOCK
    c_start = np.arange(nc, dtype=np.int32) * CMP_STRIDE
    c_end = c_start + CMP_LEN - 1
    s_start = np.arange(n_slc, dtype=np.int32) * SLC_BLOCK
    m = (c_start[None, :] < s_start[:, None] + SLC_BLOCK) & (s_start[:, None] <= c_end[None, :])
    return m.astype(np.float32)


def _nsa_attn_kernel(cst_ref, q_ref, ksl_ref, vsl_ref, kw_ref, vw_ref, kc_ref, vc_ref, gt_ref,
                     dtab_ref, tt_ref, smt_ref, o_ref,
                     kaug_ref, vaug_ref, qaug_ref, vct_ref, p_ref, m_ref, alpha_ref,
                     acc_ref, *, n_sel):
    qb = NSA_QB
    dh = HEAD_DIM
    grp = NSA_GROUP
    nrows = grp * qb
    rblk = NSA_ROWBLK
    g = pl.program_id(1)
    qi = pl.program_id(2)
    seq = ksl_ref.shape[0]
    nc = kc_ref.shape[0]
    n_slc = seq // SLC_BLOCK
    blocks_per_chunk = qb // SLC_BLOCK

    @pl.when(qi == 0)
    def _():
        lane = lax.broadcasted_iota(jnp.int32, (seq, dh), 1)
        rb = lax.broadcasted_iota(jnp.int32, (seq, dh), 0) // SLC_BLOCK
        ones_tail = jnp.where(lane >= dh - 2, 1.0, 0.0)
        kaug_ref[0, :, 0:dh] = ksl_ref[...]
        kaug_ref[0, :, dh:] = (jnp.where(rb == lane, -SEL_PENALTY, 0.0) + ones_tail).astype(BF16)
        kaug_ref[1, :, 0:dh] = kw_ref[...]
        kaug_ref[1, :, dh:] = ones_tail.astype(BF16)
        ones = jnp.ones((seq, dh), BF16)
        vaug_ref[0, :, 0:dh] = vsl_ref[...]
        vaug_ref[0, :, dh:] = ones
        vaug_ref[1, :, 0:dh] = vw_ref[...]
        vaug_ref[1, :, dh:] = ones
        vct_ref[...] = vc_ref[...].astype(F32).T.astype(BF16)

    lane1 = lax.broadcasted_iota(jnp.int32, (1, dh), 1)
    cvec = []
    for r in range(grp):
        h = g * grp + r
        cvec.append(jnp.where(lane1 == dh - 2, cst_ref[0, h], jnp.where(lane1 == dh - 1, cst_ref[1, h], 0.0)))
        qaug_ref[r * qb:(r + 1) * qb, 0:dh] = q_ref[:, r * dh:(r + 1) * dh]
        qaug_ref[r * qb:(r + 1) * qb, dh:] = jnp.broadcast_to(cvec[r], (qb, dh)).astype(BF16)
    q4 = qaug_ref[:, 0:dh]

    m_ref[...] = jnp.full(m_ref.shape, NEG, F32)
    acc_ref[...] = jnp.zeros(acc_ref.shape, F32)

    def flash_steps(steps):
        svals = []
        for n, (_, branch, kchunk, _) in enumerate(steps):
            k0 = pl.multiple_of(kchunk * qb, qb)
            svals.append(_dot_nt(qaug_ref[...], kaug_ref[branch, pl.ds(k0, qb), :]))
        for rbi in range(nrows // rblk):
            rows = slice(rbi * rblk, (rbi + 1) * rblk)
            for n, (a, branch, kchunk, tile_idx) in enumerate(steps):
                s0 = svals[n][rows, 0:dh]
                s1 = svals[n][rows, dh:]
                if tile_idx is not None:
                    head = (rbi * rblk) // qb
                    t0 = (rbi * rblk) % qb
                    s0 = s0 + dtab_ref[head, tile_idx, t0:t0 + rblk, 0:dh]
                    s1 = s1 + dtab_ref[head, tile_idx, t0:t0 + rblk, dh:]
                m_old = m_ref[a, rows, :]
                m_new = jnp.maximum(m_old, jnp.max(jnp.maximum(s0, s1), axis=-1, keepdims=True))
                alpha_ref[n, rows, :] = jnp.exp2(m_old - m_new)
                p_ref[n, rows, 0:dh] = jnp.exp2(s0 - m_new).astype(BF16)
                p_ref[n, rows, dh:] = jnp.exp2(s1 - m_new).astype(BF16)
                m_ref[a, rows, :] = m_new
        pvs = []
        for n, (_, branch, kchunk, _) in enumerate(steps):
            k0 = pl.multiple_of(kchunk * qb, qb)
            pvs.append(_dot(p_ref[n], vaug_ref[branch, pl.ds(k0, qb), :]))
        for n, (a, _, _, _) in enumerate(steps):
            alpha = alpha_ref[n]
            acc_ref[a, :, 0:dh] = acc_ref[a, :, 0:dh] * alpha + pvs[n][:, 0:dh]
            acc_ref[a, :, dh:] = acc_ref[a, :, dh:] * alpha + pvs[n][:, dh:]

    def near(kchunk, tile):
        return jnp.maximum(kchunk, 0), jnp.where(kchunk >= 0, tile, 3)

    k_s1, t_s1 = near(qi - 1, 1)
    k_w2, t_w2 = near(qi - 2, 2)
    flash_steps([(1, 1, k_w2, t_w2), (1, 1, k_s1, t_s1), (1, 1, qi, 0)])

    st = _dot_nt(kc_ref[...], q4)
    start = pl.multiple_of(nc + CMP_STRIDE - (qb // CMP_STRIDE) * qi, CMP_STRIDE)
    gr = range(grp)
    bias = [tt_ref[r, pl.ds(start, nc), :] for r in gr]
    sc = [st[:, r * qb:(r + 1) * qb] + bias[r] for r in gr]
    mx = [jnp.max(sc[r], axis=0, keepdims=True) for r in gr]
    pc = [jnp.where(bias[r] > 0.5 * NEG, jnp.exp2(sc[r] - mx[r]), 0.0) for r in gr]
    den = [jnp.maximum(jnp.sum(pc[r], axis=0, keepdims=True), 1e-30) for r in gr]
    pc = [pc[r] / den[r] for r in gr]
    psum = pc[0]
    for r in range(1, grp):
        psum = psum + pc[r]
    o_cmp = [_dot(vct_ref[...], pc[r].astype(BF16)) for r in gr]
    o_cmp = [o_cmp[r].T for r in gr]

    ph = psum.astype(BF16)
    pl_ = (psum - ph.astype(F32)).astype(BF16)
    smt = smt_ref[...]
    imp = _dot(smt, ph) + _dot(smt, pl_)
    jb = lax.broadcasted_iota(jnp.int32, (n_slc, qb), 0)
    tb = qi * blocks_per_chunk + lax.broadcasted_iota(jnp.int32, (n_slc, qb), 1) // SLC_BLOCK
    forced = (jb == 0) | (jb == tb) | (jb == tb - 1)
    score = jnp.where(jb <= tb, jnp.where(forced, jnp.inf, imp), -jnp.inf)
    jbf = jb.astype(F32)
    taken = forced & (jb <= tb)
    notsel = jnp.where(taken, 0.0, 1.0)
    score = jnp.where(taken, -jnp.inf, score)
    for _ in range(max(n_sel - 3, 0)):
        top = jnp.max(score, axis=0, keepdims=True)
        first = jnp.min(jnp.where(score == top, jbf, float(n_slc)), axis=0, keepdims=True)
        pick = (jbf == first) & (top > -jnp.inf)
        notsel = jnp.where(pick, 0.0, notsel)
        score = jnp.where(pick, -jnp.inf, score)
    notsel = jnp.concatenate([notsel, jnp.zeros((dh - n_slc, qb), F32)], axis=0).T
    for r in range(grp):
        qaug_ref[r * qb:(r + 1) * qb, dh:] = (notsel + cvec[r]).astype(BF16)

    n_far = jnp.maximum(qi - 1, 0)

    def pair_body(i, carry):
        flash_steps([(0, 0, 2 * i, None), (0, 0, 2 * i + 1, None)])
        return carry

    lax.fori_loop(0, n_far // 2, pair_body, 0)

    @pl.when(n_far % 2 == 1)
    def _():
        flash_steps([(0, 0, n_far - 1, None)])

    gates = _sigmoid(gt_ref[...])
    gate = [[jnp.broadcast_to(gates[:, 3 * r + br:3 * r + br + 1], (qb, dh)) for br in range(3)]
            for r in gr]
    o_gc = [gate[r][0] * o_cmp[r] for r in gr]

    flash_steps([(0, 0, k_s1, t_s1), (0, 0, qi, 0)])

    o_sel = [acc_ref[0, r * qb:(r + 1) * qb, 0:dh] / acc_ref[0, r * qb:(r + 1) * qb, dh:] for r in gr]
    o_win = [acc_ref[1, r * qb:(r + 1) * qb, 0:dh] / acc_ref[1, r * qb:(r + 1) * qb, dh:] for r in gr]
    for r in gr:
        o = o_gc[r] + gate[r][1] * o_sel[r] + gate[r][2] * o_win[r]
        o_ref[:, r * dh:(r + 1) * dh] = o.astype(o_ref.dtype)


def nsa_attention(proj, aux, kvc, rel_bias, *, batch, seq, n_groups):
    qb = NSA_QB
    dh = HEAD_DIM
    grp = NSA_GROUP
    nq = seq // qb
    nc = seq // CMP_STRIDE
    n_slc = seq // SLC_BLOCK
    n_sel = min(SLC_TOPK, n_slc)
    assert n_slc <= dh - 2 and n_sel >= 3
    nrows = grp * qb
    dtab, tt = nsa_bias_tables(rel_bias, seq)
    smt = jnp.asarray(_sel_map_t(seq), BF16)
    c31 = rel_bias[NUM_BUCKETS - 1].astype(F32) * LOG2E
    c31_hi = c31.astype(BF16).astype(F32)
    cst = jnp.stack([c31_hi, c31 - c31_hi])
    kv_base = n_groups * grp
    kv_map = lambda j: (lambda b, g, i, c: (b, kv_base + j * n_groups + g))
    grid_spec = pltpu.PrefetchScalarGridSpec(
        num_scalar_prefetch=1,
        grid=(batch, n_groups, nq),
        in_specs=[
            pl.BlockSpec((qb, grp * dh), lambda b, g, i, c: (b * nq + i, g)),
            pl.BlockSpec((seq, dh), kv_map(0)),
            pl.BlockSpec((seq, dh), kv_map(1)),
            pl.BlockSpec((seq, dh), kv_map(2)),
            pl.BlockSpec((seq, dh), kv_map(3)),
            pl.BlockSpec((None, None, None, nc, dh), lambda b, g, i, c: (0, b, g, 0, 0)),
            pl.BlockSpec((None, None, None, nc, dh), lambda b, g, i, c: (1, b, g, 0, 0)),
            pl.BlockSpec((qb, 128), lambda b, g, i, c: (b * nq + i, 2 * n_groups + g)),
            pl.BlockSpec((grp, 4, qb, qb), lambda b, g, i, c: (g, 0, 0, 0)),
            pl.BlockSpec((grp, tt.shape[1], qb), lambda b, g, i, c: (g, 0, 0)),
            pl.BlockSpec((n_slc, nc), lambda b, g, i, c: (0, 0)),
        ],
        out_specs=pl.BlockSpec((qb, grp * dh), lambda b, g, i, c: (b * nq + i, g)),
        scratch_shapes=[
            pltpu.VMEM((2, seq, 2 * dh), BF16),
            pltpu.VMEM((2, seq, 2 * dh), BF16),
            pltpu.VMEM((nrows, 2 * dh), BF16),
            pltpu.VMEM((dh, nc), BF16),
            pltpu.VMEM((NSA_STEPS, nrows, qb), BF16),
            pltpu.VMEM((2, nrows, dh), F32),
            pltpu.VMEM((NSA_STEPS, nrows, dh), F32),
            pltpu.VMEM((2, nrows, 2 * dh), F32),
        ],
    )
    return pl.pallas_call(
        functools.partial(_nsa_attn_kernel, n_sel=n_sel),
        grid_spec=grid_spec,
        out_shape=jax.ShapeDtypeStruct((batch * seq, n_groups * grp * dh), BF16),
        compiler_params=_cparams(("parallel", "parallel", "arbitrary")),
        name="nsa_attention",
    )(cst, proj, proj, proj, proj, proj, kvc, kvc, aux, dtab, tt, smt)


def _gdn_in_weights(w_in, n_heads):
    hp = GDN_HEADS_PER_STEP
    wd = n_heads * HEAD_DIM
    ng = n_heads // hp
    main = jnp.concatenate([w_in[:, :4 * wd], w_in[:, 4 * wd + 2 * n_heads:]], axis=1)
    wb = w_in[:, 4 * wd:4 * wd + n_heads].reshape(-1, ng, hp)
    wa = w_in[:, 4 * wd + n_heads:4 * wd + 2 * n_heads].reshape(-1, ng, hp)
    gate = jnp.concatenate([wb, wa, jnp.zeros((w_in.shape[0], ng, 128 - 2 * hp), w_in.dtype)], axis=2)
    return main.astype(BF16), gate.reshape(w_in.shape[0], ng * 128).astype(BF16)


def _nsa_in_weights(w_in, n_heads):
    grp = NSA_GROUP
    ng = n_heads // grp
    qw = n_heads * HEAD_DIM
    kvw = ng * HEAD_DIM
    main = jnp.concatenate([w_in[:, :qw] * (HEAD_DIM ** -0.5 * LOG2E), w_in[:, qw + 2 * kvw:qw + 6 * kvw],
                            w_in[:, qw + 6 * kvw + 3 * n_heads:]], axis=1)
    cmp_w = w_in[:, qw:qw + 2 * kvw]
    wg = w_in[:, qw + 6 * kvw:qw + 6 * kvw + 3 * n_heads].reshape(-1, ng, 3 * grp)
    gate = jnp.concatenate([wg, jnp.zeros((w_in.shape[0], ng, 128 - 3 * grp), w_in.dtype)], axis=2)
    aux = jnp.concatenate([cmp_w, gate.reshape(w_in.shape[0], ng * 128)], axis=1)
    return main.astype(BF16), aux.astype(BF16)


def _pick(n, candidates):
    for c in candidates:
        if n % c == 0:
            return c
    return n


def kernel(x, mem, rel_bias, norm_mix_w, norm_ffn_w, final_norm_w, mem_norm_w, mem_w_kv, w_out, gdn_w_in, gdn_conv_w, gdn_a_log, gdn_dt_bias, gdn_norm_w, nsa_w_in, nsa_cmp_pos_k, nsa_cmp_w1_k, nsa_cmp_w2_k, nsa_cmp_pos_v, nsa_cmp_w1_v, nsa_cmp_w2_v, ffn_w_up, ffn_conv_w, ffn_conv_b, ffn_w_down):
    batch, seq, d_model = x.shape
    depth = norm_mix_w.shape[0]
    n_heads = d_model // HEAD_DIM
    m_tok = mem.shape[1]
    mw = MEM_HEADS * HEAD_DIM
    d_ff = ffn_w_down.shape[1]
    t = batch * seq
    xf = x.reshape(t, d_model)
    memf = mem.reshape(batch * m_tok, d_model)
    tm = _pick(seq, (1024, 512, 256, 128))
    tm_small = _pick(seq, (512, 256, 128))

    for i in range(depth):
        j = i // 2
        kvw = mem_w_kv[i].astype(BF16)
        kv, _ = norm_matmul(memf, mem_norm_w[i], kvw, kvw[:, :128],
                            tm=_pick(batch * m_tok, (512, 256)), tn=_pick(kvw.shape[1], (512, 256)))
        if i % 2 == 0:
            w_main, w_aux = _gdn_in_weights(gdn_w_in[j], n_heads)
            proj, aux = norm_matmul(xf, norm_mix_w[i], w_main, w_aux, tm=tm_small,
                                    tn=_pick(w_main.shape[1], (512, 256)))
            mix = gdn_mixer(proj, aux, gdn_conv_w[j], gdn_a_log[j], gdn_dt_bias[j], gdn_norm_w[j],
                            batch=batch, seq=seq, lblk=_pick(seq, (512, 256, 128, 64)))
            qm_block = 4 * n_heads * HEAD_DIM // mw
        else:
            ng = n_heads // NSA_GROUP
            w_main, w_aux = _nsa_in_weights(nsa_w_in[j], n_heads)
            proj, aux = norm_matmul(xf, norm_mix_w[i], w_main, w_aux, tm=tm_small,
                                    tn=_pick(w_main.shape[1], (512, 256)))
            pos = jnp.stack([nsa_cmp_pos_k[j], nsa_cmp_pos_v[j]]).astype(F32)
            w1 = jnp.stack([nsa_cmp_w1_k[j], nsa_cmp_w1_v[j]]).astype(BF16)
            w2 = jnp.stack([nsa_cmp_w2_k[j], nsa_cmp_w2_v[j]]).astype(BF16)
            kvc = nsa_compress(aux, pos, w1, w2, batch=batch, seq=seq, n_groups=ng)
            mix = nsa_attention(proj, aux, kvc, rel_bias, batch=batch, seq=seq, n_groups=ng)
            qm_block = (n_heads * HEAD_DIM + 4 * ng * HEAD_DIM) // mw
        mo = mem_attention(proj, kv, batch=batch, seq=seq, q_col_block=qm_block, ts=tm)
        xf, act = mix_ffn_up(xf, mix, mo, w_out[i].astype(BF16), norm_ffn_w[i], ffn_w_up[i].astype(BF16),
                             ffn_conv_w[i], ffn_conv_b[i], seq=seq, tm=tm_small,
                             tn=_pick(d_ff, (256, 128)))
        xf = ffn_down(xf, act, ffn_w_down[i].astype(BF16), final_norm_w, tm=tm_small,
                      final_norm=(i == depth - 1))
    return xf.reshape(batch, seq, d_model)
```

```python
import functools
import math

import jax
import jax.numpy as jnp
import numpy as np
from jax import lax
from jax.experimental import pallas as pl
from jax.experimental.pallas import tpu as pltpu

F32 = jnp.float32
BF16 = jnp.bfloat16

HEAD_DIM = 128
GDN_CONV = 4
GDN_CHUNK = 64
GDN_HEADS_PER_STEP = 4
GDN_SUPER = 256
NSA_GROUP = 4
CMP_LEN = 32
CMP_STRIDE = 16
SLC_BLOCK = 64
SLC_TOPK = 16
WINDOW = 512
NSA_QB = 256
NSA_ROWBLK = 128
NSA_STEPS = 5
MEM_HEADS = 4
NUM_BUCKETS = 32
MAX_DISTANCE = 128
FFN_CONV = 3
EPS = 1e-6
LOG2E = math.log2(math.e)
NEG = -1e30
SEL_PENALTY = 32768.0
VMEM_LIMIT = 56 * 1024 * 1024


def _cparams(sem):
    return pltpu.CompilerParams(dimension_semantics=sem, vmem_limit_bytes=VMEM_LIMIT)


def _dot(a, b):
    return jnp.dot(a, b, preferred_element_type=F32)


def _dot_nt(a, b):
    return lax.dot_general(a, b, (((1,), (1,)), ((), ())), preferred_element_type=F32)


def _silu(x):
    return x * (1.0 / (1.0 + jnp.exp(-x)))


def _sigmoid(x):
    return 1.0 / (1.0 + jnp.exp(-x))


def _softplus(x):
    return jnp.maximum(x, 0.0) + jnp.log(1.0 + jnp.exp(-jnp.abs(x)))


def _shift_rows(x, prev, k):
    sub = lax.broadcasted_iota(jnp.int32, (8, x.shape[1]), 0)
    rolled = pltpu.roll(x, k, 0)
    head = jnp.where(sub < k, pltpu.roll(prev, k, 0), rolled[0:8, :])
    return jnp.concatenate([head, rolled[8:, :]], axis=0)


def _split3(x):
    h = x.astype(BF16)
    r = x - h.astype(F32)
    m = r.astype(BF16)
    l = (r - m.astype(F32)).astype(BF16)
    return h, m, l


def _norm_mm_kernel(x_ref, nw_ref, w_ref, waux_ref, o_ref, oaux_ref, *, rs, tn):
    for r0 in range(0, x_ref.shape[0], rs):
        rows = slice(r0, r0 + rs)
        x = x_ref[rows, :]
        ms = jnp.mean(x * x, axis=-1, keepdims=True)
        xn = (x * lax.rsqrt(ms + EPS) * nw_ref[...]).astype(BF16)
        oaux_ref[rows, :] = _dot(xn, waux_ref[...])
        for c0 in range(0, o_ref.shape[1], tn):
            o_ref[rows, c0:c0 + tn] = _dot(xn, w_ref[:, c0:c0 + tn]).astype(o_ref.dtype)


def norm_matmul(x, nw, w, waux, *, tm, tn):
    t, d = x.shape
    n = w.shape[1]
    na = waux.shape[1]
    whole = lambda i: (0, 0)
    return pl.pallas_call(
        functools.partial(_norm_mm_kernel, rs=min(tm, 256), tn=tn),
        grid=(t // tm,),
        in_specs=[
            pl.BlockSpec((tm, d), lambda i: (i, 0)),
            pl.BlockSpec((1, d), whole),
            pl.BlockSpec((d, n), whole),
            pl.BlockSpec((d, na), whole),
        ],
        out_specs=[
            pl.BlockSpec((tm, n), lambda i: (i, 0)),
            pl.BlockSpec((tm, na), lambda i: (i, 0)),
        ],
        out_shape=[jax.ShapeDtypeStruct((t, n), BF16), jax.ShapeDtypeStruct((t, na), F32)],
        compiler_params=_cparams(("parallel",)),
        name="norm_matmul",
    )(x, nw.reshape(1, d), w, waux)


def _mem_attn_kernel(q_ref, kv_ref, o_ref):
    scale = HEAD_DIM ** -0.5
    mw = MEM_HEADS * HEAD_DIM
    for h in range(MEM_HEADS):
        lo = h * HEAD_DIM
        q = q_ref[:, lo:lo + HEAD_DIM]
        k = kv_ref[:, lo:lo + HEAD_DIM]
        v = kv_ref[:, mw + lo:mw + lo + HEAD_DIM]
        s = _dot_nt(q, k) * scale
        m = jnp.max(s, axis=-1, keepdims=True)
        p = jnp.exp(s - m)
        l = jnp.sum(p, axis=-1, keepdims=True)
        o = _dot(p.astype(BF16), v) / l
        o_ref[:, lo:lo + HEAD_DIM] = o.astype(o_ref.dtype)


def mem_attention(proj, kv, *, batch, seq, q_col_block, ts):
    mw = MEM_HEADS * HEAD_DIM
    m_tok = kv.shape[0] // batch
    nt = seq // ts
    return pl.pallas_call(
        _mem_attn_kernel,
        grid=(batch, nt),
        in_specs=[
            pl.BlockSpec((ts, mw), lambda b, i: (b * nt + i, q_col_block)),
            pl.BlockSpec((m_tok, 2 * mw), lambda b, i: (b, 0)),
        ],
        out_specs=pl.BlockSpec((ts, mw), lambda b, i: (b * nt + i, 0)),
        out_shape=jax.ShapeDtypeStruct((batch * seq, mw), BF16),
        compiler_params=_cparams(("parallel", "parallel")),
        name="mem_attention",
    )(proj, kv)


def _mix_ffn_up_kernel(x_ref, a_ref, b_ref, wo_ref, nw_ref, wu_ref, cw_ref, cb_ref, x1_ref, o_ref, gs_ref,
                       *, tiles_per_seq, tn, rs):
    tm = x_ref.shape[0]
    dff = o_ref.shape[1]
    ka = a_ref.shape[1]
    first = pl.program_id(0) % tiles_per_seq == 0

    for r0 in range(0, tm, rs):
        rows = slice(r0, r0 + rs)
        x = (x_ref[rows, :] + _dot(a_ref[rows, :], wo_ref[0:ka, :])
             + _dot(b_ref[rows, :], wo_ref[ka:, :]))
        x1_ref[rows, :] = x
        ms = jnp.mean(x * x, axis=-1, keepdims=True)
        xn = (x * lax.rsqrt(ms + EPS) * nw_ref[...]).astype(BF16)
        for j in range(dff // tn):
            cols = slice(j * tn, (j + 1) * tn)
            g = _dot(xn, wu_ref[:, cols])
            v = _dot(xn, wu_ref[:, dff + j * tn:dff + (j + 1) * tn])
            prev = gs_ref[:, cols]
            if r0 == 0:
                prev = jnp.where(first, 0.0, prev)
            gs_ref[:, cols] = g[rs - 8:, :]
            conv = (_shift_rows(g, prev, 2) * cw_ref[0:1, cols] + _shift_rows(g, prev, 1) * cw_ref[1:2, cols]
                    + g * cw_ref[2:3, cols] + cb_ref[:, cols])
            o_ref[rows, cols] = (_silu(conv) * v).astype(o_ref.dtype)


def mix_ffn_up(x, a, b, wo, nw, wu, cw, cb, *, seq, tm, tn):
    t, d = x.shape
    ka, kb = a.shape[1], b.shape[1]
    dff = wu.shape[1] // 2
    kern = functools.partial(_mix_ffn_up_kernel, tiles_per_seq=seq // tm, tn=tn, rs=min(tm, 256))
    whole = lambda i: (0, 0)
    row = lambda i: (i, 0)
    return pl.pallas_call(
        kern,
        grid=(t // tm,),
        in_specs=[
            pl.BlockSpec((tm, d), row),
            pl.BlockSpec((tm, ka), row),
            pl.BlockSpec((tm, kb), row),
            pl.BlockSpec((ka + kb, d), whole),
            pl.BlockSpec((1, d), whole),
            pl.BlockSpec((d, 2 * dff), whole),
            pl.BlockSpec((8, dff), whole),
            pl.BlockSpec((1, dff), whole),
        ],
        out_specs=[pl.BlockSpec((tm, d), row), pl.BlockSpec((tm, dff), row)],
        out_shape=[jax.ShapeDtypeStruct((t, d), F32), jax.ShapeDtypeStruct((t, dff), BF16)],
        scratch_shapes=[pltpu.VMEM((8, dff), F32)],
        compiler_params=_cparams(("arbitrary",)),
        name="mix_ffn_up",
    )(x, a, b, wo, nw.reshape(1, d), wu, jnp.pad(cw, ((0, 8 - cw.shape[0]), (0, 0))), cb.reshape(1, dff))


def _ffn_down_kernel(x_ref, a_ref, w_ref, fw_ref, o_ref, *, final_norm):
    y = x_ref[...] + _dot(a_ref[...], w_ref[...])
    if final_norm:
        ms = jnp.mean(y * y, axis=-1, keepdims=True)
        y = y * lax.rsqrt(ms + EPS) * fw_ref[...]
    o_ref[...] = y


def ffn_down(x, a, w, fw, *, tm, final_norm):
    t, d = x.shape
    k = a.shape[1]
    return pl.pallas_call(
        functools.partial(_ffn_down_kernel, final_norm=final_norm),
        grid=(t // tm,),
        in_specs=[
            pl.BlockSpec((tm, d), lambda i: (i, 0)),
            pl.BlockSpec((tm, k), lambda i: (i, 0)),
            pl.BlockSpec((k, d), lambda i: (0, 0)),
            pl.BlockSpec((1, d), lambda i: (0, 0)),
        ],
        out_specs=pl.BlockSpec((tm, d), lambda i: (i, 0)),
        out_shape=jax.ShapeDtypeStruct((t, d), F32),
        compiler_params=_cparams(("parallel",)),
        name="ffn_down",
    )(x, a, w, fw.reshape(1, d))


def _gdn_kernel(q_ref, k_ref, v_ref, z_ref, gt_ref, cw_ref, arow_ref, dtrow_ref, nw_ref, o_ref,
                qs_ref, ks_ref, vs_ref, state_ref, *, n_chunks):
    hp = GDN_HEADS_PER_STEP
    dh = HEAD_DIM
    c = GDN_CHUNK
    lblk = q_ref.shape[0]
    sb = pl.program_id(2)

    @pl.when(sb == 0)
    def _():
        zero8 = jnp.zeros((8, hp * dh), F32)
        qs_ref[0:8, :] = zero8
        ks_ref[0:8, :] = zero8
        vs_ref[0:8, :] = zero8
        state_ref[...] = jnp.zeros(state_ref.shape, F32)

    srcs = (q_ref, k_ref, v_ref)
    dsts = (qs_ref, ks_ref, vs_ref)
    xs = [src[...].astype(F32) for src in srcs]
    prevs = [dst[0:8, :] for dst in dsts]
    ys = [xs[i] * cw_ref[i][3:4, :] for i in range(3)]
    for k in range(1, GDN_CONV):
        ys = [ys[i] + _shift_rows(xs[i], prevs[i], k) * cw_ref[i][3 - k:4 - k, :] for i in range(3)]
    for i in range(3):
        dsts[i][8:, :] = _silu(ys[i])
        dsts[i][0:8, :] = xs[i][lblk - 8:, :]

    sl = GDN_SUPER
    cps = sl // c
    row = lax.broadcasted_iota(jnp.int32, (sl, sl), 0)
    col = lax.broadcasted_iota(jnp.int32, (sl, sl), 1)
    same = (row // c) == (col // c)
    incl = same & (row >= col)
    strict = same & (row > col)
    tril = jnp.where(incl, 1.0, 0.0).astype(BF16)
    ones_blk = jnp.where(same, 1.0, 0.0).astype(BF16)
    arow = arow_ref[0]
    dtrow = dtrow_ref[0]
    nw = nw_ref[...]
    states = [state_ref[r] for r in range(hp)]
    zeros_c = jnp.zeros((c, dh), BF16)

    for sci in range(lblk // sl):
        r0 = sci * sl
        gt = gt_ref[r0:r0 + sl, :]
        beta_all = _sigmoid(gt)
        g_all = -arow * _softplus(gt + dtrow)
        gh, gm, gl = _split3(g_all)
        gc_all = _dot(tril, gh) + _dot(tril, gm) + _dot(tril, gl)
        gend_all = _dot(ones_blk, gh) + _dot(ones_blk, gm) + _dot(ones_blk, gl)
        gc_t = gc_all.T
        hr = range(hp)
        hs = [slice(r * dh, (r + 1) * dh) for r in hr]
        qh = [qs_ref[r0 + 8:r0 + 8 + sl, hs[r]] for r in hr]
        kh = [ks_ref[r0 + 8:r0 + 8 + sl, hs[r]] for r in hr]
        vh = [vs_ref[r0 + 8:r0 + 8 + sl, hs[r]] for r in hr]
        qn = [qh[r] * lax.rsqrt(jnp.sum(qh[r] * qh[r], axis=-1, keepdims=True) + EPS) * (dh ** -0.5) for r in hr]
        kn = [kh[r] * lax.rsqrt(jnp.sum(kh[r] * kh[r], axis=-1, keepdims=True) + EPS) for r in hr]
        beta = [beta_all[:, r:r + 1] for r in hr]
        gc_col = [gc_all[:, hp + r:hp + r + 1] for r in hr]
        gc_row = [gc_t[hp + r:hp + r + 1, :] for r in hr]
        gend = [gend_all[:, hp + r:hp + r + 1] for r in hr]
        kb = [kn[r] * beta[r] for r in hr]
        kn_b = [kn[r].astype(BF16) for r in hr]
        kk = [_dot_nt(kb[r].astype(BF16), kn_b[r]) for r in hr]
        qk = [_dot_nt(qn[r].astype(BF16), kn_b[r]) for r in hr]
        decay = [jnp.where(incl, jnp.exp(jnp.where(incl, gc_col[r] - gc_row[r], 0.0)), 0.0) for r in hr]
        eg = [jnp.exp(gc_col[r]) for r in hr]
        lmat = [jnp.where(strict, kk[r] * decay[r], 0.0) for r in hr]
        a_in = [jnp.where(incl, qk[r] * decay[r], 0.0).astype(BF16) for r in hr]
        heads = [(jnp.concatenate([vh[r] * beta[r], kb[r] * eg[r]], axis=1), -lmat[r],
                  (qn[r] * eg[r]).astype(BF16), kn[r] * jnp.exp(gend[r] - gc_col[r]), a_in[r],
                  jnp.exp(gend[r])) for r in hr]
        xs = [h[0] for h in heads]
        mps = [h[1] for h in heads]
        for it in range(6):
            for r in range(hp):
                mp_b = mps[r].astype(BF16)
                xs[r] = xs[r] + _dot(mp_b, xs[r].astype(BF16))
                if it < 5:
                    mps[r] = _dot(mp_b, mp_b)
        pre = []
        for r in range(hp):
            x = xs[r]
            _, _, qd_b, kd, a_intra, egend_r = heads[r]
            w_b = x[:, dh:].astype(BF16)
            pre.append((x[:, :dh],
                        [jnp.concatenate([w_b[ci * c:(ci + 1) * c], qd_b[ci * c:(ci + 1) * c]], axis=0)
                         for ci in range(cps)],
                        [kd[ci * c:(ci + 1) * c].T.astype(BF16) for ci in range(cps)],
                        a_intra, egend_r))

        for ci in range(cps):
            rows = slice(ci * c, (ci + 1) * c)
            wqs = [_dot(pre[r][1][ci], states[r].astype(BF16)) for r in hr]
            v_new_b = [(pre[r][0][rows] - wqs[r][:c]).astype(BF16) for r in hr]
            vcat = [jnp.concatenate([zeros_c] * ci + [v_new_b[r]] + [zeros_c] * (cps - 1 - ci), axis=0)
                    for r in hr]
            o_new = [wqs[r][c:] + _dot(pre[r][3][rows], vcat[r]) for r in hr]
            states = [states[r] * pre[r][4][ci * c:ci * c + 1, :] + _dot(pre[r][2][ci], v_new_b[r])
                      for r in hr]
            for r in hr:
                o = o_new[r]
                ms = jnp.mean(o * o, axis=-1, keepdims=True)
                zz = z_ref[r0 + ci * c:r0 + (ci + 1) * c, hs[r]].astype(F32)
                o = o * lax.rsqrt(ms + EPS) * nw * _silu(zz)
                o_ref[r0 + ci * c:r0 + (ci + 1) * c, hs[r]] = o.astype(o_ref.dtype)

    for r in range(hp):
        state_ref[r] = states[r]


def gdn_mixer(proj, gates, conv_w, a_log, dt_bias, norm_w, *, batch, seq, lblk):
    hp = GDN_HEADS_PER_STEP
    dh = HEAD_DIM
    n_heads = a_log.shape[0]
    ng = n_heads // hp
    wd = n_heads * dh
    bw = hp * dh
    nsb = seq // lblk
    arow = jnp.zeros((ng, 1, 128), F32).at[:, 0, hp:2 * hp].set(jnp.exp(a_log.astype(F32)).reshape(ng, hp))
    dtrow = jnp.zeros((ng, 1, 128), F32).at[:, 0, hp:2 * hp].set(dt_bias.astype(F32).reshape(ng, hp))
    cw = conv_w.reshape(GDN_CONV, 3, ng, bw).transpose(1, 2, 0, 3)
    cw = jnp.pad(cw, ((0, 0), (0, 0), (0, 8 - GDN_CONV), (0, 0)))
    kern = functools.partial(_gdn_kernel, n_chunks=lblk // GDN_CHUNK)
    row_map = lambda off: (lambda b, g, s: (b * nsb + s, off + g))
    return pl.pallas_call(
        kern,
        grid=(batch, ng, nsb),
        in_specs=[
            pl.BlockSpec((lblk, bw), row_map(0)),
            pl.BlockSpec((lblk, bw), row_map(ng)),
            pl.BlockSpec((lblk, bw), row_map(2 * ng)),
            pl.BlockSpec((lblk, bw), row_map(3 * ng)),
            pl.BlockSpec((lblk, 128), row_map(0)),
            pl.BlockSpec((3, None, 8, bw), lambda b, g, s: (0, g, 0, 0)),
            pl.BlockSpec((None, 1, 128), lambda b, g, s: (g, 0, 0)),
            pl.BlockSpec((None, 1, 128), lambda b, g, s: (g, 0, 0)),
            pl.BlockSpec((1, dh), lambda b, g, s: (0, 0)),
        ],
        out_specs=pl.BlockSpec((lblk, bw), row_map(0)),
        out_shape=jax.ShapeDtypeStruct((batch * seq, wd), BF16),
        scratch_shapes=[
            pltpu.VMEM((lblk + 8, bw), F32),
            pltpu.VMEM((lblk + 8, bw), F32),
            pltpu.VMEM((lblk + 8, bw), F32),
            pltpu.VMEM((hp, dh, dh), F32),
        ],
        compiler_params=_cparams(("parallel", "parallel", "arbitrary")),
        name="gdn_mixer",
    )(proj, proj, proj, proj, gates, cw, arow, dtrow, norm_w.reshape(1, dh).astype(F32))


def _nsa_compress_kernel(t_ref, pos_ref, w1_ref, w2_ref, o_ref):
    nc = t_ref.shape[0] // CMP_STRIDE
    dh = HEAD_DIM
    acc_a = jnp.zeros((nc, dh), F32)
    acc_b = jnp.zeros((nc, dh), F32)
    for i in range(CMP_STRIDE):
        xi = t_ref[pl.ds(i, nc, stride=CMP_STRIDE), :]
        xa = (xi + pos_ref[i:i + 1, :]).astype(BF16)
        xb = (xi + pos_ref[CMP_STRIDE + i:CMP_STRIDE + i + 1, :]).astype(BF16)
        acc_a = acc_a + _dot(xa, w1_ref[i * dh:(i + 1) * dh, :])
        acc_b = acc_b + _dot(xb, w1_ref[(CMP_STRIDE + i) * dh:(CMP_STRIDE + i + 1) * dh, :])
    h = acc_a + pltpu.roll(acc_b, nc - 1, 0)
    o_ref[...] = _dot(_silu(h).astype(BF16), w2_ref[...]).astype(o_ref.dtype)


def nsa_compress(aux, pos, w1, w2, *, batch, seq, n_groups):
    dh = HEAD_DIM
    nc = seq // CMP_STRIDE
    return pl.pallas_call(
        _nsa_compress_kernel,
        grid=(batch, 2, n_groups),
        in_specs=[
            pl.BlockSpec((seq, dh), lambda b, j, g: (b, j * n_groups + g)),
            pl.BlockSpec((None, CMP_LEN, dh), lambda b, j, g: (j, 0, 0)),
            pl.BlockSpec((None, CMP_LEN * dh, dh), lambda b, j, g: (j, 0, 0)),
            pl.BlockSpec((None, dh, dh), lambda b, j, g: (j, 0, 0)),
        ],
        out_specs=pl.BlockSpec((None, None, None, nc, dh), lambda b, j, g: (j, b, g, 0, 0)),
        out_shape=jax.ShapeDtypeStruct((2, batch, n_groups, nc, dh), BF16),
        compiler_params=_cparams(("parallel", "parallel", "parallel")),
        name="nsa_compress",
    )(aux, pos, w1, w2)


def _t5_bucket_np(n):
    max_exact = NUM_BUCKETS // 2
    nf = np.maximum(n, 1).astype(np.float32)
    logv = np.log(nf / np.float32(max_exact)) / np.float32(math.log(MAX_DISTANCE / max_exact))
    large = max_exact + (logv * np.float32(NUM_BUCKETS - max_exact)).astype(np.int32)
    large = np.minimum(large, NUM_BUCKETS - 1)
    return np.where(n < max_exact, n, large).astype(np.int32)


def _t5_thresholds():
    n = np.arange(0, MAX_DISTANCE + 1, dtype=np.int32)
    b = _t5_bucket_np(n)
    half = NUM_BUCKETS // 2
    return tuple(int(np.min(n[b >= half + k])) for k in range(1, NUM_BUCKETS - half))


def _nsa_bias_kernel(tab_ref, dtab_ref, tt_ref, *, thresholds, nc):
    h = pl.program_id(0)
    qb = NSA_QB
    half = NUM_BUCKETS // 2
    c31 = tab_ref[NUM_BUCKETS - 1, h]

    def lookup(dist):
        n = jnp.maximum(dist, 0)
        big = jnp.full(n.shape, half, jnp.int32)
        for t in thresholds:
            big = big + jnp.where(n >= t, 1, 0)
        bucket = jnp.where(n < half, n, big)
        val = jnp.zeros(n.shape, F32)
        for b in range(NUM_BUCKETS):
            val = jnp.where(bucket == b, tab_ref[b, h], val)
        return val

    q = lax.broadcasted_iota(jnp.int32, (qb, qb), 0)
    kk = lax.broadcasted_iota(jnp.int32, (qb, qb), 1)
    dtab_ref[0] = jnp.where(q >= kk, (lookup(q - kk) - c31) * LOG2E, NEG)
    dtab_ref[1] = (lookup(q - kk + qb) - c31) * LOG2E
    dtab_ref[2] = jnp.where(kk > q, 0.0, NEG)
    dtab_ref[3] = jnp.full((qb, qb), NEG, F32)
    nw = 2 * qb // CMP_STRIDE
    x = lax.broadcasted_iota(jnp.int32, (nw, qb), 0)
    ql = lax.broadcasted_iota(jnp.int32, (nw, qb), 1)
    dist = ql - CMP_STRIDE * x + (qb - CMP_LEN + 1)
    tt_ref[0:nc, :] = jnp.zeros((nc, qb), F32) + c31 * LOG2E
    tt_ref[nc:nc + nw, :] = jnp.where(dist >= 0, lookup(dist) * LOG2E, NEG)
    tt_ref[nc + nw:, :] = jnp.full((nc, qb), NEG, F32)


def nsa_bias_tables(rel_bias, seq):
    qb = NSA_QB
    nc = seq // CMP_STRIDE
    nw = 2 * qb // CMP_STRIDE
    n_heads = rel_bias.shape[1]
    kern = functools.partial(_nsa_bias_kernel, thresholds=_t5_thresholds(), nc=nc)
    return pl.pallas_call(
        kern,
        grid=(n_heads,),
        in_specs=[pl.BlockSpec(memory_space=pltpu.SMEM)],
        out_specs=[
            pl.BlockSpec((None, 4, qb, qb), lambda h: (h, 0, 0, 0)),
            pl.BlockSpec((None, 2 * nc + nw, qb), lambda h: (h, 0, 0)),
        ],
        out_shape=[jax.ShapeDtypeStruct((n_heads, 4, qb, qb), F32),
                   jax.ShapeDtypeStruct((n_heads, 2 * nc + nw, qb), F32)],
        compiler_params=_cparams(("parallel",)),
        name="nsa_bias_tables",
    )(rel_bias.astype(F32))


def _sel_map_t(seq):
    nc = seq // CMP_STRIDE
    n_slc = seq // SLC_BLOCK
    c_start = np.arange(nc, dtype=np.int32) * CMP_STRIDE
    c_end = c_start + CMP_LEN - 1
    s_start = np.arange(n_slc, dtype=np.int32) * SLC_BLOCK
    m = (c_start[None, :] < s_start[:, None] + SLC_BLOCK) & (s_start[:, None] <= c_end[None, :])
    return m.astype(np.float32)


def _nsa_attn_kernel(cst_ref, q_ref, ksl_ref, vsl_ref, kw_ref, vw_ref, kc_ref, vc_ref, gt_ref,
                     dtab_ref, tt_ref, smt_ref, o_ref,
                     kaug_ref, vaug_ref, qaug_ref, vct_ref, p_ref, m_ref, alpha_ref,
                     acc_ref, *, n_sel):
    qb = NSA_QB
    dh = HEAD_DIM
    grp = NSA_GROUP
    nrows = grp * qb
    rblk = NSA_ROWBLK
    g = pl.program_id(1)
    qi = pl.program_id(2)
    seq = ksl_ref.shape[0]
    nc = kc_ref.shape[0]
    n_slc = seq // SLC_BLOCK
    blocks_per_chunk = qb // SLC_BLOCK

    @pl.when(qi == 0)
    def _():
        lane = lax.broadcasted_iota(jnp.int32, (seq, dh), 1)
        rb = lax.broadcasted_iota(jnp.int32, (seq, dh), 0) // SLC_BLOCK
        ones_tail = jnp.where(lane >= dh - 2, 1.0, 0.0)
        kaug_ref[0, :, 0:dh] = ksl_ref[...]
        kaug_ref[0, :, dh:] = (jnp.where(rb == lane, -SEL_PENALTY, 0.0) + ones_tail).astype(BF16)
        kaug_ref[1, :, 0:dh] = kw_ref[...]
        kaug_ref[1, :, dh:] = ones_tail.astype(BF16)
        ones = jnp.ones((seq, dh), BF16)
        vaug_ref[0, :, 0:dh] = vsl_ref[...]
        vaug_ref[0, :, dh:] = ones
        vaug_ref[1, :, 0:dh] = vw_ref[...]
        vaug_ref[1, :, dh:] = ones
        vct_ref[...] = vc_ref[...].astype(F32).T.astype(BF16)

    lane1 = lax.broadcasted_iota(jnp.int32, (1, dh), 1)
    cvec = []
    for r in range(grp):
        h = g * grp + r
        cvec.append(jnp.where(lane1 == dh - 2, cst_ref[0, h], jnp.where(lane1 == dh - 1, cst_ref[1, h], 0.0)))
        qaug_ref[r * qb:(r + 1) * qb, 0:dh] = q_ref[:, r * dh:(r + 1) * dh]
        qaug_ref[r * qb:(r + 1) * qb, dh:] = jnp.broadcast_to(cvec[r], (qb, dh)).astype(BF16)
    q4 = qaug_ref[:, 0:dh]

    m_ref[...] = jnp.full(m_ref.shape, NEG, F32)
    acc_ref[...] = jnp.zeros(acc_ref.shape, F32)

    def flash_steps(steps):
        svals = []
        for n, (_, branch, kchunk, _) in enumerate(steps):
            k0 = pl.multiple_of(kchunk * qb, qb)
            svals.append(_dot_nt(qaug_ref[...], kaug_ref[branch, pl.ds(k0, qb), :]))
        for rbi in range(nrows // rblk):
            rows = slice(rbi * rblk, (rbi + 1) * rblk)
            for n, (a, branch, kchunk, tile_idx) in enumerate(steps):
                s0 = svals[n][rows, 0:dh]
                s1 = svals[n][rows, dh:]
                if tile_idx is not None:
                    head = (rbi * rblk) // qb
                    t0 = (rbi * rblk) % qb
                    s0 = s0 + dtab_ref[head, tile_idx, t0:t0 + rblk, 0:dh]
                    s1 = s1 + dtab_ref[head, tile_idx, t0:t0 + rblk, dh:]
                m_old = m_ref[a, rows, :]
                m_new = jnp.maximum(m_old, jnp.max(jnp.maximum(s0, s1), axis=-1, keepdims=True))
                alpha_ref[n, rows, :] = jnp.exp2(m_old - m_new)
                p_ref[n, rows, 0:dh] = jnp.exp2(s0 - m_new).astype(BF16)
                p_ref[n, rows, dh:] = jnp.exp2(s1 - m_new).astype(BF16)
                m_ref[a, rows, :] = m_new
        pvs = []
        for n, (_, branch, kchunk, _) in enumerate(steps):
            k0 = pl.multiple_of(kchunk * qb, qb)
            pvs.append(_dot(p_ref[n], vaug_ref[branch, pl.ds(k0, qb), :]))
        for n, (a, _, _, _) in enumerate(steps):
            alpha = alpha_ref[n]
            acc_ref[a, :, 0:dh] = acc_ref[a, :, 0:dh] * alpha + pvs[n][:, 0:dh]
            acc_ref[a, :, dh:] = acc_ref[a, :, dh:] * alpha + pvs[n][:, dh:]

    def near(kchunk, tile):
        return jnp.maximum(kchunk, 0), jnp.where(kchunk >= 0, tile, 3)

    k_s1, t_s1 = near(qi - 1, 1)
    k_w2, t_w2 = near(qi - 2, 2)
    flash_steps([(1, 1, k_w2, t_w2), (1, 1, k_s1, t_s1), (1, 1, qi, 0)])

    st = _dot_nt(kc_ref[...], q4)
    start = pl.multiple_of(nc + CMP_STRIDE - (qb // CMP_STRIDE) * qi, CMP_STRIDE)
    gr = range(grp)
    bias = [tt_ref[r, pl.ds(start, nc), :] for r in gr]
    sc = [st[:, r * qb:(r + 1) * qb] + bias[r] for r in gr]
    mx = [jnp.max(sc[r], axis=0, keepdims=True) for r in gr]
    pc = [jnp.where(bias[r] > 0.5 * NEG, jnp.exp2(sc[r] - mx[r]), 0.0) for r in gr]
    den = [jnp.maximum(jnp.sum(pc[r], axis=0, keepdims=True), 1e-30) for r in gr]
    pc = [pc[r] / den[r] for r in gr]
    psum = pc[0]
    for r in range(1, grp):
        psum = psum + pc[r]
    o_cmp = [_dot(vct_ref[...], pc[r].astype(BF16)) for r in gr]
    o_cmp = [o_cmp[r].T for r in gr]

    ph = psum.astype(BF16)
    pl_ = (psum - ph.astype(F32)).astype(BF16)
    smt = smt_ref[...]
    imp = _dot(smt, ph) + _dot(smt, pl_)
    jb = lax.broadcasted_iota(jnp.int32, (n_slc, qb), 0)
    tb = qi * blocks_per_chunk + lax.broadcasted_iota(jnp.int32, (n_slc, qb), 1) // SLC_BLOCK
    forced = (jb == 0) | (jb == tb) | (jb == tb - 1)
    score = jnp.where(jb <= tb, jnp.where(forced, jnp.inf, imp), -jnp.inf)
    jbf = jb.astype(F32)
    taken = forced & (jb <= tb)
    notsel = jnp.where(taken, 0.0, 1.0)
    score = jnp.where(taken, -jnp.inf, score)
    for _ in range(max(n_sel - 3, 0)):
        top = jnp.max(score, axis=0, keepdims=True)
        first = jnp.min(jnp.where(score == top, jbf, float(n_slc)), axis=0, keepdims=True)
        pick = (jbf == first) & (top > -jnp.inf)
        notsel = jnp.where(pick, 0.0, notsel)
        score = jnp.where(pick, -jnp.inf, score)
    notsel = jnp.concatenate([notsel, jnp.zeros((dh - n_slc, qb), F32)], axis=0).T
    for r in range(grp):
        qaug_ref[r * qb:(r + 1) * qb, dh:] = (notsel + cvec[r]).astype(BF16)

    n_far = jnp.maximum(qi - 1, 0)

    def pair_body(i, carry):
        flash_steps([(0, 0, 2 * i, None), (0, 0, 2 * i + 1, None)])
        return carry

    lax.fori_loop(0, n_far // 2, pair_body, 0)

    @pl.when(n_far % 2 == 1)
    def _():
        flash_steps([(0, 0, n_far - 1, None)])

    gates = _sigmoid(gt_ref[...])
    gate = [[jnp.broadcast_to(gates[:, 3 * r + br:3 * r + br + 1], (qb, dh)) for br in range(3)]
            for r in gr]
    o_gc = [gate[r][0] * o_cmp[r] for r in gr]

    flash_steps([(0, 0, k_s1, t_s1), (0, 0, qi, 0)])

    o_sel = [acc_ref[0, r * qb:(r + 1) * qb, 0:dh] / acc_ref[0, r * qb:(r + 1) * qb, dh:] for r in gr]
    o_win = [acc_ref[1, r * qb:(r + 1) * qb, 0:dh] / acc_ref[1, r * qb:(r + 1) * qb, dh:] for r in gr]
    for r in gr:
        o = o_gc[r] + gate[r][1] * o_sel[r] + gate[r][2] * o_win[r]
        o_ref[:, r * dh:(r + 1) * dh] = o.astype(o_ref.dtype)


def nsa_attention(proj, aux, kvc, rel_bias, *, batch, seq, n_groups):
    qb = NSA_QB
    dh = HEAD_DIM
    grp = NSA_GROUP
    nq = seq // qb
    nc = seq // CMP_STRIDE
    n_slc = seq // SLC_BLOCK
    n_sel = min(SLC_TOPK, n_slc)
    assert n_slc <= dh - 2 and n_sel >= 3
    nrows = grp * qb
    dtab, tt = nsa_bias_tables(rel_bias, seq)
    smt = jnp.asarray(_sel_map_t(seq), BF16)
    c31 = rel_bias[NUM_BUCKETS - 1].astype(F32) * LOG2E
    c31_hi = c31.astype(BF16).astype(F32)
    cst = jnp.stack([c31_hi, c31 - c31_hi])
    kv_base = n_groups * grp
    kv_map = lambda j: (lambda b, g, i, c: (b, kv_base + j * n_groups + g))
    grid_spec = pltpu.PrefetchScalarGridSpec(
        num_scalar_prefetch=1,
        grid=(batch, n_groups, nq),
        in_specs=[
            pl.BlockSpec((qb, grp * dh), lambda b, g, i, c: (b * nq + i, g)),
            pl.BlockSpec((seq, dh), kv_map(0)),
            pl.BlockSpec((seq, dh), kv_map(1)),
            pl.BlockSpec((seq, dh), kv_map(2)),
            pl.BlockSpec((seq, dh), kv_map(3)),
            pl.BlockSpec((None, None, None, nc, dh), lambda b, g, i, c: (0, b, g, 0, 0)),
            pl.BlockSpec((None, None, None, nc, dh), lambda b, g, i, c: (1, b, g, 0, 0)),
            pl.BlockSpec((qb, 128), lambda b, g, i, c: (b * nq + i, 2 * n_groups + g)),
            pl.BlockSpec((grp, 4, qb, qb), lambda b, g, i, c: (g, 0, 0, 0)),
            pl.BlockSpec((grp, tt.shape[1], qb), lambda b, g, i, c: (g, 0, 0)),
            pl.BlockSpec((n_slc, nc), lambda b, g, i, c: (0, 0)),
        ],
        out_specs=pl.BlockSpec((qb, grp * dh), lambda b, g, i, c: (b * nq + i, g)),
        scratch_shapes=[
            pltpu.VMEM((2, seq, 2 * dh), BF16),
            pltpu.VMEM((2, seq, 2 * dh), BF16),
            pltpu.VMEM((nrows, 2 * dh), BF16),
            pltpu.VMEM((dh, nc), BF16),
            pltpu.VMEM((NSA_STEPS, nrows, qb), BF16),
            pltpu.VMEM((2, nrows, dh), F32),
            pltpu.VMEM((NSA_STEPS, nrows, dh), F32),
            pltpu.VMEM((2, nrows, 2 * dh), F32),
        ],
    )
    return pl.pallas_call(
        functools.partial(_nsa_attn_kernel, n_sel=n_sel),
        grid_spec=grid_spec,
        out_shape=jax.ShapeDtypeStruct((batch * seq, n_groups * grp * dh), BF16),
        compiler_params=_cparams(("parallel", "parallel", "arbitrary")),
        name="nsa_attention",
    )(cst, proj, proj, proj, proj, proj, kvc, kvc, aux, dtab, tt, smt)


def _gdn_in_weights(w_in, n_heads):
    hp = GDN_HEADS_PER_STEP
    wd = n_heads * HEAD_DIM
    ng = n_heads // hp
    main = jnp.concatenate([w_in[:, :4 * wd], w_in[:, 4 * wd + 2 * n_heads:]], axis=1)
    wb = w_in[:, 4 * wd:4 * wd + n_heads].reshape(-1, ng, hp)
    wa = w_in[:, 4 * wd + n_heads:4 * wd + 2 * n_heads].reshape(-1, ng, hp)
    gate = jnp.concatenate([wb, wa, jnp.zeros((w_in.shape[0], ng, 128 - 2 * hp), w_in.dtype)], axis=2)
    return main.astype(BF16), gate.reshape(w_in.shape[0], ng * 128).astype(BF16)


def _nsa_in_weights(w_in, n_heads):
    grp = NSA_GROUP
    ng = n_heads // grp
    qw = n_heads * HEAD_DIM
    kvw = ng * HEAD_DIM
    main = jnp.concatenate([w_in[:, :qw] * (HEAD_DIM ** -0.5 * LOG2E), w_in[:, qw + 2 * kvw:qw + 6 * kvw],
                            w_in[:, qw + 6 * kvw + 3 * n_heads:]], axis=1)
    cmp_w = w_in[:, qw:qw + 2 * kvw]
    wg = w_in[:, qw + 6 * kvw:qw + 6 * kvw + 3 * n_heads].reshape(-1, ng, 3 * grp)
    gate = jnp.concatenate([wg, jnp.zeros((w_in.shape[0], ng, 128 - 3 * grp), w_in.dtype)], axis=2)
    aux = jnp.concatenate([cmp_w, gate.reshape(w_in.shape[0], ng * 128)], axis=1)
    return main.astype(BF16), aux.astype(BF16)


def _pick(n, candidates):
    for c in candidates:
        if n % c == 0:
            return c
    return n


def kernel(x, mem, rel_bias, norm_mix_w, norm_ffn_w, final_norm_w, mem_norm_w, mem_w_kv, w_out, gdn_w_in, gdn_conv_w, gdn_a_log, gdn_dt_bias, gdn_norm_w, nsa_w_in, nsa_cmp_pos_k, nsa_cmp_w1_k, nsa_cmp_w2_k, nsa_cmp_pos_v, nsa_cmp_w1_v, nsa_cmp_w2_v, ffn_w_up, ffn_conv_w, ffn_conv_b, ffn_w_down):
    batch, seq, d_model = x.shape
    depth = norm_mix_w.shape[0]
    n_heads = d_model // HEAD_DIM
    m_tok = mem.shape[1]
    mw = MEM_HEADS * HEAD_DIM
    d_ff = ffn_w_down.shape[1]
    t = batch * seq
    xf = x.reshape(t, d_model)
    memf = mem.reshape(batch * m_tok, d_model)
    tm = _pick(seq, (1024, 512, 256, 128))
    tm_small = _pick(seq, (512, 256, 128))

    for i in range(depth):
        j = i // 2
        kvw = mem_w_kv[i].astype(BF16)
        kv, _ = norm_matmul(memf, mem_norm_w[i], kvw, kvw[:, :128],
                            tm=_pick(batch * m_tok, (512, 256)), tn=_pick(kvw.shape[1], (512, 256)))
        if i % 2 == 0:
            w_main, w_aux = _gdn_in_weights(gdn_w_in[j], n_heads)
            proj, aux = norm_matmul(xf, norm_mix_w[i], w_main, w_aux, tm=tm_small,
                                    tn=_pick(w_main.shape[1], (512, 256)))
            mix = gdn_mixer(proj, aux, gdn_conv_w[j], gdn_a_log[j], gdn_dt_bias[j], gdn_norm_w[j],
                            batch=batch, seq=seq, lblk=_pick(seq, (1024, 512, 256)))
            qm_block = 4 * n_heads * HEAD_DIM // mw
        else:
            ng = n_heads // NSA_GROUP
            w_main, w_aux = _nsa_in_weights(nsa_w_in[j], n_heads)
            proj, aux = norm_matmul(xf, norm_mix_w[i], w_main, w_aux, tm=tm_small,
                                    tn=_pick(w_main.shape[1], (512, 256)))
            pos = jnp.stack([nsa_cmp_pos_k[j], nsa_cmp_pos_v[j]]).astype(F32)
            w1 = jnp.stack([nsa_cmp_w1_k[j], nsa_cmp_w1_v[j]]).astype(BF16)
            w2 = jnp.stack([nsa_cmp_w2_k[j], nsa_cmp_w2_v[j]]).astype(BF16)
            kvc = nsa_compress(aux, pos, w1, w2, batch=batch, seq=seq, n_groups=ng)
            mix = nsa_attention(proj, aux, kvc, rel_bias, batch=batch, seq=seq, n_groups=ng)
            qm_block = (n_heads * HEAD_DIM + 4 * ng * HEAD_DIM) // mw
        mo = mem_attention(proj, kv, batch=batch, seq=seq, q_col_block=qm_block, ts=tm)
        xf, act = mix_ffn_up(xf, mix, mo, w_out[i].astype(BF16), norm_ffn_w[i], ffn_w_up[i].astype(BF16),
                             ffn_conv_w[i], ffn_conv_b[i], seq=seq, tm=tm_small,
                             tn=_pick(d_ff, (256, 128)))
        xf = ffn_down(xf, act, ffn_w_down[i].astype(BF16), final_norm_w, tm=tm_small,
                      final_norm=(i == depth - 1))
    return xf.reshape(batch, seq, d_model)
```

```python
import functools
import math

import jax
import jax.numpy as jnp
import numpy as np
from jax import lax
from jax.experimental import pallas as pl
from jax.experimental.pallas import tpu as pltpu

F32 = jnp.float32
BF16 = jnp.bfloat16

HEAD_DIM = 128
GDN_CONV = 4
GDN_CHUNK = 64
GDN_HEADS_PER_STEP = 8
GDN_SUPER = 256
NSA_GROUP = 4
CMP_LEN = 32
CMP_STRIDE = 16
SLC_BLOCK = 64
SLC_TOPK = 16
WINDOW = 512
NSA_QB = 256
NSA_ROWBLK = 128
NSA_STEPS = 5
MEM_HEADS = 4
NUM_BUCKETS = 32
MAX_DISTANCE = 128
FFN_CONV = 3
EPS = 1e-6
LOG2E = math.log2(math.e)
NEG = -1e30
SEL_PENALTY = 32768.0
VMEM_LIMIT = 56 * 1024 * 1024


def _cparams(sem):
    return pltpu.CompilerParams(dimension_semantics=sem, vmem_limit_bytes=VMEM_LIMIT)


def _dot(a, b):
    return jnp.dot(a, b, preferred_element_type=F32)


def _dot_nt(a, b):
    return lax.dot_general(a, b, (((1,), (1,)), ((), ())), preferred_element_type=F32)


def _silu(x):
    return x * (1.0 / (1.0 + jnp.exp(-x)))


def _sigmoid(x):
    return 1.0 / (1.0 + jnp.exp(-x))


def _softplus(x):
    return jnp.maximum(x, 0.0) + jnp.log(1.0 + jnp.exp(-jnp.abs(x)))


def _shift_rows(x, prev, k):
    sub = lax.broadcasted_iota(jnp.int32, (8, x.shape[1]), 0)
    rolled = pltpu.roll(x, k, 0)
    head = jnp.where(sub < k, pltpu.roll(prev, k, 0), rolled[0:8, :])
    return jnp.concatenate([head, rolled[8:, :]], axis=0)


def _split3(x):
    h = x.astype(BF16)
    r = x - h.astype(F32)
    m = r.astype(BF16)
    l = (r - m.astype(F32)).astype(BF16)
    return h, m, l


def _norm_mm_kernel(x_ref, nw_ref, w_ref, waux_ref, o_ref, oaux_ref, *, rs, tn):
    for r0 in range(0, x_ref.shape[0], rs):
        rows = slice(r0, r0 + rs)
        x = x_ref[rows, :]
        ms = jnp.mean(x * x, axis=-1, keepdims=True)
        xn = (x * lax.rsqrt(ms + EPS) * nw_ref[...]).astype(BF16)
        oaux_ref[rows, :] = _dot(xn, waux_ref[...])
        for c0 in range(0, o_ref.shape[1], tn):
            o_ref[rows, c0:c0 + tn] = _dot(xn, w_ref[:, c0:c0 + tn]).astype(o_ref.dtype)


def norm_matmul(x, nw, w, waux, *, tm, tn):
    t, d = x.shape
    n = w.shape[1]
    na = waux.shape[1]
    whole = lambda i: (0, 0)
    return pl.pallas_call(
        functools.partial(_norm_mm_kernel, rs=min(tm, 256), tn=tn),
        grid=(t // tm,),
        in_specs=[
            pl.BlockSpec((tm, d), lambda i: (i, 0)),
            pl.BlockSpec((1, d), whole),
            pl.BlockSpec((d, n), whole),
            pl.BlockSpec((d, na), whole),
        ],
        out_specs=[
            pl.BlockSpec((tm, n), lambda i: (i, 0)),
            pl.BlockSpec((tm, na), lambda i: (i, 0)),
        ],
        out_shape=[jax.ShapeDtypeStruct((t, n), BF16), jax.ShapeDtypeStruct((t, na), F32)],
        compiler_params=_cparams(("parallel",)),
        name="norm_matmul",
    )(x, nw.reshape(1, d), w, waux)


def _mem_attn_kernel(q_ref, kv_ref, o_ref):
    scale = HEAD_DIM ** -0.5
    mw = MEM_HEADS * HEAD_DIM
    for h in range(MEM_HEADS):
        lo = h * HEAD_DIM
        q = q_ref[:, lo:lo + HEAD_DIM]
        k = kv_ref[:, lo:lo + HEAD_DIM]
        v = kv_ref[:, mw + lo:mw + lo + HEAD_DIM]
        s = _dot_nt(q, k) * scale
        m = jnp.max(s, axis=-1, keepdims=True)
        p = jnp.exp(s - m)
        l = jnp.sum(p, axis=-1, keepdims=True)
        o = _dot(p.astype(BF16), v) / l
        o_ref[:, lo:lo + HEAD_DIM] = o.astype(o_ref.dtype)


def mem_attention(proj, kv, *, batch, seq, q_col_block, ts):
    mw = MEM_HEADS * HEAD_DIM
    m_tok = kv.shape[0] // batch
    nt = seq // ts
    return pl.pallas_call(
        _mem_attn_kernel,
        grid=(batch, nt),
        in_specs=[
            pl.BlockSpec((ts, mw), lambda b, i: (b * nt + i, q_col_block)),
            pl.BlockSpec((m_tok, 2 * mw), lambda b, i: (b, 0)),
        ],
        out_specs=pl.BlockSpec((ts, mw), lambda b, i: (b * nt + i, 0)),
        out_shape=jax.ShapeDtypeStruct((batch * seq, mw), BF16),
        compiler_params=_cparams(("parallel", "parallel")),
        name="mem_attention",
    )(proj, kv)


def _mix_ffn_up_kernel(x_ref, a_ref, b_ref, wo_ref, nw_ref, wu_ref, cw_ref, cb_ref, x1_ref, o_ref, gs_ref,
                       *, tiles_per_seq, tn, rs):
    tm = x_ref.shape[0]
    dff = o_ref.shape[1]
    ka = a_ref.shape[1]
    first = pl.program_id(0) % tiles_per_seq == 0

    for r0 in range(0, tm, rs):
        rows = slice(r0, r0 + rs)
        x = (x_ref[rows, :] + _dot(a_ref[rows, :], wo_ref[0:ka, :])
             + _dot(b_ref[rows, :], wo_ref[ka:, :]))
        x1_ref[rows, :] = x
        ms = jnp.mean(x * x, axis=-1, keepdims=True)
        xn = (x * lax.rsqrt(ms + EPS) * nw_ref[...]).astype(BF16)
        for j in range(dff // tn):
            cols = slice(j * tn, (j + 1) * tn)
            g = _dot(xn, wu_ref[:, cols])
            v = _dot(xn, wu_ref[:, dff + j * tn:dff + (j + 1) * tn])
            prev = gs_ref[:, cols]
            if r0 == 0:
                prev = jnp.where(first, 0.0, prev)
            gs_ref[:, cols] = g[rs - 8:, :]
            conv = (_shift_rows(g, prev, 2) * cw_ref[0:1, cols] + _shift_rows(g, prev, 1) * cw_ref[1:2, cols]
                    + g * cw_ref[2:3, cols] + cb_ref[:, cols])
            o_ref[rows, cols] = (_silu(conv) * v).astype(o_ref.dtype)


def mix_ffn_up(x, a, b, wo, nw, wu, cw, cb, *, seq, tm, tn):
    t, d = x.shape
    ka, kb = a.shape[1], b.shape[1]
    dff = wu.shape[1] // 2
    kern = functools.partial(_mix_ffn_up_kernel, tiles_per_seq=seq // tm, tn=tn, rs=min(tm, 256))
    whole = lambda i: (0, 0)
    row = lambda i: (i, 0)
    return pl.pallas_call(
        kern,
        grid=(t // tm,),
        in_specs=[
            pl.BlockSpec((tm, d), row),
            pl.BlockSpec((tm, ka), row),
            pl.BlockSpec((tm, kb), row),
            pl.BlockSpec((ka + kb, d), whole),
            pl.BlockSpec((1, d), whole),
            pl.BlockSpec((d, 2 * dff), whole),
            pl.BlockSpec((8, dff), whole),
            pl.BlockSpec((1, dff), whole),
        ],
        out_specs=[pl.BlockSpec((tm, d), row), pl.BlockSpec((tm, dff), row)],
        out_shape=[jax.ShapeDtypeStruct((t, d), F32), jax.ShapeDtypeStruct((t, dff), BF16)],
        scratch_shapes=[pltpu.VMEM((8, dff), F32)],
        compiler_params=_cparams(("arbitrary",)),
        name="mix_ffn_up",
    )(x, a, b, wo, nw.reshape(1, d), wu, jnp.pad(cw, ((0, 8 - cw.shape[0]), (0, 0))), cb.reshape(1, dff))


def _ffn_down_kernel(x_ref, a_ref, w_ref, fw_ref, o_ref, *, final_norm):
    y = x_ref[...] + _dot(a_ref[...], w_ref[...])
    if final_norm:
        ms = jnp.mean(y * y, axis=-1, keepdims=True)
        y = y * lax.rsqrt(ms + EPS) * fw_ref[...]
    o_ref[...] = y


def ffn_down(x, a, w, fw, *, tm, final_norm):
    t, d = x.shape
    k = a.shape[1]
    return pl.pallas_call(
        functools.partial(_ffn_down_kernel, final_norm=final_norm),
        grid=(t // tm,),
        in_specs=[
            pl.BlockSpec((tm, d), lambda i: (i, 0)),
            pl.BlockSpec((tm, k), lambda i: (i, 0)),
            pl.BlockSpec((k, d), lambda i: (0, 0)),
            pl.BlockSpec((1, d), lambda i: (0, 0)),
        ],
        out_specs=pl.BlockSpec((tm, d), lambda i: (i, 0)),
        out_shape=jax.ShapeDtypeStruct((t, d), F32),
        compiler_params=_cparams(("parallel",)),
        name="ffn_down",
    )(x, a, w, fw.reshape(1, d))


def _gdn_kernel(q_ref, k_ref, v_ref, z_ref, gt_ref, cw_ref, arow_ref, dtrow_ref, nw_ref, o_ref,
                qs_ref, ks_ref, vs_ref, state_ref, *, n_chunks):
    hp = GDN_HEADS_PER_STEP
    dh = HEAD_DIM
    c = GDN_CHUNK
    lblk = q_ref.shape[0]
    sb = pl.program_id(2)

    @pl.when(sb == 0)
    def _():
        zero8 = jnp.zeros((8, hp * dh), F32)
        qs_ref[0:8, :] = zero8
        ks_ref[0:8, :] = zero8
        vs_ref[0:8, :] = zero8
        state_ref[...] = jnp.zeros(state_ref.shape, F32)

    srcs = (q_ref, k_ref, v_ref)
    dsts = (qs_ref, ks_ref, vs_ref)
    xs = [src[...].astype(F32) for src in srcs]
    prevs = [dst[0:8, :] for dst in dsts]
    ys = [xs[i] * cw_ref[i][3:4, :] for i in range(3)]
    for k in range(1, GDN_CONV):
        ys = [ys[i] + _shift_rows(xs[i], prevs[i], k) * cw_ref[i][3 - k:4 - k, :] for i in range(3)]
    for i in range(3):
        dsts[i][8:, :] = _silu(ys[i])
        dsts[i][0:8, :] = xs[i][lblk - 8:, :]

    sl = GDN_SUPER
    cps = sl // c
    row = lax.broadcasted_iota(jnp.int32, (sl, sl), 0)
    col = lax.broadcasted_iota(jnp.int32, (sl, sl), 1)
    same = (row // c) == (col // c)
    incl = same & (row >= col)
    strict = same & (row > col)
    tril = jnp.where(incl, 1.0, 0.0).astype(BF16)
    ones_blk = jnp.where(same, 1.0, 0.0).astype(BF16)
    arow = arow_ref[0]
    dtrow = dtrow_ref[0]
    nw = nw_ref[...]
    states = [state_ref[r] for r in range(hp)]
    zeros_c = jnp.zeros((c, dh), BF16)

    for sci in range(lblk // sl):
        r0 = sci * sl
        gt = gt_ref[r0:r0 + sl, :]
        beta_all = _sigmoid(gt)
        g_all = -arow * _softplus(gt + dtrow)
        gh, gm, gl = _split3(g_all)
        gc_all = _dot(tril, gh) + _dot(tril, gm) + _dot(tril, gl)
        gend_all = _dot(ones_blk, gh) + _dot(ones_blk, gm) + _dot(ones_blk, gl)
        gc_t = gc_all.T
        hr = range(hp)
        hs = [slice(r * dh, (r + 1) * dh) for r in hr]
        qh = [qs_ref[r0 + 8:r0 + 8 + sl, hs[r]] for r in hr]
        kh = [ks_ref[r0 + 8:r0 + 8 + sl, hs[r]] for r in hr]
        vh = [vs_ref[r0 + 8:r0 + 8 + sl, hs[r]] for r in hr]
        qn = [qh[r] * lax.rsqrt(jnp.sum(qh[r] * qh[r], axis=-1, keepdims=True) + EPS) * (dh ** -0.5) for r in hr]
        kn = [kh[r] * lax.rsqrt(jnp.sum(kh[r] * kh[r], axis=-1, keepdims=True) + EPS) for r in hr]
        beta = [beta_all[:, r:r + 1] for r in hr]
        gc_col = [gc_all[:, hp + r:hp + r + 1] for r in hr]
        gc_row = [gc_t[hp + r:hp + r + 1, :] for r in hr]
        gend = [gend_all[:, hp + r:hp + r + 1] for r in hr]
        kb = [kn[r] * beta[r] for r in hr]
        kn_b = [kn[r].astype(BF16) for r in hr]
        kk = [_dot_nt(kb[r].astype(BF16), kn_b[r]) for r in hr]
        qk = [_dot_nt(qn[r].astype(BF16), kn_b[r]) for r in hr]
        decay = [jnp.where(incl, jnp.exp(jnp.where(incl, gc_col[r] - gc_row[r], 0.0)), 0.0) for r in hr]
        eg = [jnp.exp(gc_col[r]) for r in hr]
        lmat = [jnp.where(strict, kk[r] * decay[r], 0.0) for r in hr]
        a_in = [jnp.where(incl, qk[r] * decay[r], 0.0).astype(BF16) for r in hr]
        heads = [(jnp.concatenate([vh[r] * beta[r], kb[r] * eg[r]], axis=1), -lmat[r],
                  (qn[r] * eg[r]).astype(BF16), kn[r] * jnp.exp(gend[r] - gc_col[r]), a_in[r],
                  jnp.exp(gend[r])) for r in hr]
        xs = [h[0] for h in heads]
        mps = [h[1] for h in heads]
        for it in range(6):
            for r in range(hp):
                mp_b = mps[r].astype(BF16)
                xs[r] = xs[r] + _dot(mp_b, xs[r].astype(BF16))
                if it < 5:
                    mps[r] = _dot(mp_b, mp_b)
        pre = []
        for r in range(hp):
            x = xs[r]
            _, _, qd_b, kd, a_intra, egend_r = heads[r]
            w_b = x[:, dh:].astype(BF16)
            pre.append((x[:, :dh],
                        [jnp.concatenate([w_b[ci * c:(ci + 1) * c], qd_b[ci * c:(ci + 1) * c]], axis=0)
                         for ci in range(cps)],
                        [kd[ci * c:(ci + 1) * c].T.astype(BF16) for ci in range(cps)],
                        a_intra, egend_r))

        for ci in range(cps):
            rows = slice(ci * c, (ci + 1) * c)
            wqs = [_dot(pre[r][1][ci], states[r].astype(BF16)) for r in hr]
            v_new_b = [(pre[r][0][rows] - wqs[r][:c]).astype(BF16) for r in hr]
            vcat = [jnp.concatenate([zeros_c] * ci + [v_new_b[r]] + [zeros_c] * (cps - 1 - ci), axis=0)
                    for r in hr]
            o_new = [wqs[r][c:] + _dot(pre[r][3][rows], vcat[r]) for r in hr]
            states = [states[r] * pre[r][4][ci * c:ci * c + 1, :] + _dot(pre[r][2][ci], v_new_b[r])
                      for r in hr]
            for r in hr:
                o = o_new[r]
                ms = jnp.mean(o * o, axis=-1, keepdims=True)
                zz = z_ref[r0 + ci * c:r0 + (ci + 1) * c, hs[r]].astype(F32)
                o = o * lax.rsqrt(ms + EPS) * nw * _silu(zz)
                o_ref[r0 + ci * c:r0 + (ci + 1) * c, hs[r]] = o.astype(o_ref.dtype)

    for r in range(hp):
        state_ref[r] = states[r]


def gdn_mixer(proj, gates, conv_w, a_log, dt_bias, norm_w, *, batch, seq, lblk):
    hp = GDN_HEADS_PER_STEP
    dh = HEAD_DIM
    n_heads = a_log.shape[0]
    ng = n_heads // hp
    wd = n_heads * dh
    bw = hp * dh
    nsb = seq // lblk
    arow = jnp.zeros((ng, 1, 128), F32).at[:, 0, hp:2 * hp].set(jnp.exp(a_log.astype(F32)).reshape(ng, hp))
    dtrow = jnp.zeros((ng, 1, 128), F32).at[:, 0, hp:2 * hp].set(dt_bias.astype(F32).reshape(ng, hp))
    cw = conv_w.reshape(GDN_CONV, 3, ng, bw).transpose(1, 2, 0, 3)
    cw = jnp.pad(cw, ((0, 0), (0, 0), (0, 8 - GDN_CONV), (0, 0)))
    kern = functools.partial(_gdn_kernel, n_chunks=lblk // GDN_CHUNK)
    row_map = lambda off: (lambda b, g, s: (b * nsb + s, off + g))
    return pl.pallas_call(
        kern,
        grid=(batch, ng, nsb),
        in_specs=[
            pl.BlockSpec((lblk, bw), row_map(0)),
            pl.BlockSpec((lblk, bw), row_map(ng)),
            pl.BlockSpec((lblk, bw), row_map(2 * ng)),
            pl.BlockSpec((lblk, bw), row_map(3 * ng)),
            pl.BlockSpec((lblk, 128), row_map(0)),
            pl.BlockSpec((3, None, 8, bw), lambda b, g, s: (0, g, 0, 0)),
            pl.BlockSpec((None, 1, 128), lambda b, g, s: (g, 0, 0)),
            pl.BlockSpec((None, 1, 128), lambda b, g, s: (g, 0, 0)),
            pl.BlockSpec((1, dh), lambda b, g, s: (0, 0)),
        ],
        out_specs=pl.BlockSpec((lblk, bw), row_map(0)),
        out_shape=jax.ShapeDtypeStruct((batch * seq, wd), BF16),
        scratch_shapes=[
            pltpu.VMEM((lblk + 8, bw), F32),
            pltpu.VMEM((lblk + 8, bw), F32),
            pltpu.VMEM((lblk + 8, bw), F32),
            pltpu.VMEM((hp, dh, dh), F32),
        ],
        compiler_params=_cparams(("parallel", "parallel", "arbitrary")),
        name="gdn_mixer",
    )(proj, proj, proj, proj, gates, cw, arow, dtrow, norm_w.reshape(1, dh).astype(F32))


def _nsa_compress_kernel(t_ref, pos_ref, w1_ref, w2_ref, o_ref):
    nc = t_ref.shape[0] // CMP_STRIDE
    dh = HEAD_DIM
    acc_a = jnp.zeros((nc, dh), F32)
    acc_b = jnp.zeros((nc, dh), F32)
    for i in range(CMP_STRIDE):
        xi = t_ref[pl.ds(i, nc, stride=CMP_STRIDE), :]
        xa = (xi + pos_ref[i:i + 1, :]).astype(BF16)
        xb = (xi + pos_ref[CMP_STRIDE + i:CMP_STRIDE + i + 1, :]).astype(BF16)
        acc_a = acc_a + _dot(xa, w1_ref[i * dh:(i + 1) * dh, :])
        acc_b = acc_b + _dot(xb, w1_ref[(CMP_STRIDE + i) * dh:(CMP_STRIDE + i + 1) * dh, :])
    h = acc_a + pltpu.roll(acc_b, nc - 1, 0)
    o_ref[...] = _dot(_silu(h).astype(BF16), w2_ref[...]).astype(o_ref.dtype)


def nsa_compress(aux, pos, w1, w2, *, batch, seq, n_groups):
    dh = HEAD_DIM
    nc = seq // CMP_STRIDE
    return pl.pallas_call(
        _nsa_compress_kernel,
        grid=(batch, 2, n_groups),
        in_specs=[
            pl.BlockSpec((seq, dh), lambda b, j, g: (b, j * n_groups + g)),
            pl.BlockSpec((None, CMP_LEN, dh), lambda b, j, g: (j, 0, 0)),
            pl.BlockSpec((None, CMP_LEN * dh, dh), lambda b, j, g: (j, 0, 0)),
            pl.BlockSpec((None, dh, dh), lambda b, j, g: (j, 0, 0)),
        ],
        out_specs=pl.BlockSpec((None, None, None, nc, dh), lambda b, j, g: (j, b, g, 0, 0)),
        out_shape=jax.ShapeDtypeStruct((2, batch, n_groups, nc, dh), BF16),
        compiler_params=_cparams(("parallel", "parallel", "parallel")),
        name="nsa_compress",
    )(aux, pos, w1, w2)


def _t5_bucket_np(n):
    max_exact = NUM_BUCKETS // 2
    nf = np.maximum(n, 1).astype(np.float32)
    logv = np.log(nf / np.float32(max_exact)) / np.float32(math.log(MAX_DISTANCE / max_exact))
    large = max_exact + (logv * np.float32(NUM_BUCKETS - max_exact)).astype(np.int32)
    large = np.minimum(large, NUM_BUCKETS - 1)
    return np.where(n < max_exact, n, large).astype(np.int32)


def _t5_thresholds():
    n = np.arange(0, MAX_DISTANCE + 1, dtype=np.int32)
    b = _t5_bucket_np(n)
    half = NUM_BUCKETS // 2
    return tuple(int(np.min(n[b >= half + k])) for k in range(1, NUM_BUCKETS - half))


def _nsa_bias_kernel(tab_ref, dtab_ref, tt_ref, *, thresholds, nc):
    h = pl.program_id(0)
    qb = NSA_QB
    half = NUM_BUCKETS // 2
    c31 = tab_ref[NUM_BUCKETS - 1, h]

    def lookup(dist):
        n = jnp.maximum(dist, 0)
        big = jnp.full(n.shape, half, jnp.int32)
        for t in thresholds:
            big = big + jnp.where(n >= t, 1, 0)
        bucket = jnp.where(n < half, n, big)
        val = jnp.zeros(n.shape, F32)
        for b in range(NUM_BUCKETS):
            val = jnp.where(bucket == b, tab_ref[b, h], val)
        return val

    q = lax.broadcasted_iota(jnp.int32, (qb, qb), 0)
    kk = lax.broadcasted_iota(jnp.int32, (qb, qb), 1)
    dtab_ref[0] = jnp.where(q >= kk, (lookup(q - kk) - c31) * LOG2E, NEG)
    dtab_ref[1] = (lookup(q - kk + qb) - c31) * LOG2E
    dtab_ref[2] = jnp.where(kk > q, 0.0, NEG)
    dtab_ref[3] = jnp.full((qb, qb), NEG, F32)
    nw = 2 * qb // CMP_STRIDE
    x = lax.broadcasted_iota(jnp.int32, (nw, qb), 0)
    ql = lax.broadcasted_iota(jnp.int32, (nw, qb), 1)
    dist = ql - CMP_STRIDE * x + (qb - CMP_LEN + 1)
    tt_ref[0:nc, :] = jnp.zeros((nc, qb), F32) + c31 * LOG2E
    tt_ref[nc:nc + nw, :] = jnp.where(dist >= 0, lookup(dist) * LOG2E, NEG)
    tt_ref[nc + nw:, :] = jnp.full((nc, qb), NEG, F32)


def nsa_bias_tables(rel_bias, seq):
    qb = NSA_QB
    nc = seq // CMP_STRIDE
    nw = 2 * qb // CMP_STRIDE
    n_heads = rel_bias.shape[1]
    kern = functools.partial(_nsa_bias_kernel, thresholds=_t5_thresholds(), nc=nc)
    return pl.pallas_call(
        kern,
        grid=(n_heads,),
        in_specs=[pl.BlockSpec(memory_space=pltpu.SMEM)],
        out_specs=[
            pl.BlockSpec((None, 4, qb, qb), lambda h: (h, 0, 0, 0)),
            pl.BlockSpec((None, 2 * nc + nw, qb), lambda h: (h, 0, 0)),
        ],
        out_shape=[jax.ShapeDtypeStruct((n_heads, 4, qb, qb), F32),
                   jax.ShapeDtypeStruct((n_heads, 2 * nc + nw, qb), F32)],
        compiler_params=_cparams(("parallel",)),
        name="nsa_bias_tables",
    )(rel_bias.astype(F32))


def _sel_map_t(seq):
    nc = seq // CMP_STRIDE
    n_slc = seq // SLC_BLOCK
    c_start = np.arange(nc, dtype=np.int32) * CMP_STRIDE
    c_end = c_start + CMP_LEN - 1
    s_start = np.arange(n_slc, dtype=np.int32) * SLC_BLOCK
    m = (c_start[None, :] < s_start[:, None] + SLC_BLOCK) & (s_start[:, None] <= c_end[None, :])
    return m.astype(np.float32)


def _nsa_attn_kernel(cst_ref, q_ref, ksl_ref, vsl_ref, kw_ref, vw_ref, kc_ref, vc_ref, gt_ref,
                     dtab_ref, tt_ref, smt_ref, o_ref,
                     kaug_ref, vaug_ref, qaug_ref, vct_ref, p_ref, m_ref, alpha_ref,
                     acc_ref, *, n_sel):
    qb = NSA_QB
    dh = HEAD_DIM
    grp = NSA_GROUP
    nrows = grp * qb
    rblk = NSA_ROWBLK
    g = pl.program_id(1)
    qi = pl.program_id(2)
    seq = ksl_ref.shape[0]
    nc = kc_ref.shape[0]
    n_slc = seq // SLC_BLOCK
    blocks_per_chunk = qb // SLC_BLOCK

    @pl.when(qi == 0)
    def _():
        lane = lax.broadcasted_iota(jnp.int32, (seq, dh), 1)
        rb = lax.broadcasted_iota(jnp.int32, (seq, dh), 0) // SLC_BLOCK
        ones_tail = jnp.where(lane >= dh - 2, 1.0, 0.0)
        kaug_ref[0, :, 0:dh] = ksl_ref[...]
        kaug_ref[0, :, dh:] = (jnp.where(rb == lane, -SEL_PENALTY, 0.0) + ones_tail).astype(BF16)
        kaug_ref[1, :, 0:dh] = kw_ref[...]
        kaug_ref[1, :, dh:] = ones_tail.astype(BF16)
        ones = jnp.ones((seq, dh), BF16)
        vaug_ref[0, :, 0:dh] = vsl_ref[...]
        vaug_ref[0, :, dh:] = ones
        vaug_ref[1, :, 0:dh] = vw_ref[...]
        vaug_ref[1, :, dh:] = ones
        vct_ref[...] = vc_ref[...].astype(F32).T.astype(BF16)

    lane1 = lax.broadcasted_iota(jnp.int32, (1, dh), 1)
    cvec = []
    for r in range(grp):
        h = g * grp + r
        cvec.append(jnp.where(lane1 == dh - 2, cst_ref[0, h], jnp.where(lane1 == dh - 1, cst_ref[1, h], 0.0)))
        qaug_ref[r * qb:(r + 1) * qb, 0:dh] = q_ref[:, r * dh:(r + 1) * dh]
        qaug_ref[r * qb:(r + 1) * qb, dh:] = jnp.broadcast_to(cvec[r], (qb, dh)).astype(BF16)
    q4 = qaug_ref[:, 0:dh]

    m_ref[...] = jnp.full(m_ref.shape, NEG, F32)
    acc_ref[...] = jnp.zeros(acc_ref.shape, F32)

    def flash_steps(steps):
        svals = []
        for n, (_, branch, kchunk, _) in enumerate(steps):
            k0 = pl.multiple_of(kchunk * qb, qb)
            svals.append(_dot_nt(qaug_ref[...], kaug_ref[branch, pl.ds(k0, qb), :]))
        for rbi in range(nrows // rblk):
            rows = slice(rbi * rblk, (rbi + 1) * rblk)
            for n, (a, branch, kchunk, tile_idx) in enumerate(steps):
                s0 = svals[n][rows, 0:dh]
                s1 = svals[n][rows, dh:]
                if tile_idx is not None:
                    head = (rbi * rblk) // qb
                    t0 = (rbi * rblk) % qb
                    s0 = s0 + dtab_ref[head, tile_idx, t0:t0 + rblk, 0:dh]
                    s1 = s1 + dtab_ref[head, tile_idx, t0:t0 + rblk, dh:]
                m_old = m_ref[a, rows, :]
                m_new = jnp.maximum(m_old, jnp.max(jnp.maximum(s0, s1), axis=-1, keepdims=True))
                alpha_ref[n, rows, :] = jnp.exp2(m_old - m_new)
                p_ref[n, rows, 0:dh] = jnp.exp2(s0 - m_new).astype(BF16)
                p_ref[n, rows, dh:] = jnp.exp2(s1 - m_new).astype(BF16)
                m_ref[a, rows, :] = m_new
        pvs = []
        for n, (_, branch, kchunk, _) in enumerate(steps):
            k0 = pl.multiple_of(kchunk * qb, qb)
            pvs.append(_dot(p_ref[n], vaug_ref[branch, pl.ds(k0, qb), :]))
        for n, (a, _, _, _) in enumerate(steps):
            alpha = alpha_ref[n]
            acc_ref[a, :, 0:dh] = acc_ref[a, :, 0:dh] * alpha + pvs[n][:, 0:dh]
            acc_ref[a, :, dh:] = acc_ref[a, :, dh:] * alpha + pvs[n][:, dh:]

    def near(kchunk, tile):
        return jnp.maximum(kchunk, 0), jnp.where(kchunk >= 0, tile, 3)

    k_s1, t_s1 = near(qi - 1, 1)
    k_w2, t_w2 = near(qi - 2, 2)
    flash_steps([(1, 1, k_w2, t_w2), (1, 1, k_s1, t_s1), (1, 1, qi, 0)])

    st = _dot_nt(kc_ref[...], q4)
    start = pl.multiple_of(nc + CMP_STRIDE - (qb // CMP_STRIDE) * qi, CMP_STRIDE)
    gr = range(grp)
    bias = [tt_ref[r, pl.ds(start, nc), :] for r in gr]
    sc = [st[:, r * qb:(r + 1) * qb] + bias[r] for r in gr]
    mx = [jnp.max(sc[r], axis=0, keepdims=True) for r in gr]
    pc = [jnp.where(bias[r] > 0.5 * NEG, jnp.exp2(sc[r] - mx[r]), 0.0) for r in gr]
    den = [jnp.maximum(jnp.sum(pc[r], axis=0, keepdims=True), 1e-30) for r in gr]
    pc = [pc[r] / den[r] for r in gr]
    psum = pc[0]
    for r in range(1, grp):
        psum = psum + pc[r]
    o_cmp = [_dot(vct_ref[...], pc[r].astype(BF16)) for r in gr]
    o_cmp = [o_cmp[r].T for r in gr]

    ph = psum.astype(BF16)
    pl_ = (psum - ph.astype(F32)).astype(BF16)
    smt = smt_ref[...]
    imp = _dot(smt, ph) + _dot(smt, pl_)
    jb = lax.broadcasted_iota(jnp.int32, (n_slc, qb), 0)
    tb = qi * blocks_per_chunk + lax.broadcasted_iota(jnp.int32, (n_slc, qb), 1) // SLC_BLOCK
    forced = (jb == 0) | (jb == tb) | (jb == tb - 1)
    score = jnp.where(jb <= tb, jnp.where(forced, jnp.inf, imp), -jnp.inf)
    jbf = jb.astype(F32)
    taken = forced & (jb <= tb)
    notsel = jnp.where(taken, 0.0, 1.0)
    score = jnp.where(taken, -jnp.inf, score)
    for _ in range(max(n_sel - 3, 0)):
        top = jnp.max(score, axis=0, keepdims=True)
        first = jnp.min(jnp.where(score == top, jbf, float(n_slc)), axis=0, keepdims=True)
        pick = (jbf == first) & (top > -jnp.inf)
        notsel = jnp.where(pick, 0.0, notsel)
        score = jnp.where(pick, -jnp.inf, score)
    notsel = jnp.concatenate([notsel, jnp.zeros((dh - n_slc, qb), F32)], axis=0).T
    for r in range(grp):
        qaug_ref[r * qb:(r + 1) * qb, dh:] = (notsel + cvec[r]).astype(BF16)

    n_far = jnp.maximum(qi - 1, 0)

    def pair_body(i, carry):
        flash_steps([(0, 0, 2 * i, None), (0, 0, 2 * i + 1, None)])
        return carry

    lax.fori_loop(0, n_far // 2, pair_body, 0)

    @pl.when(n_far % 2 == 1)
    def _():
        flash_steps([(0, 0, n_far - 1, None)])

    gates = _sigmoid(gt_ref[...])
    gate = [[jnp.broadcast_to(gates[:, 3 * r + br:3 * r + br + 1], (qb, dh)) for br in range(3)]
            for r in gr]
    o_gc = [gate[r][0] * o_cmp[r] for r in gr]

    flash_steps([(0, 0, k_s1, t_s1), (0, 0, qi, 0)])

    o_sel = [acc_ref[0, r * qb:(r + 1) * qb, 0:dh] / acc_ref[0, r * qb:(r + 1) * qb, dh:] for r in gr]
    o_win = [acc_ref[1, r * qb:(r + 1) * qb, 0:dh] / acc_ref[1, r * qb:(r + 1) * qb, dh:] for r in gr]
    for r in gr:
        o = o_gc[r] + gate[r][1] * o_sel[r] + gate[r][2] * o_win[r]
        o_ref[:, r * dh:(r + 1) * dh] = o.astype(o_ref.dtype)


def nsa_attention(proj, aux, kvc, rel_bias, *, batch, seq, n_groups):
    qb = NSA_QB
    dh = HEAD_DIM
    grp = NSA_GROUP
    nq = seq // qb
    nc = seq // CMP_STRIDE
    n_slc = seq // SLC_BLOCK
    n_sel = min(SLC_TOPK, n_slc)
    assert n_slc <= dh - 2 and n_sel >= 3
    nrows = grp * qb
    dtab, tt = nsa_bias_tables(rel_bias, seq)
    smt = jnp.asarray(_sel_map_t(seq), BF16)
    c31 = rel_bias[NUM_BUCKETS - 1].astype(F32) * LOG2E
    c31_hi = c31.astype(BF16).astype(F32)
    cst = jnp.stack([c31_hi, c31 - c31_hi])
    kv_base = n_groups * grp
    kv_map = lambda j: (lambda b, g, i, c: (b, kv_base + j * n_groups + g))
    grid_spec = pltpu.PrefetchScalarGridSpec(
        num_scalar_prefetch=1,
        grid=(batch, n_groups, nq),
        in_specs=[
            pl.BlockSpec((qb, grp * dh), lambda b, g, i, c: (b * nq + i, g)),
            pl.BlockSpec((seq, dh), kv_map(0)),
            pl.BlockSpec((seq, dh), kv_map(1)),
            pl.BlockSpec((seq, dh), kv_map(2)),
            pl.BlockSpec((seq, dh), kv_map(3)),
            pl.BlockSpec((None, None, None, nc, dh), lambda b, g, i, c: (0, b, g, 0, 0)),
            pl.BlockSpec((None, None, None, nc, dh), lambda b, g, i, c: (1, b, g, 0, 0)),
            pl.BlockSpec((qb, 128), lambda b, g, i, c: (b * nq + i, 2 * n_groups + g)),
            pl.BlockSpec((grp, 4, qb, qb), lambda b, g, i, c: (g, 0, 0, 0)),
            pl.BlockSpec((grp, tt.shape[1], qb), lambda b, g, i, c: (g, 0, 0)),
            pl.BlockSpec((n_slc, nc), lambda b, g, i, c: (0, 0)),
        ],
        out_specs=pl.BlockSpec((qb, grp * dh), lambda b, g, i, c: (b * nq + i, g)),
        scratch_shapes=[
            pltpu.VMEM((2, seq, 2 * dh), BF16),
            pltpu.VMEM((2, seq, 2 * dh), BF16),
            pltpu.VMEM((nrows, 2 * dh), BF16),
            pltpu.VMEM((dh, nc), BF16),
            pltpu.VMEM((NSA_STEPS, nrows, qb), BF16),
            pltpu.VMEM((2, nrows, dh), F32),
            pltpu.VMEM((NSA_STEPS, nrows, dh), F32),
            pltpu.VMEM((2, nrows, 2 * dh), F32),
        ],
    )
    return pl.pallas_call(
        functools.partial(_nsa_attn_kernel, n_sel=n_sel),
        grid_spec=grid_spec,
        out_shape=jax.ShapeDtypeStruct((batch * seq, n_groups * grp * dh), BF16),
        compiler_params=_cparams(("parallel", "parallel", "arbitrary")),
        name="nsa_attention",
    )(cst, proj, proj, proj, proj, proj, kvc, kvc, aux, dtab, tt, smt)


def _gdn_in_weights(w_in, n_heads):
    hp = GDN_HEADS_PER_STEP
    wd = n_heads * HEAD_DIM
    ng = n_heads // hp
    main = jnp.concatenate([w_in[:, :4 * wd], w_in[:, 4 * wd + 2 * n_heads:]], axis=1)
    wb = w_in[:, 4 * wd:4 * wd + n_heads].reshape(-1, ng, hp)
    wa = w_in[:, 4 * wd + n_heads:4 * wd + 2 * n_heads].reshape(-1, ng, hp)
    gate = jnp.concatenate([wb, wa, jnp.zeros((w_in.shape[0], ng, 128 - 2 * hp), w_in.dtype)], axis=2)
    return main.astype(BF16), gate.reshape(w_in.shape[0], ng * 128).astype(BF16)


def _nsa_in_weights(w_in, n_heads):
    grp = NSA_GROUP
    ng = n_heads // grp
    qw = n_heads * HEAD_DIM
    kvw = ng * HEAD_DIM
    main = jnp.concatenate([w_in[:, :qw] * (HEAD_DIM ** -0.5 * LOG2E), w_in[:, qw + 2 * kvw:qw + 6 * kvw],
                            w_in[:, qw + 6 * kvw + 3 * n_heads:]], axis=1)
    cmp_w = w_in[:, qw:qw + 2 * kvw]
    wg = w_in[:, qw + 6 * kvw:qw + 6 * kvw + 3 * n_heads].reshape(-1, ng, 3 * grp)
    gate = jnp.concatenate([wg, jnp.zeros((w_in.shape[0], ng, 128 - 3 * grp), w_in.dtype)], axis=2)
    aux = jnp.concatenate([cmp_w, gate.reshape(w_in.shape[0], ng * 128)], axis=1)
    return main.astype(BF16), aux.astype(BF16)


def _pick(n, candidates):
    for c in candidates:
        if n % c == 0:
            return c
    return n


def kernel(x, mem, rel_bias, norm_mix_w, norm_ffn_w, final_norm_w, mem_norm_w, mem_w_kv, w_out, gdn_w_in, gdn_conv_w, gdn_a_log, gdn_dt_bias, gdn_norm_w, nsa_w_in, nsa_cmp_pos_k, nsa_cmp_w1_k, nsa_cmp_w2_k, nsa_cmp_pos_v, nsa_cmp_w1_v, nsa_cmp_w2_v, ffn_w_up, ffn_conv_w, ffn_conv_b, ffn_w_down):
    batch, seq, d_model = x.shape
    depth = norm_mix_w.shape[0]
    n_heads = d_model // HEAD_DIM
    m_tok = mem.shape[1]
    mw = MEM_HEADS * HEAD_DIM
    d_ff = ffn_w_down.shape[1]
    t = batch * seq
    xf = x.reshape(t, d_model)
    memf = mem.reshape(batch * m_tok, d_model)
    tm = _pick(seq, (1024, 512, 256, 128))
    tm_small = _pick(seq, (512, 256, 128))

    for i in range(depth):
        j = i // 2
        kvw = mem_w_kv[i].astype(BF16)
        kv, _ = norm_matmul(memf, mem_norm_w[i], kvw, kvw[:, :128],
                            tm=_pick(batch * m_tok, (512, 256)), tn=_pick(kvw.shape[1], (512, 256)))
        if i % 2 == 0:
            w_main, w_aux = _gdn_in_weights(gdn_w_in[j], n_heads)
            proj, aux = norm_matmul(xf, norm_mix_w[i], w_main, w_aux, tm=tm_small,
                                    tn=_pick(w_main.shape[1], (512, 256)))
            mix = gdn_mixer(proj, aux, gdn_conv_w[j], gdn_a_log[j], gdn_dt_bias[j], gdn_norm_w[j],
                            batch=batch, seq=seq, lblk=_pick(seq, (256,)))
            qm_block = 4 * n_heads * HEAD_DIM // mw
        else:
            ng = n_heads // NSA_GROUP
            w_main, w_aux = _nsa_in_weights(nsa_w_in[j], n_heads)
            proj, aux = norm_matmul(xf, norm_mix_w[i], w_main, w_aux, tm=tm_small,
                                    tn=_pick(w_main.shape[1], (512, 256)))
            pos = jnp.stack([nsa_cmp_pos_k[j], nsa_cmp_pos_v[j]]).astype(F32)
            w1 = jnp.stack([nsa_cmp_w1_k[j], nsa_cmp_w1_v[j]]).astype(BF16)
            w2 = jnp.stack([nsa_cmp_w2_k[j], nsa_cmp_w2_v[j]]).astype(BF16)
            kvc = nsa_compress(aux, pos, w1, w2, batch=batch, seq=seq, n_groups=ng)
            mix = nsa_attention(proj, aux, kvc, rel_bias, batch=batch, seq=seq, n_groups=ng)
            qm_block = (n_heads * HEAD_DIM + 4 * ng * HEAD_DIM) // mw
        mo = mem_attention(proj, kv, batch=batch, seq=seq, q_col_block=qm_block, ts=tm)
        xf, act = mix_ffn_up(xf, mix, mo, w_out[i].astype(BF16), norm_ffn_w[i], ffn_w_up[i].astype(BF16),
                             ffn_conv_w[i], ffn_conv_b[i], seq=seq, tm=tm_small,
                             tn=_pick(d_ff, (256, 128)))
        xf = ffn_down(xf, act, ffn_w_down[i].astype(BF16), final_norm_w, tm=tm_small,
                      final_norm=(i == depth - 1))
    return xf.reshape(batch, seq, d_model)
```
